```python
import jax, jax.numpy as jnp
from jax import lax
import numpy as np

D_MODEL = 1024
BATCH = 16
SEQ = 256
DEPTH = 4
DEC_BATCH = 8
DEC_SEQ = 2048
PAST_LEN = 256

GRID_W = 64
N_EVEN = (DEPTH + 1) // 2
N_ODD = DEPTH // 2
D_FF = 4 * D_MODEL
EPS = 1e-6
ROPE_BASE = 10000.0
CHUNK = 64
Q_BLOCK = 128
H_A = 4
DK_A = 64
DV_A = 128
GATE_RANK = 16
GATE_NORM = 16.0
H_B = 4
DK_B = 64
DV_B = 128
H_C = 8
Q_LORA = 256
KV_LORA = 256
QK_NOPE = 128
QK_ROPE = 64
V_HEAD_C = 128
A_QK = H_A * DK_A
A_V = H_A * DV_A
B_QK = H_B * DK_B
B_V = H_B * DV_B
EVEN_SPLITS = (A_QK, A_QK, A_V, A_V, 2 * GATE_RANK, B_QK, B_QK, B_V, B_V)
EVEN_IN = 2 * A_QK + 2 * A_V + 2 * GATE_RANK + 2 * B_QK + 2 * B_V
EVEN_OUT = A_V + B_V
ODD_SPLITS = (Q_LORA, KV_LORA, QK_ROPE)
ODD_IN = Q_LORA + KV_LORA + QK_ROPE

kernel_name = 'hybrid_gla_retnet_mla_diffusion_step'


def rms_norm(x, g=None):
    xf = x.astype(jnp.float32)
    y = xf * lax.rsqrt(jnp.mean(jnp.square(xf), axis=-1, keepdims=True) + EPS)
    if g is not None:
        y = y * g.astype(jnp.float32)
    return y.astype(x.dtype)


def split_cols(x, sizes):
    return jnp.split(x, np.cumsum(sizes)[:-1].tolist(), axis=-1)


def ada_mod(cond, w_ada, b_ada):
    m = jnp.einsum('bd,de->be', jax.nn.silu(cond), w_ada) + b_ada
    return jnp.split(m[:, None, :], 6, axis=-1)


def pre_norm_mod(x, g, shift, scale):
    return rms_norm(x, g) * (1.0 + scale) + shift


def post_norm_residual(x, y, g, gate):
    return x + gate * rms_norm(y, g)


def sq_relu_mlp(h, w1, w2):
    u = jax.nn.relu(jnp.einsum('bld,df->blf', h, w1))
    return jnp.einsum('blf,fd->bld', jnp.square(u), w2)


def axial_rope(rows, rot_dim):
    row = jnp.repeat(jnp.arange(rows), GRID_W).astype(jnp.float32)
    col = jnp.tile(jnp.arange(GRID_W), rows).astype(jnp.float32)
    n_freq = rot_dim // 4
    inv = ROPE_BASE ** (-jnp.arange(n_freq, dtype=jnp.float32) / n_freq)
    ang = jnp.concatenate([row[:, None] * inv, col[:, None] * inv], axis=-1)
    return jnp.cos(ang), jnp.sin(ang)


def apply_rope(x, cos, sin):
    half = x.shape[-1] // 2
    xf = x.astype(jnp.float32)
    x1, x2 = xf[..., :half], xf[..., half:]
    c, s = cos[None, :, None, :], sin[None, :, None, :]
    return jnp.concatenate([x1 * c - x2 * s, x1 * s + x2 * c], axis=-1).astype(x.dtype)


def chunked_gated_scan(q, k, v, log_a, s0):
    nb, L, H, _ = q.shape
    dv = v.shape[-1]
    n = L // CHUNK

    def to_chunks(t):
        t = t.astype(jnp.float32).reshape(nb, n, CHUNK, H, t.shape[-1])
        return jnp.transpose(t, (1, 0, 3, 2, 4))

    qc, kc, vc, gc = to_chunks(q), to_chunks(k), to_chunks(v), to_chunks(log_a)
    causal_in_chunk = jnp.tril(jnp.ones((CHUNK, CHUNK), dtype=bool))

    def step(S, inp):
        qi, ki, vi, gi = inp
        b = jnp.cumsum(gi, axis=2)
        b_last = b[:, :, -1:, :]
        q_dec = qi * jnp.exp(b)
        k_inv = ki * jnp.exp(-b)
        a = jnp.where(causal_in_chunk, jnp.einsum('bhik,bhjk->bhij', q_dec, k_inv), 0.0)
        o = jnp.einsum('bhik,bhkv->bhiv', q_dec, S) + jnp.einsum('bhij,bhjv->bhiv', a, vi)
        k_up = ki * jnp.exp(b_last - b)
        S_new = jnp.exp(b_last[:, :, 0, :])[..., None] * S + jnp.einsum('bhjk,bhjv->bhkv', k_up, vi)
        return S_new, o

    S_fin, oc = lax.scan(step, s0.astype(jnp.float32), (qc, kc, vc, gc))
    o = jnp.transpose(oc, (1, 0, 3, 2, 4)).reshape(nb, L, H, dv)
    return o.astype(q.dtype), S_fin


def bidir_scan(q, k, v, log_a_fwd, log_a_bwd, s_fwd, s_bwd):
    flip = lambda t: jnp.flip(t, axis=1)
    o_f, S_f = chunked_gated_scan(q, k, v, log_a_fwd, s_fwd)
    o_b, S_b = chunked_gated_scan(flip(q), flip(k), flip(v), flip(log_a_bwd), s_bwd)
    return o_f + flip(o_b), S_f, S_b


def even_mixer(h, w_in, w_gk2, b_gk2, gla_norm, ret_decay, w_out, s_gla, s_ret, rope):
    nb, L, _ = h.shape
    proj = jnp.einsum('bld,de->ble', h, w_in)
    qa, ka, va, ga, gk_lr, qb, kb, vb, gb = split_cols(proj, EVEN_SPLITS)
    qa = qa.reshape(nb, L, H_A, DK_A) * DK_A ** -0.5
    ka = ka.reshape(nb, L, H_A, DK_A)
    va = va.reshape(nb, L, H_A, DV_A)
    gate_pre = jnp.einsum('bldr,drk->bldk', gk_lr.reshape(nb, L, 2, GATE_RANK), w_gk2) + b_gk2
    log_a = (jax.nn.log_sigmoid(gate_pre.astype(jnp.float32)) / GATE_NORM).reshape(nb, L, 2, H_A, DK_A)
    o_a, sa_f, sa_b = bidir_scan(qa, ka, va, log_a[:, :, 0], log_a[:, :, 1], s_gla[:, 0], s_gla[:, 1])
    o_a = rms_norm(o_a, gla_norm).reshape(nb, L, A_V) * jax.nn.silu(ga)
    qb = qb.reshape(nb, L, H_B, DK_B)
    kb = kb.reshape(nb, L, H_B, DK_B) * DK_B ** -0.5
    vb = vb.reshape(nb, L, H_B, DV_B)
    if rope is not None:
        qb, kb = apply_rope(qb, *rope), apply_rope(kb, *rope)
    log_g = -jnp.exp(ret_decay.astype(jnp.float32))
    g_f = jnp.broadcast_to(log_g[0][:, None], (nb, L, H_B, DK_B))
    g_b = jnp.broadcast_to(log_g[1][:, None], (nb, L, H_B, DK_B))
    o_b, sb_f, sb_b = bidir_scan(qb, kb, vb, g_f, g_b, s_ret[:, 0], s_ret[:, 1])
    o_b = rms_norm(o_b).reshape(nb, L, B_V) * jax.nn.silu(gb)
    y = jnp.einsum('ble,ed->bld', jnp.concatenate([o_a, o_b], axis=-1), w_out)
    return y, jnp.stack([sa_f, sa_b], axis=1), jnp.stack([sb_f, sb_b], axis=1)


def blocked_attention(q, k, v):
    nb, Lq, H, dq = q.shape
    dv = v.shape[-1]
    scale = dq ** -0.5
    kf, vf = k.astype(jnp.float32), v.astype(jnp.float32)
    qblocks = jnp.moveaxis(q.reshape(nb, Lq // Q_BLOCK, Q_BLOCK, H, dq), 1, 0)

    def one_block(qi):
        s = jnp.einsum('bqhd,bkhd->bhqk', qi.astype(jnp.float32), kf) * scale
        p = jax.nn.softmax(s, axis=-1)
        return jnp.einsum('bhqk,bkhv->bqhv', p, vf).astype(q.dtype)

    out = lax.map(one_block, qblocks)
    return jnp.moveaxis(out, 0, 1).reshape(nb, Lq, H, dv)


def mla_expand(ckv, kpe, w_kv_b, rope):
    nb, L, _ = ckv.shape
    kv = jnp.einsum('blr,re->ble', ckv, w_kv_b).reshape(nb, L, H_C, QK_NOPE + V_HEAD_C)
    k_pe = kpe[:, :, None, :]
    if rope is not None:
        k_pe = apply_rope(k_pe, *rope)
    k = jnp.concatenate([kv[..., :QK_NOPE], jnp.broadcast_to(k_pe, (nb, L, H_C, QK_ROPE))], axis=-1)
    return k, kv[..., QK_NOPE:]


def mla_mixer(h, w_in, q_a_norm, w_q_b, kv_a_norm, w_kv_b, w_out, rope, ckv_ctx=None, kpe_ctx=None):
    nb, L, _ = h.shape
    q_lat, ckv, kpe = split_cols(jnp.einsum('bld,de->ble', h, w_in), ODD_SPLITS)
    q = jnp.einsum('blr,re->ble', rms_norm(q_lat, q_a_norm), w_q_b).reshape(nb, L, H_C, QK_NOPE + QK_ROPE)
    ckv = rms_norm(ckv, kv_a_norm)
    k, v = mla_expand(ckv, kpe, w_kv_b, rope)
    if rope is not None:
        q = jnp.concatenate([q[..., :QK_NOPE], apply_rope(q[..., QK_NOPE:], *rope)], axis=-1)
    if ckv_ctx is not None:
        k_c, v_c = mla_expand(ckv_ctx, kpe_ctx, w_kv_b, None)
        k = jnp.concatenate([k_c, k], axis=1)
        v = jnp.concatenate([v_c, v], axis=1)
    o = blocked_attention(q, k, v)
    y = jnp.einsum('ble,ed->bld', o.reshape(nb, L, H_C * V_HEAD_C), w_out)
    return y, ckv, kpe


def setup_inputs(seed: int = 0) -> dict:
    key = jax.random.key(seed)
    ks = iter(jax.random.split(key, 32))
    nrm = lambda shape, scale: jax.random.normal(next(ks), shape, jnp.float32) * scale
    gain = lambda shape: 1.0 + nrm(shape, 0.05)
    decay_init = jnp.log(-jnp.log1p(-(2.0 ** (-5.0 - jnp.arange(H_B, dtype=jnp.float32)))))
    return {
        'x_prompt': nrm((BATCH, SEQ, D_MODEL), 1.0),
        'x_sample': nrm((DEC_BATCH, DEC_SEQ, D_MODEL), 1.0),
        'cache_ckv': nrm((DEC_BATCH, N_ODD, PAST_LEN, KV_LORA), 1.0),
        'cache_kpe': nrm((DEC_BATCH, N_ODD, PAST_LEN, QK_ROPE), 1.0),
        'state_gla': nrm((DEC_BATCH, N_EVEN, 2, H_A, DK_A, DV_A), 1.0),
        'state_ret': nrm((DEC_BATCH, N_EVEN, 2, H_B, DK_B, DV_B), 1.0),
        'c': nrm((DEC_BATCH, D_MODEL), 1.0),
        'c_ctx': nrm((D_MODEL,), 1.0),
        'w_ada': nrm((DEPTH, D_MODEL, 6 * D_MODEL), 0.5 * D_MODEL ** -0.5),
        'b_ada': nrm((DEPTH, 6 * D_MODEL), 0.02),
        'norm_mix_pre': gain((DEPTH, D_MODEL)),
        'norm_mix_post': gain((DEPTH, D_MODEL)),
        'norm_mlp_pre': gain((DEPTH, D_MODEL)),
        'norm_mlp_post': gain((DEPTH, D_MODEL)),
        'w_in_even': nrm((N_EVEN, D_MODEL, EVEN_IN), D_MODEL ** -0.5),
        'w_gk2': nrm((N_EVEN, 2, GATE_RANK, A_QK), GATE_RANK ** -0.5),
        'b_gk2': nrm((N_EVEN, 2, A_QK), 0.1),
        'gla_norm': gain((N_EVEN, DV_A)),
        'ret_decay': decay_init + nrm((N_EVEN, 2, H_B), 0.05),
        'w_out_even': nrm((N_EVEN, EVEN_OUT, D_MODEL), EVEN_OUT ** -0.5),
        'w_in_odd': nrm((N_ODD, D_MODEL, ODD_IN), D_MODEL ** -0.5),
        'q_a_norm': gain((N_ODD, Q_LORA)),
        'w_q_b': nrm((N_ODD, Q_LORA, H_C * (QK_NOPE + QK_ROPE)), Q_LORA ** -0.5),
        'kv_a_norm': gain((N_ODD, KV_LORA)),
        'w_kv_b': nrm((N_ODD, KV_LORA, H_C * (QK_NOPE + V_HEAD_C)), KV_LORA ** -0.5),
        'w_out_odd': nrm((N_ODD, H_C * V_HEAD_C, D_MODEL), (H_C * V_HEAD_C) ** -0.5),
        'w_mlp1': nrm((DEPTH, D_MODEL, D_FF), D_MODEL ** -0.5),
        'w_mlp2': nrm((DEPTH, D_FF, D_MODEL), D_FF ** -0.5),
    }


def reference(x_prompt, x_sample, cache_ckv, cache_kpe, state_gla, state_ret, c, c_ctx,
              w_ada, b_ada, norm_mix_pre, norm_mix_post, norm_mlp_pre, norm_mlp_post,
              w_in_even, w_gk2, b_gk2, gla_norm, ret_decay, w_out_even,
              w_in_odd, q_a_norm, w_q_b, kv_a_norm, w_kv_b, w_out_odd,
              w_mlp1, w_mlp2):
    n_lat = x_sample.shape[1]
    rows = n_lat // GRID_W
    rope_ret = axial_rope(rows, DK_B)
    rope_mla = axial_rope(rows, QK_ROPE)
    nb_ctx = x_prompt.shape[0]
    zero_gla = jnp.zeros((nb_ctx, 2, H_A, DK_A, DV_A), jnp.float32)
    zero_ret = jnp.zeros((nb_ctx, 2, H_B, DK_B, DV_B), jnp.float32)

    xc, xl = x_prompt, x_sample
    new_ckv, new_kpe, new_gla, new_ret = [], [], [], []
    for l in range(DEPTH):
        i = l // 2
        sh1c, sc1c, gt1c, sh2c, sc2c, gt2c = ada_mod(c_ctx[None, :], w_ada[l], b_ada[l])
        sh1l, sc1l, gt1l, sh2l, sc2l, gt2l = ada_mod(c, w_ada[l], b_ada[l])
        hc = pre_norm_mod(xc, norm_mix_pre[l], sh1c, sc1c)
        hl = pre_norm_mod(xl, norm_mix_pre[l], sh1l, sc1l)
        if l % 2 == 0:
            ew = (w_in_even[i], w_gk2[i], b_gk2[i], gla_norm[i], ret_decay[i], w_out_even[i])
            yc, sg, sr = even_mixer(hc, *ew, zero_gla, zero_ret, None)
            yl, _, _ = even_mixer(hl, *ew, state_gla[:, i], state_ret[:, i], rope_ret)
            new_gla.append(sg)
            new_ret.append(sr)
        else:
            ow = (w_in_odd[i], q_a_norm[i], w_q_b[i], kv_a_norm[i], w_kv_b[i], w_out_odd[i])
            yc, ckv_c, kpe_c = mla_mixer(hc, *ow, None)
            yl, _, _ = mla_mixer(hl, *ow, rope_mla, cache_ckv[:, i], cache_kpe[:, i])
            new_ckv.append(ckv_c)
            new_kpe.append(kpe_c)
        xc = post_norm_residual(xc, yc, norm_mix_post[l], gt1c)
        xl = post_norm_residual(xl, yl, norm_mix_post[l], gt1l)
        hc = pre_norm_mod(xc, norm_mlp_pre[l], sh2c, sc2c)
        hl = pre_norm_mod(xl, norm_mlp_pre[l], sh2l, sc2l)
        xc = post_norm_residual(xc, sq_relu_mlp(hc, w_mlp1[l], w_mlp2[l]), norm_mlp_post[l], gt2c)
        xl = post_norm_residual(xl, sq_relu_mlp(hl, w_mlp1[l], w_mlp2[l]), norm_mlp_post[l], gt2l)

    return (xc, xl, jnp.stack(new_ckv, axis=1), jnp.stack(new_kpe, axis=1),
            jnp.stack(new_gla, axis=1), jnp.stack(new_ret, axis=1))
```

```python
import functools

import numpy as np
import jax
import jax.numpy as jnp
from jax import lax
from jax.experimental import pallas as pl
from jax.experimental.pallas import tpu as pltpu

F32 = jnp.float32
BF16 = jnp.bfloat16

EPS = 1e-6
ROPE_BASE = 10000.0
GRID_W = 64
CHUNK = 64
GATE_RANK = 16
GATE_NORM = 16.0
N_HEAD_SCAN = 4
DK = 64
DV = 128
H_C = 8
Q_LORA = 256
KV_LORA = 256
QK_NOPE = 128
QK_ROPE = 64
V_HEAD_C = 128
LANE = 128
MOD_ROWS = 16

VMEM_LIMIT = 56 * 1024 * 1024


def _cparams(sem):
    return pltpu.CompilerParams(dimension_semantics=sem, vmem_limit_bytes=VMEM_LIMIT)


def _dot(a, b):
    return jnp.dot(a, b, preferred_element_type=F32)


def _dot_nt(a, b):
    return lax.dot_general(a, b, (((1,), (1,)), ((), ())), preferred_element_type=F32)


def _dot_tn(a, b):
    return lax.dot_general(a, b, (((0,), (0,)), ((), ())), preferred_element_type=F32)


def _rms(x):
    return x * lax.rsqrt(jnp.mean(x * x, axis=-1, keepdims=True) + EPS)


def _silu(x):
    return x * jax.nn.sigmoid(x)


def _full(shape):
    n = len(shape)
    return pl.BlockSpec(shape, lambda *_: (0,) * n)


def _ada_body(cond_ref, w_ref, b_ref, o_ref):
    s = _silu(cond_ref[...]).astype(BF16)
    o_ref[...] = _dot(s, w_ref[...].astype(BF16)) + b_ref[...]


def _ada_call(cond, w_ada, b_ada):
    depth, d, n = w_ada.shape
    tn = 1536
    return pl.pallas_call(
        _ada_body,
        grid=(depth, n // tn),
        in_specs=[
            pl.BlockSpec((MOD_ROWS, d), lambda l, j: (0, 0)),
            pl.BlockSpec((None, d, tn), lambda l, j: (l, 0, j)),
            pl.BlockSpec((None, 1, tn), lambda l, j: (l, 0, j)),
        ],
        out_specs=pl.BlockSpec((None, MOD_ROWS, tn), lambda l, j: (l, 0, j)),
        out_shape=jax.ShapeDtypeStruct((depth, MOD_ROWS, n), F32),
        compiler_params=_cparams(("arbitrary", "arbitrary")),
        name="ada_mod",
    )(cond, w_ada, b_ada.reshape(depth, 1, n))


class _Tokens:
    def __init__(self, nb_ctx, l_ctx, nb_lat, l_lat, d):
        self.nb_ctx, self.l_ctx, self.nb_lat, self.l_lat, self.d = nb_ctx, l_ctx, nb_lat, l_lat, d
        self.n_ctx = nb_ctx * l_ctx
        self.n_lat = nb_lat * l_lat
        self.n = self.n_ctx + self.n_lat
        self.ctx_row = nb_lat

    def tile(self, want):
        t = want
        while self.n_ctx % t or self.l_lat % t:
            t //= 2
        return t

    def mod_spec(self, layer, chunk, tm):
        n_ctx, l_lat, ctx_row = self.n_ctx, self.l_lat, self.ctx_row

        def idx(i, *_):
            start = i * tm
            row = jnp.where(start < n_ctx, ctx_row, (start - n_ctx) // l_lat)
            return (layer, row, chunk, 0, 0)

        return pl.BlockSpec((None, None, None, 1, self.d), idx)


A_QK = N_HEAD_SCAN * DK
A_V = N_HEAD_SCAN * DV
E_QA, E_KA, E_VA, E_GA = 0, A_QK, 2 * A_QK, 2 * A_QK + A_V
E_QB = E_GA + A_V
E_KB = E_QB + A_QK
E_VB = E_KB + A_QK
E_GB = E_VB + A_V
E_GK = E_GB + A_V
E_QBS = E_GK + LANE
E_KBS = E_QBS + A_QK
E_COLS = E_KBS + A_QK


def _log_sigmoid(x):
    return jnp.minimum(x, 0.0) - jnp.log1p(jnp.exp(-jnp.abs(x)))


def _even_proj_body(x_ref, g_ref, sh_ref, sc_ref, w_ref, wg_ref, bg_ref, dec_ref, cc_ref, ss_ref,
                    q_ref, k_ref, lf_ref, lb_ref, gt_ref, v_ref):
    tm = x_ref.shape[0]
    h = (_rms(x_ref[...]) * g_ref[...] * (1.0 + sc_ref[...]) + sh_ref[...]).astype(BF16)

    def proj(start, width):
        return _dot(h, w_ref[:, start:start + width])

    cc = cc_ref[...]
    ss = ss_ref[...]
    scale = DK ** -0.5
    q_ref[:, 0:A_QK] = proj(E_QA, A_QK) * scale
    k_ref[:, 0:A_QK] = proj(E_KA, A_QK)
    for j in range(A_QK // LANE):
        o = j * LANE
        q_ref[:, A_QK + o:A_QK + o + LANE] = proj(E_QB + o, LANE) * cc + proj(E_QBS + o, LANE) * ss
        k_ref[:, A_QK + o:A_QK + o + LANE] = (proj(E_KB + o, LANE) * cc + proj(E_KBS + o, LANE) * ss) * scale
    v_ref[:, 0:A_V] = proj(E_VA, A_V).astype(BF16)
    v_ref[:, A_V:2 * A_V] = proj(E_VB, A_V).astype(BF16)
    gt_ref[:, 0:A_V] = proj(E_GA, A_V)
    gt_ref[:, A_V:2 * A_V] = proj(E_GB, A_V)
    gk = proj(E_GK, LANE).astype(BF16)
    la = _log_sigmoid(_dot(gk, wg_ref[...]) + bg_ref[...]) * (1.0 / GATE_NORM)
    lf_ref[:, 0:A_QK] = la[:, 0:A_QK]
    lb_ref[:, 0:A_QK] = la[:, A_QK:2 * A_QK]
    log_g = -jnp.exp(dec_ref[...])
    lf_ref[:, A_QK:2 * A_QK] = jnp.broadcast_to(log_g[0:1, :], (tm, A_QK))
    lb_ref[:, A_QK:2 * A_QK] = jnp.broadcast_to(log_g[1:2, :], (tm, A_QK))


def _even_proj_call(tok, x, mods, layer, g_pre, w, wg, bg, dec, cc, ss):
    tm = tok.tile(512)
    d = tok.d
    row = lambda i: (i, 0)
    outs = [
        jax.ShapeDtypeStruct((tok.n, 2 * A_QK), F32),
        jax.ShapeDtypeStruct((tok.n, 2 * A_QK), F32),
        jax.ShapeDtypeStruct((tok.n, 2 * A_QK), F32),
        jax.ShapeDtypeStruct((tok.n, 2 * A_QK), F32),
        jax.ShapeDtypeStruct((tok.n, 2 * A_V), F32),
        jax.ShapeDtypeStruct((tok.n, 2 * A_V), BF16),
    ]
    return pl.pallas_call(
        _even_proj_body,
        grid=(tok.n // tm,),
        in_specs=[
            pl.BlockSpec((tm, d), row),
            _full((1, d)),
            tok.mod_spec(layer, 0, tm),
            tok.mod_spec(layer, 1, tm),
            _full(w.shape), _full(wg.shape), _full(bg.shape), _full(dec.shape),
            pl.BlockSpec((tm, LANE), row),
            pl.BlockSpec((tm, LANE), row),
        ],
        out_specs=[pl.BlockSpec((tm, o.shape[1]), row) for o in outs],
        out_shape=outs,
        compiler_params=_cparams(("arbitrary",)),
        name="even_proj",
    )(x, g_pre, mods, mods, w, wg, bg, dec, cc, ss)


def _cumsum_rows(x):
    row = lax.broadcasted_iota(jnp.int32, x.shape, 0)
    s = 1
    while s < x.shape[0]:
        x = x + jnp.where(row >= s, pltpu.roll(x, s, axis=0), 0.0)
        s *= 2
    return x


def _scan_body(q_ref, k_ref, lf_ref, lb_ref, v_ref, gt_ref, gain_ref, s0_ref, *rest, seq_len, emit_state):
    if emit_state:
        m_ref, sfin_ref, st_ref, o_acc = rest
    else:
        m_ref, st_ref, o_acc = rest
    C = CHUNK
    n = seq_len // C
    pair_w = 2 * DK
    lane = lax.broadcasted_iota(jnp.int32, (C, pair_w), 1)
    head0 = lane < DK
    head0_state = lax.broadcasted_iota(jnp.int32, (DV, pair_w), 1) < DK
    r2 = lax.broadcasted_iota(jnp.int32, (2 * C, C), 0) % C
    c2 = lax.broadcasted_iota(jnp.int32, (2 * C, C), 1)
    keep_fwd = r2 >= c2
    keep_bwd = r2 <= c2

    st_ref[...] = s0_ref[...]

    def one_direction(r0, d, log_ref, keep, reverse):
        g = log_ref[pl.ds(r0, C), :]
        b = _cumsum_rows(g)
        tot = b[C - 1:C, :]
        if reverse:
            b = tot - b + g
        q = q_ref[pl.ds(r0, C), :]
        k = k_ref[pl.ds(r0, C), :]
        vc = v_ref[pl.ds(r0, C), :]
        q_dec = q * jnp.exp(b)
        k_inv = (k * jnp.exp(-b)).astype(BF16)
        k_up = (k * jnp.exp(tot - b)).astype(BF16)
        qs = jnp.concatenate([jnp.where(head0, q_dec, 0.0), jnp.where(head0, 0.0, q_dec)], axis=0).astype(BF16)
        a = jnp.where(keep, _dot_nt(qs, k_inv), 0.0).astype(BF16)
        st = st_ref[d]
        o1 = _dot_nt(qs, st.astype(BF16))
        o2 = _dot(a, vc)
        o = jnp.concatenate([o1[0:C] + o2[0:C, 0:DV], o1[C:2 * C] + o2[C:2 * C, DV:2 * DV]], axis=1)
        r = _dot_tn(vc, k_up)
        upd = jnp.where(head0_state, r[0:DV], r[DV:2 * DV])
        st_ref[d] = st * jnp.exp(tot) + upd
        return o

    def body(c, carry, accumulate):
        rf = pl.multiple_of(c * C, C)
        rb = pl.multiple_of((n - 1 - c) * C, C)
        o_f = one_direction(rf, 0, lf_ref, keep_fwd, False)
        o_b = one_direction(rb, 1, lb_ref, keep_bwd, True)
        if accumulate:
            o_acc[pl.ds(rf, C), :] += o_f
            o_acc[pl.ds(rb, C), :] += o_b
        else:
            o_acc[pl.ds(rf, C), :] = o_f
            o_acc[pl.ds(rb, C), :] = o_b
        return carry

    lax.fori_loop(0, n // 2, functools.partial(body, accumulate=False), 0)
    lax.fori_loop(n // 2, n, functools.partial(body, accumulate=True), 0)

    if emit_state:
        sfin_ref[...] = st_ref[...]

    tr = min(seq_len, 256)

    def finish(i, carry):
        r0 = pl.multiple_of(i * tr, tr)
        o = o_acc[pl.ds(r0, tr), :]
        gate = _silu(gt_ref[pl.ds(r0, tr), :])
        gain = gain_ref[...]
        for hh in range(2):
            sl = slice(hh * DV, (hh + 1) * DV)
            m_ref[pl.ds(r0, tr), sl] = (_rms(o[:, sl]) * gain[:, sl] * gate[:, sl]).astype(BF16)
        return carry

    lax.fori_loop(0, seq_len // tr, finish, 0)


def _scan_call(tok, group, q, k, lf, lb, v, gt, gain, s0t):
    if group == "ctx":
        nb, seq_len, blk0, emit_state = tok.nb_ctx, tok.l_ctx, 0, True
    else:
        nb, seq_len, blk0, emit_state = tok.nb_lat, tok.l_lat, tok.n_ctx // tok.l_lat, False
    pairs = 2 * N_HEAD_SCAN // 2
    seq = lambda b, p: (blk0 + b, p)
    st_spec = pl.BlockSpec((None, 2, None, DV, 2 * DK), lambda b, p: (b, 0, p, 0, 0))
    out_shape = [jax.ShapeDtypeStruct((nb * seq_len, 2 * A_V), BF16)]
    out_specs = [pl.BlockSpec((seq_len, 2 * DV), lambda b, p: (b, p))]
    if emit_state:
        out_shape.append(jax.ShapeDtypeStruct((nb, 2, pairs, DV, 2 * DK), F32))
        out_specs.append(st_spec)
    return pl.pallas_call(
        functools.partial(_scan_body, seq_len=seq_len, emit_state=emit_state),
        grid=(nb, pairs),
        in_specs=[
            pl.BlockSpec((seq_len, 2 * DK), seq),
            pl.BlockSpec((seq_len, 2 * DK), seq),
            pl.BlockSpec((seq_len, 2 * DK), seq),
            pl.BlockSpec((seq_len, 2 * DK), seq),
            pl.BlockSpec((seq_len, 2 * DV), seq),
            pl.BlockSpec((seq_len, 2 * DV), seq),
            pl.BlockSpec((1, 2 * DV), lambda b, p: (0, p)),
            st_spec,
        ],
        out_specs=out_specs,
        out_shape=out_shape,
        scratch_shapes=[pltpu.VMEM((2, DV, 2 * DK), F32), pltpu.VMEM((seq_len, 2 * DV), F32)],
        compiler_params=_cparams(("arbitrary", "arbitrary")),
        name="scan_" + group,
    )(q, k, lf, lb, v, gt, gain, s0t)


def _out_proj_body(m_ref, x_ref, w_ref, g_ref, gate_ref, o_ref):
    y = _dot(m_ref[...], w_ref[...])
    o_ref[...] = x_ref[...] + gate_ref[...] * (_rms(y) * g_ref[...])


def _out_proj_call(tok, m, x, mods, layer, w, g_post):
    tm = tok.tile(512)
    d = tok.d
    row = lambda i: (i, 0)
    return pl.pallas_call(
        _out_proj_body,
        grid=(tok.n // tm,),
        in_specs=[
            pl.BlockSpec((tm, m.shape[1]), row),
            pl.BlockSpec((tm, d), row),
            _full(w.shape),
            _full((1, d)),
            tok.mod_spec(layer, 2, tm),
        ],
        out_specs=pl.BlockSpec((tm, d), row),
        out_shape=jax.ShapeDtypeStruct((tok.n, d), F32),
        compiler_params=_cparams(("arbitrary",)),
        name="out_proj",
    )(m, x, w, g_post, mods)


def _mlp_body(x_ref, gpre_ref, sh_ref, sc_ref, w1_ref, w2_ref, gpost_ref, gate_ref, o_ref, h_ref, acc_ref):
    f = pl.program_id(1)

    @pl.when(f == 0)
    def _():
        h_ref[...] = (_rms(x_ref[...]) * gpre_ref[...] * (1.0 + sc_ref[...]) + sh_ref[...]).astype(BF16)

    u = jnp.maximum(_dot(h_ref[...], w1_ref[...]), 0.0)
    part = _dot((u * u).astype(BF16), w2_ref[...])

    @pl.when(f == 0)
    def _():
        acc_ref[...] = part

    @pl.when(f > 0)
    def _():
        acc_ref[...] += part

    @pl.when(f == pl.num_programs(1) - 1)
    def _():
        o_ref[...] = x_ref[...] + gate_ref[...] * (_rms(acc_ref[...]) * gpost_ref[...])


def _mlp_call(tok, x, mods, layer, g_pre, w1, w2, g_post):
    tm = tok.tile(1024)
    d = tok.d
    ff = w1.shape[1]
    tf = 1024
    row = lambda i, f: (i, 0)
    return pl.pallas_call(
        _mlp_body,
        grid=(tok.n // tm, ff // tf),
        in_specs=[
            pl.BlockSpec((tm, d), row),
            _full((1, d)),
            tok.mod_spec(layer, 3, tm),
            tok.mod_spec(layer, 4, tm),
            pl.BlockSpec((d, tf), lambda i, f: (0, f)),
            pl.BlockSpec((tf, d), lambda i, f: (f, 0)),
            _full((1, d)),
            tok.mod_spec(layer, 5, tm),
        ],
        out_specs=pl.BlockSpec((tm, d), row),
        out_shape=jax.ShapeDtypeStruct((tok.n, d), F32),
        scratch_shapes=[pltpu.VMEM((tm, d), BF16), pltpu.VMEM((tm, d), F32)],
        compiler_params=_cparams(("arbitrary", "arbitrary")),
        name="mlp",
    )(x, g_pre, mods, mods, w1, w2, g_post, mods)


HEAD_W = 2 * LANE
O_QLAT, O_CKV, O_KPE, O_KPES, O_COLS = 0, Q_LORA, Q_LORA + KV_LORA, Q_LORA + KV_LORA + LANE, Q_LORA + KV_LORA + 2 * LANE
QB_NOPE, QB_ROPE, QB_SWAP = 0, H_C * LANE, 2 * H_C * LANE


def _expand_kv(cb, kper, wkvb_ref, k_ref, v_ref):
    for hh in range(H_C):
        k_ref[:, hh * HEAD_W:hh * HEAD_W + LANE] = _dot(cb, wkvb_ref[:, hh * LANE:(hh + 1) * LANE]).astype(BF16)
        k_ref[:, hh * HEAD_W + LANE:(hh + 1) * HEAD_W] = kper
    v_ref[...] = _dot(cb, wkvb_ref[:, H_C * LANE:2 * H_C * LANE]).astype(BF16)


def _mla_proj_body(x_ref, g_ref, sh_ref, sc_ref, win_ref, qn_ref, wqb_ref, kvn_ref, wkvb_ref, cc_ref, ss_ref,
                   q_ref, k_ref, v_ref, ckv_ref, kpe_ref):
    h = (_rms(x_ref[...]) * g_ref[...] * (1.0 + sc_ref[...]) + sh_ref[...]).astype(BF16)
    cc = cc_ref[...]
    ss = ss_ref[...]
    qn = (_rms(_dot(h, win_ref[:, O_QLAT:O_QLAT + Q_LORA])) * qn_ref[...]).astype(BF16)
    ckvn = _rms(_dot(h, win_ref[:, O_CKV:O_CKV + KV_LORA])) * kvn_ref[...]
    kpe = _dot(h, win_ref[:, O_KPE:O_KPE + LANE])
    kper = (kpe * cc + _dot(h, win_ref[:, O_KPES:O_KPES + LANE]) * ss).astype(BF16)
    ckv_ref[...] = ckvn
    kpe_ref[...] = kpe
    for hh in range(H_C):
        o = hh * LANE
        q_ref[:, hh * HEAD_W:hh * HEAD_W + LANE] = _dot(qn, wqb_ref[:, QB_NOPE + o:QB_NOPE + o + LANE]).astype(BF16)
        rot = (_dot(qn, wqb_ref[:, QB_ROPE + o:QB_ROPE + o + LANE]) * cc
               + _dot(qn, wqb_ref[:, QB_SWAP + o:QB_SWAP + o + LANE]) * ss)
        q_ref[:, hh * HEAD_W + LANE:(hh + 1) * HEAD_W] = rot.astype(BF16)
    _expand_kv(ckvn.astype(BF16), kper, wkvb_ref, k_ref, v_ref)


def _mla_proj_call(tok, x, mods, layer, g_pre, win, qn, wqb, kvn, wkvb, cc, ss):
    tm = tok.tile(512)
    d = tok.d
    row = lambda i: (i, 0)
    outs = [
        jax.ShapeDtypeStruct((tok.n, H_C * HEAD_W), BF16),
        jax.ShapeDtypeStruct((tok.n, H_C * HEAD_W), BF16),
        jax.ShapeDtypeStruct((tok.n, H_C * V_HEAD_C), BF16),
        jax.ShapeDtypeStruct((tok.n, KV_LORA), F32),
        jax.ShapeDtypeStruct((tok.n, LANE), F32),
    ]
    return pl.pallas_call(
        _mla_proj_body,
        grid=(tok.n // tm,),
        in_specs=[
            pl.BlockSpec((tm, d), row),
            _full((1, d)),
            tok.mod_spec(layer, 0, tm),
            tok.mod_spec(layer, 1, tm),
            _full(win.shape), _full(qn.shape), _full(wqb.shape), _full(kvn.shape), _full(wkvb.shape),
            pl.BlockSpec((tm, LANE), row),
            pl.BlockSpec((tm, LANE), row),
        ],
        out_specs=[pl.BlockSpec((tm, o.shape[1]), row) for o in outs],
        out_shape=outs,
        compiler_params=_cparams(("arbitrary",)),
        name="mla_proj",
    )(x, g_pre, mods, mods, win, qn, wqb, kvn, wkvb, cc, ss)


def _cache_expand_body(ckv_ref, kpe_ref, wkvb_ref, k_ref, v_ref):
    _expand_kv(ckv_ref[...].astype(BF16), kpe_ref[...].astype(BF16), wkvb_ref, k_ref, v_ref)


def _cache_expand_call(ckv, kpe_pad, wkvb):
    n = ckv.shape[0]
    tm = 512
    while n % tm:
        tm //= 2
    row = lambda i: (i, 0)
    outs = [jax.ShapeDtypeStruct((n, H_C * HEAD_W), BF16), jax.ShapeDtypeStruct((n, H_C * V_HEAD_C), BF16)]
    return pl.pallas_call(
        _cache_expand_body,
        grid=(n // tm,),
        in_specs=[pl.BlockSpec((tm, KV_LORA), row), pl.BlockSpec((tm, LANE), row), _full(wkvb.shape)],
        out_specs=[pl.BlockSpec((tm, o.shape[1]), row) for o in outs],
        out_shape=outs,
        compiler_params=_cparams(("arbitrary",)),
        name="cache_expand",
    )(ckv, kpe_pad, wkvb)


def _attn_body(*refs, n_src):
    q_ref = refs[0]
    o_ref = refs[1 + 2 * n_src]
    scale = (QK_NOPE + QK_ROPE) ** -0.5
    q = q_ref[...]
    s = [_dot_nt(q, refs[1 + 2 * i][...]) * scale for i in range(n_src)]
    m = functools.reduce(jnp.maximum, [jnp.max(x, axis=-1, keepdims=True) for x in s])
    p = [jnp.exp(x - m) for x in s]
    den = functools.reduce(jnp.add, [jnp.sum(x, axis=-1, keepdims=True) for x in p])
    o = functools.reduce(jnp.add, [_dot(p[i].astype(BF16), refs[2 + 2 * i][...]) for i in range(n_src)])
    o_ref[...] = (o / den).astype(BF16)


def _attn_call(tok, group, q, k, v, k_cache=None, v_cache=None):
    if group == "ctx":
        nb, seq_len, blk0 = tok.nb_ctx, tok.l_ctx, 0
    else:
        nb, seq_len, blk0 = tok.nb_lat, tok.l_lat, tok.n_ctx // tok.l_lat
    tq = min(seq_len, 256)
    nq = seq_len // tq
    qblk0 = blk0 * nq
    in_specs = [pl.BlockSpec((tq, HEAD_W), lambda b, hh, i: (qblk0 + b * nq + i, hh))]
    args = [q]
    if k_cache is not None:
        past = k_cache.shape[0] // nb
        in_specs += [pl.BlockSpec((past, HEAD_W), lambda b, hh, i: (b, hh)),
                     pl.BlockSpec((past, V_HEAD_C), lambda b, hh, i: (b, hh))]
        args += [k_cache, v_cache]
    in_specs += [pl.BlockSpec((seq_len, HEAD_W), lambda b, hh, i: (blk0 + b, hh)),
                 pl.BlockSpec((seq_len, V_HEAD_C), lambda b, hh, i: (blk0 + b, hh))]
    args += [k, v]
    return pl.pallas_call(
        functools.partial(_attn_body, n_src=(len(args) - 1) // 2),
        grid=(nb, H_C, nq),
        in_specs=in_specs,
        out_specs=pl.BlockSpec((tq, V_HEAD_C), lambda b, hh, i: (b * nq + i, hh)),
        out_shape=jax.ShapeDtypeStruct((nb * seq_len, H_C * V_HEAD_C), BF16),
        compiler_params=_cparams(("arbitrary", "arbitrary", "arbitrary")),
        name="attn_" + group,
    )(*args)


def _swap_halves(w, head_dim):
    n = w.shape[-1]
    idx = np.arange(n)
    idx = (idx // head_dim) * head_dim + (idx % head_dim + head_dim // 2) % head_dim
    return w[..., idx]


def _pad_cols(w, width):
    return jnp.pad(w, ((0, 0), (0, width - w.shape[1])))


def _even_weights(w_in, w_gk2, b_gk2):
    sizes = (A_QK, A_QK, A_V, A_V, 2 * GATE_RANK, A_QK, A_QK, A_V, A_V)
    qa, ka, va, ga, gk, qb, kb, vb, gb = jnp.split(w_in, np.cumsum(sizes)[:-1].tolist(), axis=1)
    w = jnp.concatenate([qa, ka, va, ga, qb, kb, vb, gb, _pad_cols(gk, LANE),
                         _swap_halves(qb, DK), _swap_halves(kb, DK)], axis=1).astype(BF16)
    wg = jnp.zeros((LANE, 2 * A_QK), F32)
    wg = wg.at[0:GATE_RANK, 0:A_QK].set(w_gk2[0]).at[GATE_RANK:2 * GATE_RANK, A_QK:2 * A_QK].set(w_gk2[1])
    bg = b_gk2.reshape(1, 2 * A_QK)
    return w, wg.astype(BF16), bg


def _odd_weights(w_in, w_q_b, w_kv_b):
    q_lat, ckv, kpe = w_in[:, :Q_LORA], w_in[:, Q_LORA:Q_LORA + KV_LORA], w_in[:, Q_LORA + KV_LORA:]
    win = jnp.concatenate([q_lat, ckv, _pad_cols(kpe, LANE), _pad_cols(_swap_halves(kpe, QK_ROPE), LANE)],
                          axis=1).astype(BF16)
    wq = w_q_b.reshape(Q_LORA, H_C, QK_NOPE + QK_ROPE)
    nope = wq[:, :, :QK_NOPE].reshape(Q_LORA, H_C * QK_NOPE)
    rope = wq[:, :, QK_NOPE:]
    pad = lambda r: jnp.pad(r, ((0, 0), (0, 0), (0, LANE - QK_ROPE))).reshape(Q_LORA, H_C * LANE)
    wqb = jnp.concatenate([nope, pad(rope), pad(_swap_halves(rope, QK_ROPE))], axis=1).astype(BF16)
    wkv = w_kv_b.reshape(KV_LORA, H_C, QK_NOPE + V_HEAD_C)
    wkvb = jnp.concatenate([wkv[:, :, :QK_NOPE].reshape(KV_LORA, H_C * QK_NOPE),
                            wkv[:, :, QK_NOPE:].reshape(KV_LORA, H_C * V_HEAD_C)], axis=1).astype(BF16)
    return win, wqb, wkvb


def _rope_tables(tok):
    rows = tok.l_lat // GRID_W
    row = jnp.repeat(jnp.arange(rows), GRID_W).astype(F32)
    col = jnp.tile(jnp.arange(GRID_W), rows).astype(F32)
    n_freq = QK_ROPE // 4
    inv = ROPE_BASE ** (-jnp.arange(n_freq, dtype=F32) / n_freq)
    ang = jnp.concatenate([row[:, None] * inv, col[:, None] * inv], axis=-1)
    cos, sin = jnp.cos(ang), jnp.sin(ang)
    cc = jnp.tile(jnp.concatenate([cos, cos], axis=-1), (tok.nb_lat, LANE // QK_ROPE))
    ss = jnp.tile(jnp.concatenate([-sin, sin], axis=-1), (tok.nb_lat, LANE // QK_ROPE))
    cc = jnp.concatenate([jnp.ones((tok.n_ctx, LANE), F32), cc], axis=0)
    ss = jnp.concatenate([jnp.zeros((tok.n_ctx, LANE), F32), ss], axis=0)
    return cc, ss


def _states_to_kernel(s):
    nb, _, heads = s.shape[:3]
    s = s.reshape(nb, 2, heads // 2, 2, DK, DV)
    return jnp.transpose(s, (0, 1, 2, 5, 3, 4)).reshape(nb, 2, heads // 2, DV, 2 * DK)


def _states_from_kernel(st):
    nb, _, pairs = st.shape[:3]
    st = st.reshape(nb, 2, pairs, DV, 2, DK)
    return jnp.transpose(st, (0, 1, 2, 4, 5, 3)).reshape(nb, 2, 2 * pairs, DK, DV)


def kernel(x_prompt, x_sample, cache_ckv, cache_kpe, state_gla, state_ret, c, c_ctx, w_ada, b_ada, norm_mix_pre, norm_mix_post, norm_mlp_pre, norm_mlp_post, w_in_even, w_gk2, b_gk2, gla_norm, ret_decay, w_out_even, w_in_odd, q_a_norm, w_q_b, kv_a_norm, w_kv_b, w_out_odd, w_mlp1, w_mlp2):
    nb_ctx, l_ctx, d = x_prompt.shape
    nb_lat, l_lat, _ = x_sample.shape
    depth = w_ada.shape[0]
    tok = _Tokens(nb_ctx, l_ctx, nb_lat, l_lat, d)
    assert nb_lat < MOD_ROWS and tok.n_ctx % l_lat == 0 and l_ctx % CHUNK == 0 and l_lat % (2 * CHUNK) == 0

    cond = jnp.concatenate([c, c_ctx[None, :], jnp.zeros((MOD_ROWS - nb_lat - 1, d), F32)], axis=0)
    mods = _ada_call(cond, w_ada, b_ada).reshape(depth, MOD_ROWS, 6, 1, d)
    cc, ss = _rope_tables(tok)
    x = jnp.concatenate([x_prompt.reshape(tok.n_ctx, d), x_sample.reshape(tok.n_lat, d)], axis=0)
    vec = lambda a: a.reshape(1, -1)

    new_ckv, new_kpe, new_gla, new_ret = [], [], [], []
    for l in range(depth):
        i = l // 2
        if l % 2 == 0:
            w, wg, bg = _even_weights(w_in_even[i], w_gk2[i], b_gk2[i])
            dec = jnp.repeat(ret_decay[i], DK, axis=-1)
            q, k, lf, lb, gt, v = _even_proj_call(tok, x, mods, l, vec(norm_mix_pre[l]), w, wg, bg, dec, cc, ss)
            gain = jnp.concatenate([jnp.tile(gla_norm[i], N_HEAD_SCAN), jnp.ones((A_V,), F32)]).reshape(1, 2 * A_V)
            s0_lat = _states_to_kernel(jnp.concatenate([state_gla[:, i], state_ret[:, i]], axis=2))
            s0_ctx = jnp.zeros((nb_ctx,) + s0_lat.shape[1:], F32)
            m_ctx, s_fin = _scan_call(tok, "ctx", q, k, lf, lb, v, gt, gain, s0_ctx)
            (m_lat,) = _scan_call(tok, "lat", q, k, lf, lb, v, gt, gain, s0_lat)
            m = jnp.concatenate([m_ctx, m_lat], axis=0)
            s_fin = _states_from_kernel(s_fin)
            new_gla.append(s_fin[:, :, :N_HEAD_SCAN])
            new_ret.append(s_fin[:, :, N_HEAD_SCAN:])
            w_out = w_out_even[i].astype(BF16)
        else:
            win, wqb, wkvb = _odd_weights(w_in_odd[i], w_q_b[i], w_kv_b[i])
            q, k, v, ckv, kpe = _mla_proj_call(tok, x, mods, l, vec(norm_mix_pre[l]), win, vec(q_a_norm[i]), wqb,
                                               vec(kv_a_norm[i]), wkvb, cc, ss)
            past = cache_ckv.shape[2]
            kpe_pad = jnp.pad(cache_kpe[:, i].reshape(nb_lat * past, QK_ROPE), ((0, 0), (0, LANE - QK_ROPE)))
            k_c, v_c = _cache_expand_call(cache_ckv[:, i].reshape(nb_lat * past, KV_LORA), kpe_pad, wkvb)
            m_ctx = _attn_call(tok, "ctx", q, k, v)
            m_lat = _attn_call(tok, "lat", q, k, v, k_c, v_c)
            m = jnp.concatenate([m_ctx, m_lat], axis=0)
            new_ckv.append(ckv[:tok.n_ctx].reshape(nb_ctx, l_ctx, KV_LORA))
            new_kpe.append(kpe[:tok.n_ctx, :QK_ROPE].reshape(nb_ctx, l_ctx, QK_ROPE))
            w_out = w_out_odd[i].astype(BF16)
        x = _out_proj_call(tok, m, x, mods, l, w_out, vec(norm_mix_post[l]))
        x = _mlp_call(tok, x, mods, l, vec(norm_mlp_pre[l]), w_mlp1[l].astype(BF16), w_mlp2[l].astype(BF16),
                      vec(norm_mlp_post[l]))

    return (x[:tok.n_ctx].reshape(nb_ctx, l_ctx, d), x[tok.n_ctx:].reshape(nb_lat, l_lat, d),
            jnp.stack(new_ckv, axis=1), jnp.stack(new_kpe, axis=1),
            jnp.stack(new_gla, axis=1), jnp.stack(new_ret, axis=1))
```

```python
import functools

import numpy as np
import jax
import jax.numpy as jnp
from jax import lax
from jax.experimental import pallas as pl
from jax.experimental.pallas import tpu as pltpu

F32 = jnp.float32
BF16 = jnp.bfloat16

EPS = 1e-6
ROPE_BASE = 10000.0
GRID_W = 64
CHUNK = 64
GATE_RANK = 16
GATE_NORM = 16.0
N_HEAD_SCAN = 4
DK = 64
DV = 128
H_C = 8
Q_LORA = 256
KV_LORA = 256
QK_NOPE = 128
QK_ROPE = 64
V_HEAD_C = 128
LANE = 128
MOD_ROWS = 16

VMEM_LIMIT = 56 * 1024 * 1024


def _cparams(sem):
    return pltpu.CompilerParams(dimension_semantics=sem, vmem_limit_bytes=VMEM_LIMIT)


def _dot(a, b):
    return jnp.dot(a, b, preferred_element_type=F32)


def _dot_nt(a, b):
    return lax.dot_general(a, b, (((1,), (1,)), ((), ())), preferred_element_type=F32)


def _dot_tn(a, b):
    return lax.dot_general(a, b, (((0,), (0,)), ((), ())), preferred_element_type=F32)


def _rms(x):
    return x * lax.rsqrt(jnp.mean(x * x, axis=-1, keepdims=True) + EPS)


def _silu(x):
    return x * jax.nn.sigmoid(x)


def _full(shape):
    n = len(shape)
    return pl.BlockSpec(shape, lambda *_: (0,) * n)


def _ada_body(cond_ref, w_ref, b_ref, o_ref):
    s = _silu(cond_ref[...]).astype(BF16)
    o_ref[...] = _dot(s, w_ref[...].astype(BF16)) + b_ref[...]


def _ada_call(cond, w_ada, b_ada):
    depth, d, n = w_ada.shape
    tn = 1536
    return pl.pallas_call(
        _ada_body,
        grid=(depth, n // tn),
        in_specs=[
            pl.BlockSpec((MOD_ROWS, d), lambda l, j: (0, 0)),
            pl.BlockSpec((None, d, tn), lambda l, j: (l, 0, j)),
            pl.BlockSpec((None, 1, tn), lambda l, j: (l, 0, j)),
        ],
        out_specs=pl.BlockSpec((None, MOD_ROWS, tn), lambda l, j: (l, 0, j)),
        out_shape=jax.ShapeDtypeStruct((depth, MOD_ROWS, n), F32),
        compiler_params=_cparams(("arbitrary", "arbitrary")),
        name="ada_mod",
    )(cond, w_ada, b_ada.reshape(depth, 1, n))


class _Tokens:
    def __init__(self, nb_ctx, l_ctx, nb_lat, l_lat, d):
        self.nb_ctx, self.l_ctx, self.nb_lat, self.l_lat, self.d = nb_ctx, l_ctx, nb_lat, l_lat, d
        self.n_ctx = nb_ctx * l_ctx
        self.n_lat = nb_lat * l_lat
        self.n = self.n_ctx + self.n_lat
        self.ctx_row = nb_lat

    def tile(self, want):
        t = want
        while self.n_ctx % t or self.l_lat % t:
            t //= 2
        return t

    def mod_spec(self, layer, chunk, tm):
        n_ctx, l_lat, ctx_row = self.n_ctx, self.l_lat, self.ctx_row

        def idx(i, *_):
            start = i * tm
            row = jnp.where(start < n_ctx, ctx_row, (start - n_ctx) // l_lat)
            return (layer, row, chunk, 0, 0)

        return pl.BlockSpec((None, None, None, 1, self.d), idx)


A_QK = N_HEAD_SCAN * DK
A_V = N_HEAD_SCAN * DV
E_QA, E_KA, E_VA, E_GA = 0, A_QK, 2 * A_QK, 2 * A_QK + A_V
E_QB = E_GA + A_V
E_KB = E_QB + A_QK
E_VB = E_KB + A_QK
E_GB = E_VB + A_V
E_GK = E_GB + A_V
E_QBS = E_GK + LANE
E_KBS = E_QBS + A_QK
E_COLS = E_KBS + A_QK


def _log_sigmoid(x):
    return jnp.minimum(x, 0.0) - jnp.log1p(jnp.exp(-jnp.abs(x)))


def _even_proj_body(x_ref, g_ref, sh_ref, sc_ref, w_ref, wg_ref, bg_ref, dec_ref, cc_ref, ss_ref,
                    q_ref, k_ref, lf_ref, lb_ref, gt_ref, v_ref):
    tm = x_ref.shape[0]
    h = (_rms(x_ref[...]) * g_ref[...] * (1.0 + sc_ref[...]) + sh_ref[...]).astype(BF16)

    def proj(start, width):
        return _dot(h, w_ref[:, start:start + width])

    cc = cc_ref[...]
    ss = ss_ref[...]
    scale = DK ** -0.5
    q_ref[:, 0:A_QK] = proj(E_QA, A_QK) * scale
    k_ref[:, 0:A_QK] = proj(E_KA, A_QK)
    for j in range(A_QK // LANE):
        o = j * LANE
        q_ref[:, A_QK + o:A_QK + o + LANE] = proj(E_QB + o, LANE) * cc + proj(E_QBS + o, LANE) * ss
        k_ref[:, A_QK + o:A_QK + o + LANE] = (proj(E_KB + o, LANE) * cc + proj(E_KBS + o, LANE) * ss) * scale
    v_ref[:, 0:A_V] = proj(E_VA, A_V).astype(BF16)
    v_ref[:, A_V:2 * A_V] = proj(E_VB, A_V).astype(BF16)
    gt_ref[:, 0:A_V] = proj(E_GA, A_V)
    gt_ref[:, A_V:2 * A_V] = proj(E_GB, A_V)
    gk = proj(E_GK, LANE).astype(BF16)
    la = _log_sigmoid(_dot(gk, wg_ref[...]) + bg_ref[...]) * (1.0 / GATE_NORM)
    lf_ref[:, 0:A_QK] = la[:, 0:A_QK]
    lb_ref[:, 0:A_QK] = la[:, A_QK:2 * A_QK]
    log_g = -jnp.exp(dec_ref[...])
    lf_ref[:, A_QK:2 * A_QK] = jnp.broadcast_to(log_g[0:1, :], (tm, A_QK))
    lb_ref[:, A_QK:2 * A_QK] = jnp.broadcast_to(log_g[1:2, :], (tm, A_QK))


def _even_proj_call(tok, x, mods, layer, g_pre, w, wg, bg, dec, cc, ss):
    tm = tok.tile(512)
    d = tok.d
    row = lambda i: (i, 0)
    outs = [
        jax.ShapeDtypeStruct((tok.n, 2 * A_QK), F32),
        jax.ShapeDtypeStruct((tok.n, 2 * A_QK), F32),
        jax.ShapeDtypeStruct((tok.n, 2 * A_QK), F32),
        jax.ShapeDtypeStruct((tok.n, 2 * A_QK), F32),
        jax.ShapeDtypeStruct((tok.n, 2 * A_V), F32),
        jax.ShapeDtypeStruct((tok.n, 2 * A_V), BF16),
    ]
    return pl.pallas_call(
        _even_proj_body,
        grid=(tok.n // tm,),
        in_specs=[
            pl.BlockSpec((tm, d), row),
            _full((1, d)),
            tok.mod_spec(layer, 0, tm),
            tok.mod_spec(layer, 1, tm),
            _full(w.shape), _full(wg.shape), _full(bg.shape), _full(dec.shape),
            pl.BlockSpec((tm, LANE), row),
            pl.BlockSpec((tm, LANE), row),
        ],
        out_specs=[pl.BlockSpec((tm, o.shape[1]), row) for o in outs],
        out_shape=outs,
        compiler_params=_cparams(("arbitrary",)),
        name="even_proj",
    )(x, g_pre, mods, mods, w, wg, bg, dec, cc, ss)


SCAN_GROUP = 8


def _chunk_cumsum(x):
    row = lax.broadcasted_iota(jnp.int32, x.shape, 0) % CHUNK
    s = 1
    while s < CHUNK:
        x = x + jnp.where(row >= s, pltpu.roll(x, s, axis=0), 0.0)
        s *= 2
    return x


def _scan_body(q_ref, k_ref, lf_ref, lb_ref, v_ref, gt_ref, gain_ref, s0_ref, *rest, seq_len, group, emit_state):
    if emit_state:
        m_ref, sfin_ref, st_ref, o_acc = rest
    else:
        m_ref, st_ref, o_acc = rest
    C = CHUNK
    blk = group * C
    nblk = seq_len // blk
    pair_w = 2 * DK
    head0 = lax.broadcasted_iota(jnp.int32, (blk, pair_w), 1) < DK
    t_in = lax.broadcasted_iota(jnp.int32, (C, pair_w), 0)
    j_in = lax.broadcasted_iota(jnp.int32, (C, pair_w), 1) % DK
    keep_fwd = t_in >= j_in
    keep_bwd = t_in <= j_in
    on_diag = ((lax.broadcasted_iota(jnp.int32, (2 * DV, pair_w), 0) < DV)
               == (lax.broadcasted_iota(jnp.int32, (2 * DV, pair_w), 1) < DK))
    zeros_v = jnp.zeros((C, DV), BF16)

    st_ref[...] = s0_ref[...]

    def one_direction(r0, d, log_ref, keep, reverse):
        g = log_ref[pl.ds(r0, blk), :]
        b = _chunk_cumsum(g)
        tots = [b[C * j + C - 1:C * j + C, :] for j in range(group)]
        totb = jnp.concatenate([jnp.broadcast_to(t, (C, pair_w)) for t in tots], axis=0)
        if reverse:
            b = totb - b + g
        q = q_ref[pl.ds(r0, blk), :]
        k = k_ref[pl.ds(r0, blk), :]
        vblk = v_ref[pl.ds(r0, blk), :]
        q_dec = (q * jnp.exp(b)).astype(BF16)
        k_inv = k * jnp.exp(-b)
        k_up = (k * jnp.exp(totb - b)).astype(BF16)
        k_inv0 = jnp.where(head0, k_inv, 0.0).astype(BF16)
        k_inv1 = jnp.where(head0, 0.0, k_inv).astype(BF16)
        st = st_ref[d]
        outs = [None] * group
        for j in (reversed(range(group)) if reverse else range(group)):
            sl = slice(C * j, C * (j + 1))
            k_bd = jnp.concatenate([k_inv0[sl], k_inv1[sl]], axis=0)
            a = jnp.where(keep, _dot_nt(q_dec[sl], k_bd), 0.0).astype(BF16)
            vc = vblk[sl]
            v_bd = jnp.concatenate([jnp.concatenate([vc[:, :DV], zeros_v], axis=1),
                                    jnp.concatenate([zeros_v, vc[:, DV:]], axis=1)], axis=0)
            outs[j] = _dot_nt(q_dec[sl], st.astype(BF16)) + _dot(a, v_bd)
            st = st * jnp.exp(tots[j]) + jnp.where(on_diag, _dot_tn(vc, k_up[sl]), 0.0)
        st_ref[d] = st
        return jnp.concatenate(outs, axis=0)

    def body(c, carry, accumulate):
        rf = pl.multiple_of(c * blk, blk)
        rb = pl.multiple_of((nblk - 1 - c) * blk, blk)
        o_f = one_direction(rf, 0, lf_ref, keep_fwd, False)
        o_b = one_direction(rb, 1, lb_ref, keep_bwd, True)
        if accumulate:
            o_acc[pl.ds(rf, blk), :] += o_f
            o_acc[pl.ds(rb, blk), :] += o_b
        else:
            o_acc[pl.ds(rf, blk), :] = o_f
            o_acc[pl.ds(rb, blk), :] = o_b
        return carry

    lax.fori_loop(0, nblk // 2, functools.partial(body, accumulate=False), 0)
    lax.fori_loop(nblk // 2, nblk, functools.partial(body, accumulate=True), 0)

    if emit_state:
        sfin_ref[...] = st_ref[...]

    tr = min(seq_len, 256)

    def finish(i, carry):
        r0 = pl.multiple_of(i * tr, tr)
        o = o_acc[pl.ds(r0, tr), :]
        gate = _silu(gt_ref[pl.ds(r0, tr), :])
        gain = gain_ref[...]
        for hh in range(2):
            sl = slice(hh * DV, (hh + 1) * DV)
            m_ref[pl.ds(r0, tr), sl] = (_rms(o[:, sl]) * gain[:, sl] * gate[:, sl]).astype(BF16)
        return carry

    lax.fori_loop(0, seq_len // tr, finish, 0)


def _scan_body_aliased(*refs, **kw):
    _scan_body(*refs[:8], *refs[9:], **kw)


def _scan_call(tok, which, q, k, lf, lb, v, gt, gain, s0, out_prev=None):
    if which == "ctx":
        nb, seq_len, blk0, emit_state = tok.nb_ctx, tok.l_ctx, 0, True
    else:
        nb, seq_len, blk0, emit_state = tok.nb_lat, tok.l_lat, tok.n_ctx // tok.l_lat, False
    pairs = s0.shape[2]
    group = min(SCAN_GROUP, seq_len // CHUNK // 2)
    seq = lambda b, p: (blk0 + b, p)
    st_spec = pl.BlockSpec((None, 2, None, 2 * DV, 2 * DK), lambda b, p: (b, 0, p, 0, 0))
    in_specs = [
        pl.BlockSpec((seq_len, 2 * DK), seq),
        pl.BlockSpec((seq_len, 2 * DK), seq),
        pl.BlockSpec((seq_len, 2 * DK), seq),
        pl.BlockSpec((seq_len, 2 * DK), seq),
        pl.BlockSpec((seq_len, 2 * DV), seq),
        pl.BlockSpec((seq_len, 2 * DV), seq),
        pl.BlockSpec((1, 2 * DV), lambda b, p: (0, p)),
        st_spec,
    ]
    args = [q, k, lf, lb, v, gt, gain, s0]
    out_shape = [jax.ShapeDtypeStruct((tok.n, pairs * 2 * DV), BF16)]
    out_specs = [pl.BlockSpec((seq_len, 2 * DV), seq)]
    body, aliases = _scan_body, {}
    if emit_state:
        out_shape.append(jax.ShapeDtypeStruct(s0.shape, F32))
        out_specs.append(st_spec)
    if out_prev is not None:
        in_specs.append(pl.BlockSpec(memory_space=pl.ANY))
        args.append(out_prev)
        body, aliases = _scan_body_aliased, {len(args) - 1: 0}
    return pl.pallas_call(
        functools.partial(body, seq_len=seq_len, group=group, emit_state=emit_state),
        grid=(nb, pairs),
        in_specs=in_specs,
        out_specs=out_specs,
        out_shape=out_shape,
        input_output_aliases=aliases,
        scratch_shapes=[pltpu.VMEM((2, 2 * DV, 2 * DK), F32), pltpu.VMEM((seq_len, 2 * DV), F32)],
        compiler_params=_cparams(("arbitrary", "arbitrary")),
        name="scan_" + which,
    )(*args)


def _out_proj_body(m_ref, x_ref, w_ref, g_ref, gate_ref, o_ref):
    y = _dot(m_ref[...], w_ref[...])
    o_ref[...] = x_ref[...] + gate_ref[...] * (_rms(y) * g_ref[...])


def _out_proj_call(tok, m, x, mods, layer, w, g_post):
    tm = tok.tile(512)
    d = tok.d
    row = lambda i: (i, 0)
    return pl.pallas_call(
        _out_proj_body,
        grid=(tok.n // tm,),
        in_specs=[
            pl.BlockSpec((tm, m.shape[1]), row),
            pl.BlockSpec((tm, d), row),
            _full(w.shape),
            _full((1, d)),
            tok.mod_spec(layer, 2, tm),
        ],
        out_specs=pl.BlockSpec((tm, d), row),
        out_shape=jax.ShapeDtypeStruct((tok.n, d), F32),
        compiler_params=_cparams(("arbitrary",)),
        name="out_proj",
    )(m, x, w, g_post, mods)


def _mlp_body(x_ref, gpre_ref, sh_ref, sc_ref, w1_ref, w2_ref, gpost_ref, gate_ref, o_ref, h_ref, acc_ref):
    f = pl.program_id(1)

    @pl.when(f == 0)
    def _():
        h_ref[...] = (_rms(x_ref[...]) * gpre_ref[...] * (1.0 + sc_ref[...]) + sh_ref[...]).astype(BF16)

    u = jnp.maximum(_dot(h_ref[...], w1_ref[...]), 0.0)
    part = _dot((u * u).astype(BF16), w2_ref[...])

    @pl.when(f == 0)
    def _():
        acc_ref[...] = part

    @pl.when(f > 0)
    def _():
        acc_ref[...] += part

    @pl.when(f == pl.num_programs(1) - 1)
    def _():
        o_ref[...] = x_ref[...] + gate_ref[...] * (_rms(acc_ref[...]) * gpost_ref[...])


def _mlp_call(tok, x, mods, layer, g_pre, w1, w2, g_post):
    tm = tok.tile(1024)
    d = tok.d
    ff = w1.shape[1]
    tf = 1024
    row = lambda i, f: (i, 0)
    return pl.pallas_call(
        _mlp_body,
        grid=(tok.n // tm, ff // tf),
        in_specs=[
            pl.BlockSpec((tm, d), row),
            _full((1, d)),
            tok.mod_spec(layer, 3, tm),
            tok.mod_spec(layer, 4, tm),
            pl.BlockSpec((d, tf), lambda i, f: (0, f)),
            pl.BlockSpec((tf, d), lambda i, f: (f, 0)),
            _full((1, d)),
            tok.mod_spec(layer, 5, tm),
        ],
        out_specs=pl.BlockSpec((tm, d), row),
        out_shape=jax.ShapeDtypeStruct((tok.n, d), F32),
        scratch_shapes=[pltpu.VMEM((tm, d), BF16), pltpu.VMEM((tm, d), F32)],
        compiler_params=_cparams(("arbitrary", "arbitrary")),
        name="mlp",
    )(x, g_pre, mods, mods, w1, w2, g_post, mods)


HEAD_W = 2 * LANE
O_QLAT, O_CKV, O_KPE, O_KPES, O_COLS = 0, Q_LORA, Q_LORA + KV_LORA, Q_LORA + KV_LORA + LANE, Q_LORA + KV_LORA + 2 * LANE
QB_NOPE, QB_ROPE, QB_SWAP = 0, H_C * LANE, 2 * H_C * LANE


def _expand_kv(cb, kper, wkvb_ref, k_ref, v_ref):
    for hh in range(H_C):
        k_ref[:, hh * HEAD_W:hh * HEAD_W + LANE] = _dot(cb, wkvb_ref[:, hh * LANE:(hh + 1) * LANE]).astype(BF16)
        k_ref[:, hh * HEAD_W + LANE:(hh + 1) * HEAD_W] = kper
    v_ref[...] = _dot(cb, wkvb_ref[:, H_C * LANE:2 * H_C * LANE]).astype(BF16)


def _mla_proj_body(x_ref, g_ref, sh_ref, sc_ref, win_ref, qn_ref, wqb_ref, kvn_ref, wkvb_ref, cc_ref, ss_ref,
                   q_ref, k_ref, v_ref, ckv_ref, kpe_ref):
    h = (_rms(x_ref[...]) * g_ref[...] * (1.0 + sc_ref[...]) + sh_ref[...]).astype(BF16)
    cc = cc_ref[...]
    ss = ss_ref[...]
    qn = (_rms(_dot(h, win_ref[:, O_QLAT:O_QLAT + Q_LORA])) * qn_ref[...]).astype(BF16)
    ckvn = _rms(_dot(h, win_ref[:, O_CKV:O_CKV + KV_LORA])) * kvn_ref[...]
    kpe = _dot(h, win_ref[:, O_KPE:O_KPE + LANE])
    kper = (kpe * cc + _dot(h, win_ref[:, O_KPES:O_KPES + LANE]) * ss).astype(BF16)
    ckv_ref[...] = ckvn
    kpe_ref[...] = kpe
    for hh in range(H_C):
        o = hh * LANE
        nope = _dot(qn, wqb_ref[:, QB_NOPE + o:QB_NOPE + o + LANE])
        rot = (_dot(qn, wqb_ref[:, QB_ROPE + o:QB_ROPE + o + LANE]) * cc
               + _dot(qn, wqb_ref[:, QB_SWAP + o:QB_SWAP + o + LANE]) * ss)
        q_ref[:, hh * HEAD_W:hh * HEAD_W + LANE] = (nope * ATTN_Q_SCALE).astype(BF16)
        q_ref[:, hh * HEAD_W + LANE:(hh + 1) * HEAD_W] = (rot * ATTN_Q_SCALE).astype(BF16)
    _expand_kv(ckvn.astype(BF16), kper, wkvb_ref, k_ref, v_ref)


def _mla_proj_call(tok, x, mods, layer, g_pre, win, qn, wqb, kvn, wkvb, cc, ss):
    tm = tok.tile(512)
    d = tok.d
    row = lambda i: (i, 0)
    outs = [
        jax.ShapeDtypeStruct((tok.n, H_C * HEAD_W), BF16),
        jax.ShapeDtypeStruct((tok.n, H_C * HEAD_W), BF16),
        jax.ShapeDtypeStruct((tok.n, H_C * V_HEAD_C), BF16),
        jax.ShapeDtypeStruct((tok.n, KV_LORA), F32),
        jax.ShapeDtypeStruct((tok.n, LANE), F32),
    ]
    return pl.pallas_call(
        _mla_proj_body,
        grid=(tok.n // tm,),
        in_specs=[
            pl.BlockSpec((tm, d), row),
            _full((1, d)),
            tok.mod_spec(layer, 0, tm),
            tok.mod_spec(layer, 1, tm),
            _full(win.shape), _full(qn.shape), _full(wqb.shape), _full(kvn.shape), _full(wkvb.shape),
            pl.BlockSpec((tm, LANE), row),
            pl.BlockSpec((tm, LANE), row),
        ],
        out_specs=[pl.BlockSpec((tm, o.shape[1]), row) for o in outs],
        out_shape=outs,
        compiler_params=_cparams(("arbitrary",)),
        name="mla_proj",
    )(x, g_pre, mods, mods, win, qn, wqb, kvn, wkvb, cc, ss)


def _cache_expand_body(ckv_ref, kpe_ref, wkvb_ref, k_ref, v_ref):
    _expand_kv(ckv_ref[...].astype(BF16), kpe_ref[...].astype(BF16), wkvb_ref, k_ref, v_ref)


def _cache_expand_call(ckv, kpe_pad, wkvb):
    n = ckv.shape[0]
    tm = 512
    while n % tm:
        tm //= 2
    row = lambda i: (i, 0)
    outs = [jax.ShapeDtypeStruct((n, H_C * HEAD_W), BF16), jax.ShapeDtypeStruct((n, H_C * V_HEAD_C), BF16)]
    return pl.pallas_call(
        _cache_expand_body,
        grid=(n // tm,),
        in_specs=[pl.BlockSpec((tm, KV_LORA), row), pl.BlockSpec((tm, LANE), row), _full(wkvb.shape)],
        out_specs=[pl.BlockSpec((tm, o.shape[1]), row) for o in outs],
        out_shape=outs,
        compiler_params=_cparams(("arbitrary",)),
        name="cache_expand",
    )(ckv, kpe_pad, wkvb)


ATTN_TQ = 256
ATTN_TK = 512
ATTN_Q_SCALE = (QK_NOPE + QK_ROPE) ** -0.5 * float(np.log2(np.e))


def _softmax_pv(s, values):
    p = jnp.exp2(s - jnp.max(s, axis=-1, keepdims=True))
    den = jnp.sum(p, axis=-1, keepdims=True)
    p = p.astype(BF16)
    acc = functools.reduce(jnp.add, [_dot(p[:, k0:k0 + vb.shape[0]], vb) for vb, k0 in values])
    return (acc / den).astype(BF16)


def _attn_ctx_body(q_ref, k_ref, v_ref, o_ref):
    for hh in range(H_C):
        s = _dot_nt(q_ref[:, hh * HEAD_W:(hh + 1) * HEAD_W], k_ref[:, hh * HEAD_W:(hh + 1) * HEAD_W])
        o_ref[:, hh * V_HEAD_C:(hh + 1) * V_HEAD_C] = _softmax_pv(s, [(v_ref[:, hh * V_HEAD_C:(hh + 1) * V_HEAD_C], 0)])


def _attn_lat_body(q_ref, kc_ref, vc_ref, k_ref, v_ref, prev_ref, o_ref, sa_ref, sb_ref):
    del prev_ref
    past, n_self = kc_ref.shape[0], k_ref.shape[0]
    tq = sa_ref.shape[0]
    n = q_ref.shape[0] // tq
    tk = min(ATTN_TK, n_self)

    def scores(i, s_ref):
        q = q_ref[pl.ds(pl.multiple_of(i * tq, tq), tq), :]
        s_ref[:, 0:past] = _dot_nt(q, kc_ref[...])
        for k0 in range(0, n_self, tk):
            s_ref[:, past + k0:past + k0 + tk] = _dot_nt(q, k_ref[k0:k0 + tk, :])

    def finish(i, s_ref):
        values = [(vc_ref[...], 0)] + [(v_ref[k0:k0 + tk, :], past + k0) for k0 in range(0, n_self, tk)]
        o_ref[pl.ds(pl.multiple_of(i * tq, tq), tq), :] = _softmax_pv(s_ref[...], values)

    scores(0, sa_ref)

    def two_tiles(j, carry):
        i = 2 * j
        scores(i + 1, sb_ref)
        finish(i, sa_ref)
        scores(i + 2, sa_ref)
        finish(i + 1, sb_ref)
        return carry

    lax.fori_loop(0, n // 2 - 1, two_tiles, 0)
    scores(n - 1, sb_ref)
    finish(n - 2, sa_ref)
    finish(n - 1, sb_ref)


def _attn_ctx_call(tok, q, k, v):
    seq = lambda b: (b, 0)
    return pl.pallas_call(
        _attn_ctx_body,
        grid=(tok.nb_ctx,),
        in_specs=[pl.BlockSpec((tok.l_ctx, H_C * HEAD_W), seq),
                  pl.BlockSpec((tok.l_ctx, H_C * HEAD_W), seq),
                  pl.BlockSpec((tok.l_ctx, H_C * V_HEAD_C), seq)],
        out_specs=pl.BlockSpec((tok.l_ctx, H_C * V_HEAD_C), seq),
        out_shape=jax.ShapeDtypeStruct((tok.n, H_C * V_HEAD_C), BF16),
        compiler_params=_cparams(("arbitrary",)),
        name="attn_ctx",
    )(q, k, v)


def _attn_lat_call(tok, q, k, v, k_cache, v_cache, out_prev):
    blk0 = tok.n_ctx // tok.l_lat
    past = k_cache.shape[0] // tok.nb_lat
    tq = min(ATTN_TQ, tok.l_lat // 2)
    seq = lambda b, hh: (blk0 + b, hh)
    cache = lambda b, hh: (b, hh)
    return pl.pallas_call(
        _attn_lat_body,
        grid=(tok.nb_lat, H_C),
        in_specs=[pl.BlockSpec((tok.l_lat, HEAD_W), seq),
                  pl.BlockSpec((past, HEAD_W), cache),
                  pl.BlockSpec((past, V_HEAD_C), cache),
                  pl.BlockSpec((tok.l_lat, HEAD_W), seq),
                  pl.BlockSpec((tok.l_lat, V_HEAD_C), seq),
                  pl.BlockSpec(memory_space=pl.ANY)],
        out_specs=pl.BlockSpec((tok.l_lat, V_HEAD_C), seq),
        out_shape=jax.ShapeDtypeStruct(out_prev.shape, out_prev.dtype),
        input_output_aliases={5: 0},
        scratch_shapes=[pltpu.VMEM((tq, past + tok.l_lat), F32)] * 2,
        compiler_params=_cparams(("arbitrary", "arbitrary")),
        name="attn_lat",
    )(q, k_cache, v_cache, k, v, out_prev)


def _swap_halves(w, head_dim):
    n = w.shape[-1]
    idx = np.arange(n)
    idx = (idx // head_dim) * head_dim + (idx % head_dim + head_dim // 2) % head_dim
    return w[..., idx]


def _pad_cols(w, width):
    return jnp.pad(w, ((0, 0), (0, width - w.shape[1])))


def _even_weights(w_in, w_gk2, b_gk2):
    sizes = (A_QK, A_QK, A_V, A_V, 2 * GATE_RANK, A_QK, A_QK, A_V, A_V)
    qa, ka, va, ga, gk, qb, kb, vb, gb = jnp.split(w_in, np.cumsum(sizes)[:-1].tolist(), axis=1)
    w = jnp.concatenate([qa, ka, va, ga, qb, kb, vb, gb, _pad_cols(gk, LANE),
                         _swap_halves(qb, DK), _swap_halves(kb, DK)], axis=1).astype(BF16)
    wg = jnp.zeros((LANE, 2 * A_QK), F32)
    wg = wg.at[0:GATE_RANK, 0:A_QK].set(w_gk2[0]).at[GATE_RANK:2 * GATE_RANK, A_QK:2 * A_QK].set(w_gk2[1])
    bg = b_gk2.reshape(1, 2 * A_QK)
    return w, wg.astype(BF16), bg


def _odd_weights(w_in, w_q_b, w_kv_b):
    q_lat, ckv, kpe = w_in[:, :Q_LORA], w_in[:, Q_LORA:Q_LORA + KV_LORA], w_in[:, Q_LORA + KV_LORA:]
    win = jnp.concatenate([q_lat, ckv, _pad_cols(kpe, LANE), _pad_cols(_swap_halves(kpe, QK_ROPE), LANE)],
                          axis=1).astype(BF16)
    wq = w_q_b.reshape(Q_LORA, H_C, QK_NOPE + QK_ROPE)
    nope = wq[:, :, :QK_NOPE].reshape(Q_LORA, H_C * QK_NOPE)
    rope = wq[:, :, QK_NOPE:]
    pad = lambda r: jnp.pad(r, ((0, 0), (0, 0), (0, LANE - QK_ROPE))).reshape(Q_LORA, H_C * LANE)
    wqb = jnp.concatenate([nope, pad(rope), pad(_swap_halves(rope, QK_ROPE))], axis=1).astype(BF16)
    wkv = w_kv_b.reshape(KV_LORA, H_C, QK_NOPE + V_HEAD_C)
    wkvb = jnp.concatenate([wkv[:, :, :QK_NOPE].reshape(KV_LORA, H_C * QK_NOPE),
                            wkv[:, :, QK_NOPE:].reshape(KV_LORA, H_C * V_HEAD_C)], axis=1).astype(BF16)
    return win, wqb, wkvb


def _rope_tables(tok):
    rows = tok.l_lat // GRID_W
    row = jnp.repeat(jnp.arange(rows), GRID_W).astype(F32)
    col = jnp.tile(jnp.arange(GRID_W), rows).astype(F32)
    n_freq = QK_ROPE // 4
    inv = ROPE_BASE ** (-jnp.arange(n_freq, dtype=F32) / n_freq)
    ang = jnp.concatenate([row[:, None] * inv, col[:, None] * inv], axis=-1)
    cos, sin = jnp.cos(ang), jnp.sin(ang)
    cc = jnp.tile(jnp.concatenate([cos, cos], axis=-1), (tok.nb_lat, LANE // QK_ROPE))
    ss = jnp.tile(jnp.concatenate([-sin, sin], axis=-1), (tok.nb_lat, LANE // QK_ROPE))
    cc = jnp.concatenate([jnp.ones((tok.n_ctx, LANE), F32), cc], axis=0)
    ss = jnp.concatenate([jnp.zeros((tok.n_ctx, LANE), F32), ss], axis=0)
    return cc, ss


def _states_to_kernel(s):
    nb, _, heads = s.shape[:3]
    st = jnp.swapaxes(s.reshape(nb, 2, heads // 2, 2, DK, DV), -1, -2)
    z = jnp.zeros_like(st[:, :, :, 0])
    rows = [jnp.concatenate([st[:, :, :, 0], z], axis=-1), jnp.concatenate([z, st[:, :, :, 1]], axis=-1)]
    return jnp.concatenate(rows, axis=-2)


def _states_from_kernel(st):
    nb, _, pairs = st.shape[:3]
    heads = jnp.stack([st[:, :, :, :DV, :DK], st[:, :, :, DV:, DK:]], axis=3)
    return jnp.swapaxes(heads, -1, -2).reshape(nb, 2, 2 * pairs, DK, DV)


def kernel(x_prompt, x_sample, cache_ckv, cache_kpe, state_gla, state_ret, c, c_ctx, w_ada, b_ada, norm_mix_pre, norm_mix_post, norm_mlp_pre, norm_mlp_post, w_in_even, w_gk2, b_gk2, gla_norm, ret_decay, w_out_even, w_in_odd, q_a_norm, w_q_b, kv_a_norm, w_kv_b, w_out_odd, w_mlp1, w_mlp2):
    nb_ctx, l_ctx, d = x_prompt.shape
    nb_lat, l_lat, _ = x_sample.shape
    depth = w_ada.shape[0]
    tok = _Tokens(nb_ctx, l_ctx, nb_lat, l_lat, d)
    assert nb_lat < MOD_ROWS and tok.n_ctx % l_lat == 0 and l_ctx % CHUNK == 0 and l_lat % (2 * CHUNK) == 0

    cond = jnp.concatenate([c, c_ctx[None, :], jnp.zeros((MOD_ROWS - nb_lat - 1, d), F32)], axis=0)
    mods = _ada_call(cond, w_ada, b_ada).reshape(depth, MOD_ROWS, 6, 1, d)
    cc, ss = _rope_tables(tok)
    x = jnp.concatenate([x_prompt.reshape(tok.n_ctx, d), x_sample.reshape(tok.n_lat, d)], axis=0)
    vec = lambda a: a.reshape(1, -1)

    new_ckv, new_kpe, new_gla, new_ret = [], [], [], []
    for l in range(depth):
        i = l // 2
        if l % 2 == 0:
            w, wg, bg = _even_weights(w_in_even[i], w_gk2[i], b_gk2[i])
            dec = jnp.repeat(ret_decay[i], DK, axis=-1)
            q, k, lf, lb, gt, v = _even_proj_call(tok, x, mods, l, vec(norm_mix_pre[l]), w, wg, bg, dec, cc, ss)
            gain = jnp.concatenate([jnp.tile(gla_norm[i], N_HEAD_SCAN), jnp.ones((A_V,), F32)]).reshape(1, 2 * A_V)
            s0_lat = _states_to_kernel(jnp.concatenate([state_gla[:, i], state_ret[:, i]], axis=2))
            s0_ctx = jnp.zeros((nb_ctx,) + s0_lat.shape[1:], F32)
            m_ctx, s_fin = _scan_call(tok, "ctx", q, k, lf, lb, v, gt, gain, s0_ctx)
            (m,) = _scan_call(tok, "lat", q, k, lf, lb, v, gt, gain, s0_lat, out_prev=m_ctx)
            s_fin = _states_from_kernel(s_fin)
            new_gla.append(s_fin[:, :, :N_HEAD_SCAN])
            new_ret.append(s_fin[:, :, N_HEAD_SCAN:])
            w_out = w_out_even[i].astype(BF16)
        else:
            win, wqb, wkvb = _odd_weights(w_in_odd[i], w_q_b[i], w_kv_b[i])
            q, k, v, ckv, kpe = _mla_proj_call(tok, x, mods, l, vec(norm_mix_pre[l]), win, vec(q_a_norm[i]), wqb,
                                               vec(kv_a_norm[i]), wkvb, cc, ss)
            past = cache_ckv.shape[2]
            kpe_pad = jnp.pad(cache_kpe[:, i].reshape(nb_lat * past, QK_ROPE), ((0, 0), (0, LANE - QK_ROPE)))
            k_c, v_c = _cache_expand_call(cache_ckv[:, i].reshape(nb_lat * past, KV_LORA), kpe_pad, wkvb)
            m = _attn_lat_call(tok, q, k, v, k_c, v_c, _attn_ctx_call(tok, q, k, v))
            new_ckv.append(ckv[:tok.n_ctx].reshape(nb_ctx, l_ctx, KV_LORA))
            new_kpe.append(kpe[:tok.n_ctx, :QK_ROPE].reshape(nb_ctx, l_ctx, QK_ROPE))
            w_out = w_out_odd[i].astype(BF16)
        x = _out_proj_call(tok, m, x, mods, l, w_out, vec(norm_mix_post[l]))
        x = _mlp_call(tok, x, mods, l, vec(norm_mlp_pre[l]), w_mlp1[l].astype(BF16), w_mlp2[l].astype(BF16),
                      vec(norm_mlp_post[l]))

    return (x[:tok.n_ctx].reshape(nb_ctx, l_ctx, d), x[tok.n_ctx:].reshape(nb_lat, l_lat, d),
            jnp.stack(new_ckv, axis=1), jnp.stack(new_kpe, axis=1),
            jnp.stack(new_gla, axis=1), jnp.stack(new_ret, axis=1))
```

```python
import functools

import numpy as np
import jax
import jax.numpy as jnp
from jax import lax
from jax.experimental import pallas as pl
from jax.experimental.pallas import tpu as pltpu

F32 = jnp.float32
BF16 = jnp.bfloat16

EPS = 1e-6
ROPE_BASE = 10000.0
GRID_W = 64
CHUNK = 64
GATE_RANK = 16
GATE_NORM = 16.0
N_HEAD_SCAN = 4
DK = 64
DV = 128
H_C = 8
Q_LORA = 256
KV_LORA = 256
QK_NOPE = 128
QK_ROPE = 64
V_HEAD_C = 128
LANE = 128
MOD_ROWS = 16

VMEM_LIMIT = 56 * 1024 * 1024


def _cparams(sem):
    return pltpu.CompilerParams(dimension_semantics=sem, vmem_limit_bytes=VMEM_LIMIT)


def _dot(a, b):
    return jnp.dot(a, b, preferred_element_type=F32)


def _dot_nt(a, b):
    return lax.dot_general(a, b, (((1,), (1,)), ((), ())), preferred_element_type=F32)


def _dot_tn(a, b):
    return lax.dot_general(a, b, (((0,), (0,)), ((), ())), preferred_element_type=F32)


def _rms(x):
    return x * lax.rsqrt(jnp.mean(x * x, axis=-1, keepdims=True) + EPS)


def _silu(x):
    return x * jax.nn.sigmoid(x)


def _full(shape):
    n = len(shape)
    return pl.BlockSpec(shape, lambda *_: (0,) * n)


def _ada_body(cond_ref, w_ref, b_ref, o_ref):
    s = _silu(cond_ref[...]).astype(BF16)
    o_ref[...] = _dot(s, w_ref[...].astype(BF16)) + b_ref[...]


def _ada_call(cond, w_ada, b_ada):
    depth, d, n = w_ada.shape
    tn = 1536
    return pl.pallas_call(
        _ada_body,
        grid=(depth, n // tn),
        in_specs=[
            pl.BlockSpec((MOD_ROWS, d), lambda l, j: (0, 0)),
            pl.BlockSpec((None, d, tn), lambda l, j: (l, 0, j)),
            pl.BlockSpec((None, 1, tn), lambda l, j: (l, 0, j)),
        ],
        out_specs=pl.BlockSpec((None, MOD_ROWS, tn), lambda l, j: (l, 0, j)),
        out_shape=jax.ShapeDtypeStruct((depth, MOD_ROWS, n), F32),
        compiler_params=_cparams(("arbitrary", "arbitrary")),
        name="ada_mod",
    )(cond, w_ada, b_ada.reshape(depth, 1, n))


class _Tokens:
    def __init__(self, nb_ctx, l_ctx, nb_lat, l_lat, d):
        self.nb_ctx, self.l_ctx, self.nb_lat, self.l_lat, self.d = nb_ctx, l_ctx, nb_lat, l_lat, d
        self.n_ctx = nb_ctx * l_ctx
        self.n_lat = nb_lat * l_lat
        self.n = self.n_ctx + self.n_lat
        self.ctx_row = nb_lat

    def tile(self, want):
        t = want
        while self.n_ctx % t or self.l_lat % t:
            t //= 2
        return t

    def mod_spec(self, layer, chunk, tm):
        n_ctx, l_lat, ctx_row = self.n_ctx, self.l_lat, self.ctx_row

        def idx(i, *_):
            start = i * tm
            row = jnp.where(start < n_ctx, ctx_row, (start - n_ctx) // l_lat)
            return (layer, row, chunk, 0, 0)

        return pl.BlockSpec((None, None, None, 1, self.d), idx)


A_QK = N_HEAD_SCAN * DK
A_V = N_HEAD_SCAN * DV
E_QA, E_KA, E_VA, E_GA = 0, A_QK, 2 * A_QK, 2 * A_QK + A_V
E_QB = E_GA + A_V
E_KB = E_QB + A_QK
E_VB = E_KB + A_QK
E_GB = E_VB + A_V
E_GK = E_GB + A_V
E_QBS = E_GK + LANE
E_KBS = E_QBS + A_QK
E_COLS = E_KBS + A_QK


def _log_sigmoid(x):
    return jnp.minimum(x, 0.0) - jnp.log1p(jnp.exp(-jnp.abs(x)))


def _even_proj_body(x_ref, g_ref, sh_ref, sc_ref, w_ref, wg_ref, bg_ref, dec_ref, cc_ref, ss_ref,
                    q_ref, k_ref, lf_ref, lb_ref, gt_ref, v_ref):
    tm = x_ref.shape[0]
    h = (_rms(x_ref[...]) * g_ref[...] * (1.0 + sc_ref[...]) + sh_ref[...]).astype(BF16)

    def proj(start, width):
        return _dot(h, w_ref[:, start:start + width])

    cc = cc_ref[...]
    ss = ss_ref[...]
    scale = DK ** -0.5
    q_ref[:, 0:A_QK] = proj(E_QA, A_QK) * scale
    k_ref[:, 0:A_QK] = proj(E_KA, A_QK)
    qb, qbs, kb, kbs = proj(E_QB, A_QK), proj(E_QBS, A_QK), proj(E_KB, A_QK), proj(E_KBS, A_QK)
    for j in range(A_QK // LANE):
        sl = slice(j * LANE, (j + 1) * LANE)
        o = A_QK + j * LANE
        q_ref[:, o:o + LANE] = qb[:, sl] * cc + qbs[:, sl] * ss
        k_ref[:, o:o + LANE] = (kb[:, sl] * cc + kbs[:, sl] * ss) * scale
    v_ref[:, 0:A_V] = proj(E_VA, A_V).astype(BF16)
    v_ref[:, A_V:2 * A_V] = proj(E_VB, A_V).astype(BF16)
    gt_ref[:, 0:A_V] = proj(E_GA, A_V)
    gt_ref[:, A_V:2 * A_V] = proj(E_GB, A_V)
    gk = proj(E_GK, LANE).astype(BF16)
    la = _log_sigmoid(_dot(gk, wg_ref[...]) + bg_ref[...]) * (1.0 / GATE_NORM)
    lf_ref[:, 0:A_QK] = la[:, 0:A_QK]
    lb_ref[:, 0:A_QK] = la[:, A_QK:2 * A_QK]
    log_g = -jnp.exp(dec_ref[...])
    lf_ref[:, A_QK:2 * A_QK] = jnp.broadcast_to(log_g[0:1, :], (tm, A_QK))
    lb_ref[:, A_QK:2 * A_QK] = jnp.broadcast_to(log_g[1:2, :], (tm, A_QK))


def _even_proj_call(tok, x, mods, layer, g_pre, w, wg, bg, dec, cc, ss):
    tm = tok.tile(512)
    d = tok.d
    row = lambda i: (i, 0)
    outs = [
        jax.ShapeDtypeStruct((tok.n, 2 * A_QK), F32),
        jax.ShapeDtypeStruct((tok.n, 2 * A_QK), F32),
        jax.ShapeDtypeStruct((tok.n, 2 * A_QK), F32),
        jax.ShapeDtypeStruct((tok.n, 2 * A_QK), F32),
        jax.ShapeDtypeStruct((tok.n, 2 * A_V), F32),
        jax.ShapeDtypeStruct((tok.n, 2 * A_V), BF16),
    ]
    return pl.pallas_call(
        _even_proj_body,
        grid=(tok.n // tm,),
        in_specs=[
            pl.BlockSpec((tm, d), row),
            _full((1, d)),
            tok.mod_spec(layer, 0, tm),
            tok.mod_spec(layer, 1, tm),
            _full(w.shape), _full(wg.shape), _full(bg.shape), _full(dec.shape),
            pl.BlockSpec((tm, LANE), row),
            pl.BlockSpec((tm, LANE), row),
        ],
        out_specs=[pl.BlockSpec((tm, o.shape[1]), row) for o in outs],
        out_shape=outs,
        compiler_params=_cparams(("arbitrary",)),
        name="even_proj",
    )(x, g_pre, mods, mods, w, wg, bg, dec, cc, ss)


SCAN_GROUP = 8


def _chunk_cumsum(x):
    row = lax.broadcasted_iota(jnp.int32, x.shape, 0) % CHUNK
    s = 1
    while s < CHUNK:
        x = x + jnp.where(row >= s, pltpu.roll(x, s, axis=0), 0.0)
        s *= 2
    return x


def _scan_body(q_ref, k_ref, lf_ref, lb_ref, v_ref, gt_ref, gain_ref, *rest, seq_len, group, context):
    if context:
        m_ref, sfin_ref, st_ref, o_acc = rest
    else:
        s0_ref, _, m_ref, st_ref, o_acc = rest
    C = CHUNK
    blk = group * C
    nblk = seq_len // blk
    pair_w = 2 * DK
    head0 = lax.broadcasted_iota(jnp.int32, (blk, pair_w), 1) < DK
    t_in = lax.broadcasted_iota(jnp.int32, (C, pair_w), 0)
    j_in = lax.broadcasted_iota(jnp.int32, (C, pair_w), 1) % DK
    keep_fwd = t_in >= j_in
    keep_bwd = t_in <= j_in
    on_diag = ((lax.broadcasted_iota(jnp.int32, (2 * DV, pair_w), 0) < DV)
               == (lax.broadcasted_iota(jnp.int32, (2 * DV, pair_w), 1) < DK))
    zeros_v = jnp.zeros((C, DV), BF16)

    if context:
        st_ref[...] = jnp.zeros(st_ref.shape, F32)
    else:
        st_ref[...] = s0_ref[...]

    def one_direction(r0, d, log_ref, keep, reverse):
        g = log_ref[pl.ds(r0, blk), :]
        b = _chunk_cumsum(g)
        tots = [b[C * j + C - 1:C * j + C, :] for j in range(group)]
        totb = jnp.concatenate([jnp.broadcast_to(t, (C, pair_w)) for t in tots], axis=0)
        if reverse:
            b = totb - b + g
        q = q_ref[pl.ds(r0, blk), :]
        k = k_ref[pl.ds(r0, blk), :]
        vblk = v_ref[pl.ds(r0, blk), :]
        q_dec = (q * jnp.exp(b)).astype(BF16)
        k_inv = k * jnp.exp(-b)
        k_up = (k * jnp.exp(totb - b)).astype(BF16)
        k_inv0 = jnp.where(head0, k_inv, 0.0).astype(BF16)
        k_inv1 = jnp.where(head0, 0.0, k_inv).astype(BF16)
        st = st_ref[d]
        outs = [None] * group
        for j in (reversed(range(group)) if reverse else range(group)):
            sl = slice(C * j, C * (j + 1))
            k_bd = jnp.concatenate([k_inv0[sl], k_inv1[sl]], axis=0)
            a = jnp.where(keep, _dot_nt(q_dec[sl], k_bd), 0.0).astype(BF16)
            vc = vblk[sl]
            v_bd = jnp.concatenate([jnp.concatenate([vc[:, :DV], zeros_v], axis=1),
                                    jnp.concatenate([zeros_v, vc[:, DV:]], axis=1)], axis=0)
            outs[j] = _dot_nt(q_dec[sl], st.astype(BF16)) + _dot(a, v_bd)
            st = st * jnp.exp(tots[j]) + jnp.where(on_diag, _dot_tn(vc, k_up[sl]), 0.0)
        st_ref[d] = st
        return jnp.concatenate(outs, axis=0)

    def body(c, carry, accumulate):
        rf = pl.multiple_of(c * blk, blk)
        rb = pl.multiple_of((nblk - 1 - c) * blk, blk)
        o_f = one_direction(rf, 0, lf_ref, keep_fwd, False)
        o_b = one_direction(rb, 1, lb_ref, keep_bwd, True)
        if accumulate:
            o_acc[pl.ds(rf, blk), :] += o_f
            o_acc[pl.ds(rb, blk), :] += o_b
        else:
            o_acc[pl.ds(rf, blk), :] = o_f
            o_acc[pl.ds(rb, blk), :] = o_b
        return carry

    lax.fori_loop(0, nblk // 2, functools.partial(body, accumulate=False), 0)
    lax.fori_loop(nblk // 2, nblk, functools.partial(body, accumulate=True), 0)

    if context:
        sfin_ref[...] = st_ref[...]

    tr = min(seq_len, 256)

    def finish(i, carry):
        r0 = pl.multiple_of(i * tr, tr)
        o = o_acc[pl.ds(r0, tr), :]
        gate = _silu(gt_ref[pl.ds(r0, tr), :])
        gain = gain_ref[...]
        for hh in range(2):
            sl = slice(hh * DV, (hh + 1) * DV)
            m_ref[pl.ds(r0, tr), sl] = (_rms(o[:, sl]) * gain[:, sl] * gate[:, sl]).astype(BF16)
        return carry

    lax.fori_loop(0, seq_len // tr, finish, 0)


def _scan_call(tok, q, k, lf, lb, v, gt, gain, s0=None, out_prev=None):
    context = s0 is None
    if context:
        nb, seq_len, blk0 = tok.nb_ctx, tok.l_ctx, 0
    else:
        nb, seq_len, blk0 = tok.nb_lat, tok.l_lat, tok.n_ctx // tok.l_lat
    pairs = q.shape[1] // (2 * DK)
    group = min(SCAN_GROUP, seq_len // CHUNK // 2)
    seq = lambda b, p: (blk0 + b, p)
    st_spec = pl.BlockSpec((None, 2, None, 2 * DV, 2 * DK), lambda b, p: (b, 0, p, 0, 0))
    in_specs = [
        pl.BlockSpec((seq_len, 2 * DK), seq),
        pl.BlockSpec((seq_len, 2 * DK), seq),
        pl.BlockSpec((seq_len, 2 * DK), seq),
        pl.BlockSpec((seq_len, 2 * DK), seq),
        pl.BlockSpec((seq_len, 2 * DV), seq),
        pl.BlockSpec((seq_len, 2 * DV), seq),
        pl.BlockSpec((1, 2 * DV), lambda b, p: (0, p)),
    ]
    args = [q, k, lf, lb, v, gt, gain]
    out_shape = [jax.ShapeDtypeStruct((tok.n, pairs * 2 * DV), BF16)]
    out_specs = [pl.BlockSpec((seq_len, 2 * DV), seq)]
    aliases = {}
    if context:
        out_shape.append(jax.ShapeDtypeStruct((nb, 2, pairs, 2 * DV, 2 * DK), F32))
        out_specs.append(st_spec)
    else:
        in_specs += [st_spec, pl.BlockSpec(memory_space=pl.ANY)]
        args += [s0, out_prev]
        aliases = {len(args) - 1: 0}
    return pl.pallas_call(
        functools.partial(_scan_body, seq_len=seq_len, group=group, context=context),
        grid=(nb, pairs),
        in_specs=in_specs,
        out_specs=out_specs,
        out_shape=out_shape,
        input_output_aliases=aliases,
        scratch_shapes=[pltpu.VMEM((2, 2 * DV, 2 * DK), F32), pltpu.VMEM((seq_len, 2 * DV), F32)],
        compiler_params=_cparams(("arbitrary", "arbitrary")),
        name="scan_ctx" if context else "scan_lat",
    )(*args)


def _mix_mlp_body(m_ref, x_ref, wo_ref, gmix_ref, gate1_ref, gpre_ref, sh_ref, sc_ref, w1_ref, w2_ref,
                  gpost_ref, gate2_ref, o_ref):
    y = _dot(m_ref[...], wo_ref[...])
    x1 = x_ref[...] + gate1_ref[...] * (_rms(y) * gmix_ref[...])
    h = (_rms(x1) * gpre_ref[...] * (1.0 + sc_ref[...]) + sh_ref[...]).astype(BF16)
    u = jnp.maximum(_dot(h, w1_ref[...]), 0.0)
    z = _dot((u * u).astype(BF16), w2_ref[...])
    o_ref[...] = x1 + gate2_ref[...] * (_rms(z) * gpost_ref[...])


def _mix_mlp_call(tok, m, x, mods, layer, w_out, w_out_idx, g_mix, g_pre, w1, w2, g_post):
    tm = tok.tile(512)
    d = tok.d
    row = lambda i: (i, 0)
    resident = lambda w, idx: pl.BlockSpec((None,) + w.shape[1:], lambda i: (idx, 0, 0),
                                           pipeline_mode=pl.Buffered(1))
    return pl.pallas_call(
        _mix_mlp_body,
        grid=(tok.n // tm,),
        in_specs=[
            pl.BlockSpec((tm, m.shape[1]), row),
            pl.BlockSpec((tm, d), row),
            resident(w_out, w_out_idx),
            _full((1, d)),
            tok.mod_spec(layer, 2, tm),
            _full((1, d)),
            tok.mod_spec(layer, 3, tm),
            tok.mod_spec(layer, 4, tm),
            resident(w1, layer),
            resident(w2, layer),
            _full((1, d)),
            tok.mod_spec(layer, 5, tm),
        ],
        out_specs=pl.BlockSpec((tm, d), row),
        out_shape=jax.ShapeDtypeStruct((tok.n, d), F32),
        compiler_params=_cparams(("arbitrary",)),
        name="mix_mlp",
    )(m, x, w_out, g_mix, mods, g_pre, mods, mods, w1, w2, g_post, mods)


HEAD_W = 2 * LANE
O_QLAT, O_CKV, O_KPE, O_KPES, O_COLS = 0, Q_LORA, Q_LORA + KV_LORA, Q_LORA + KV_LORA + LANE, Q_LORA + KV_LORA + 2 * LANE
QB_NOPE, QB_ROPE, QB_SWAP = 0, H_C * LANE, 2 * H_C * LANE


def _expand_kv(cb, kper, wkvb_ref, k_ref, v_ref):
    for hp in range(H_C // 2):
        nope2 = _dot(cb, wkvb_ref[:, hp * 2 * LANE:(hp + 1) * 2 * LANE])
        for j in range(2):
            hh = 2 * hp + j
            k_ref[:, hh * HEAD_W:hh * HEAD_W + LANE] = nope2[:, j * LANE:(j + 1) * LANE].astype(BF16)
            k_ref[:, hh * HEAD_W + LANE:(hh + 1) * HEAD_W] = kper
    v_ref[...] = _dot(cb, wkvb_ref[:, H_C * LANE:2 * H_C * LANE]).astype(BF16)


def _mla_proj_body(x_ref, g_ref, sh_ref, sc_ref, win_ref, qn_ref, wqb_ref, kvn_ref, wkvb_ref, cc_ref, ss_ref,
                   q_ref, k_ref, v_ref, ckv_ref, kpe_ref):
    h = (_rms(x_ref[...]) * g_ref[...] * (1.0 + sc_ref[...]) + sh_ref[...]).astype(BF16)
    cc = cc_ref[...]
    ss = ss_ref[...]
    qn = (_rms(_dot(h, win_ref[:, O_QLAT:O_QLAT + Q_LORA])) * qn_ref[...]).astype(BF16)
    ckvn = _rms(_dot(h, win_ref[:, O_CKV:O_CKV + KV_LORA])) * kvn_ref[...]
    kpe2 = _dot(h, win_ref[:, O_KPE:O_KPE + 2 * LANE])
    kpe = kpe2[:, 0:LANE]
    kper = (kpe * cc + kpe2[:, LANE:2 * LANE] * ss).astype(BF16)
    ckv_ref[...] = ckvn
    kpe_ref[...] = kpe
    for hp in range(H_C // 2):
        o = hp * 2 * LANE
        nope2 = _dot(qn, wqb_ref[:, QB_NOPE + o:QB_NOPE + o + 2 * LANE])
        rope2 = _dot(qn, wqb_ref[:, QB_ROPE + o:QB_ROPE + o + 2 * LANE])
        swap2 = _dot(qn, wqb_ref[:, QB_SWAP + o:QB_SWAP + o + 2 * LANE])
        for j in range(2):
            hh = 2 * hp + j
            sl = slice(j * LANE, (j + 1) * LANE)
            rot = rope2[:, sl] * cc + swap2[:, sl] * ss
            q_ref[:, hh * HEAD_W:hh * HEAD_W + LANE] = (nope2[:, sl] * ATTN_Q_SCALE).astype(BF16)
            q_ref[:, hh * HEAD_W + LANE:(hh + 1) * HEAD_W] = (rot * ATTN_Q_SCALE).astype(BF16)
    _expand_kv(ckvn.astype(BF16), kper, wkvb_ref, k_ref, v_ref)


def _mla_proj_call(tok, x, mods, layer, g_pre, win, qn, wqb, kvn, wkvb, cc, ss):
    tm = tok.tile(512)
    d = tok.d
    row = lambda i: (i, 0)
    outs = [
        jax.ShapeDtypeStruct((tok.n, H_C * HEAD_W), BF16),
        jax.ShapeDtypeStruct((tok.n, H_C * HEAD_W), BF16),
        jax.ShapeDtypeStruct((tok.n, H_C * V_HEAD_C), BF16),
        jax.ShapeDtypeStruct((tok.n, KV_LORA), F32),
        jax.ShapeDtypeStruct((tok.n, LANE), F32),
    ]
    return pl.pallas_call(
        _mla_proj_body,
        grid=(tok.n // tm,),
        in_specs=[
            pl.BlockSpec((tm, d), row),
            _full((1, d)),
            tok.mod_spec(layer, 0, tm),
            tok.mod_spec(layer, 1, tm),
            _full(win.shape), _full(qn.shape), _full(wqb.shape), _full(kvn.shape), _full(wkvb.shape),
            pl.BlockSpec((tm, LANE), row),
            pl.BlockSpec((tm, LANE), row),
        ],
        out_specs=[pl.BlockSpec((tm, o.shape[1]), row) for o in outs],
        out_shape=outs,
        compiler_params=_cparams(("arbitrary",)),
        name="mla_proj",
    )(x, g_pre, mods, mods, win, qn, wqb, kvn, wkvb, cc, ss)


def _cache_expand_body(ckv_ref, kpe_ref, wkvb_ref, k_ref, v_ref):
    _expand_kv(ckv_ref[...].astype(BF16), kpe_ref[...].astype(BF16), wkvb_ref, k_ref, v_ref)


def _cache_expand_call(ckv, kpe_pad, wkvb):
    n = ckv.shape[0]
    tm = 512
    while n % tm:
        tm //= 2
    row = lambda i: (i, 0)
    outs = [jax.ShapeDtypeStruct((n, H_C * HEAD_W), BF16), jax.ShapeDtypeStruct((n, H_C * V_HEAD_C), BF16)]
    return pl.pallas_call(
        _cache_expand_body,
        grid=(n // tm,),
        in_specs=[pl.BlockSpec((tm, KV_LORA), row), pl.BlockSpec((tm, LANE), row), _full(wkvb.shape)],
        out_specs=[pl.BlockSpec((tm, o.shape[1]), row) for o in outs],
        out_shape=outs,
        compiler_params=_cparams(("arbitrary",)),
        name="cache_expand",
    )(ckv, kpe_pad, wkvb)


ATTN_TQ = 256
ATTN_TK = 512
ATTN_Q_SCALE = (QK_NOPE + QK_ROPE) ** -0.5 * float(np.log2(np.e))


def _softmax_pv(s, values):
    p = jnp.exp2(s - jnp.max(s, axis=-1, keepdims=True))
    den = jnp.sum(p, axis=-1, keepdims=True)
    p = p.astype(BF16)
    acc = functools.reduce(jnp.add, [_dot(p[:, k0:k0 + vb.shape[0]], vb) for vb, k0 in values])
    return (acc / den).astype(BF16)


def _attn_ctx_body(q_ref, k_ref, v_ref, o_ref):
    for hh in range(H_C):
        s = _dot_nt(q_ref[:, hh * HEAD_W:(hh + 1) * HEAD_W], k_ref[:, hh * HEAD_W:(hh + 1) * HEAD_W])
        o_ref[:, hh * V_HEAD_C:(hh + 1) * V_HEAD_C] = _softmax_pv(s, [(v_ref[:, hh * V_HEAD_C:(hh + 1) * V_HEAD_C], 0)])


def _lane_groups(x):
    return [x[:, g:g + LANE] for g in range(0, x.shape[1], LANE)]


def _attn_lat_body(q_ref, kc_ref, vc_ref, k_ref, v_ref, prev_ref, o_ref, sa_ref, sb_ref, ma_ref, mb_ref):
    del prev_ref
    past, n_self = kc_ref.shape[0], k_ref.shape[0]
    tq = sa_ref.shape[0]
    n = q_ref.shape[0] // tq
    tk = min(ATTN_TK, n_self)
    blocks = [(kc_ref, vc_ref, 0, past, 0)] + [(k_ref, v_ref, k0, tk, past + k0) for k0 in range(0, n_self, tk)]

    def scores(i, s_ref, m_ref):
        q = q_ref[pl.ds(pl.multiple_of(i * tq, tq), tq), :]
        mx = None
        for kk_ref, _, k0, size, col in blocks:
            s = _dot_nt(q, kk_ref[k0:k0 + size, :])
            s_ref[:, col:col + size] = s
            mx = functools.reduce(jnp.maximum, _lane_groups(s) + ([] if mx is None else [mx]))
        m_ref[...] = mx

    def finish(i, s_ref, m_ref):
        m = jnp.max(m_ref[...], axis=-1, keepdims=True)
        den = acc = None
        for _, vv_ref, k0, size, col in blocks:
            p = jnp.exp2(s_ref[:, col:col + size] - m)
            den = functools.reduce(jnp.add, _lane_groups(p) + ([] if den is None else [den]))
            pv = _dot(p.astype(BF16), vv_ref[k0:k0 + size, :])
            acc = pv if acc is None else acc + pv
        out = acc / jnp.sum(den, axis=-1, keepdims=True)
        o_ref[pl.ds(pl.multiple_of(i * tq, tq), tq), :] = out.astype(BF16)

    scores(0, sa_ref, ma_ref)

    def two_tiles(j, carry):
        i = 2 * j
        scores(i + 1, sb_ref, mb_ref)
        finish(i, sa_ref, ma_ref)
        scores(i + 2, sa_ref, ma_ref)
        finish(i + 1, sb_ref, mb_ref)
        return carry

    lax.fori_loop(0, n // 2 - 1, two_tiles, 0)
    scores(n - 1, sb_ref, mb_ref)
    finish(n - 2, sa_ref, ma_ref)
    finish(n - 1, sb_ref, mb_ref)


def _attn_ctx_call(tok, q, k, v):
    seq = lambda b: (b, 0)
    return pl.pallas_call(
        _attn_ctx_body,
        grid=(tok.nb_ctx,),
        in_specs=[pl.BlockSpec((tok.l_ctx, H_C * HEAD_W), seq),
                  pl.BlockSpec((tok.l_ctx, H_C * HEAD_W), seq),
                  pl.BlockSpec((tok.l_ctx, H_C * V_HEAD_C), seq)],
        out_specs=pl.BlockSpec((tok.l_ctx, H_C * V_HEAD_C), seq),
        out_shape=jax.ShapeDtypeStruct((tok.n, H_C * V_HEAD_C), BF16),
        compiler_params=_cparams(("arbitrary",)),
        name="attn_ctx",
    )(q, k, v)


def _attn_lat_call(tok, q, k, v, k_cache, v_cache, out_prev):
    blk0 = tok.n_ctx // tok.l_lat
    past = k_cache.shape[0] // tok.nb_lat
    tq = min(ATTN_TQ, tok.l_lat // 2)
    seq = lambda b, hh: (blk0 + b, hh)
    cache = lambda b, hh: (b, hh)
    return pl.pallas_call(
        _attn_lat_body,
        grid=(tok.nb_lat, H_C),
        in_specs=[pl.BlockSpec((tok.l_lat, HEAD_W), seq),
                  pl.BlockSpec((past, HEAD_W), cache),
                  pl.BlockSpec((past, V_HEAD_C), cache),
                  pl.BlockSpec((tok.l_lat, HEAD_W), seq),
                  pl.BlockSpec((tok.l_lat, V_HEAD_C), seq),
                  pl.BlockSpec(memory_space=pl.ANY)],
        out_specs=pl.BlockSpec((tok.l_lat, V_HEAD_C), seq),
        out_shape=jax.ShapeDtypeStruct(out_prev.shape, out_prev.dtype),
        input_output_aliases={5: 0},
        scratch_shapes=[pltpu.VMEM((tq, past + tok.l_lat), F32)] * 2 + [pltpu.VMEM((tq, LANE), F32)] * 2,
        compiler_params=_cparams(("arbitrary", "arbitrary")),
        name="attn_lat",
    )(q, k_cache, v_cache, k, v, out_prev)


def _swap_halves(w, head_dim):
    n = w.shape[-1]
    idx = np.arange(n)
    idx = (idx // head_dim) * head_dim + (idx % head_dim + head_dim // 2) % head_dim
    return w[..., idx]


def _pad_cols(w, width):
    return jnp.pad(w, ((0, 0), (0, width - w.shape[1])))


def _even_weights(w_in, w_gk2, b_gk2):
    sizes = (A_QK, A_QK, A_V, A_V, 2 * GATE_RANK, A_QK, A_QK, A_V, A_V)
    qa, ka, va, ga, gk, qb, kb, vb, gb = jnp.split(w_in, np.cumsum(sizes)[:-1].tolist(), axis=1)
    w = jnp.concatenate([qa, ka, va, ga, qb, kb, vb, gb, _pad_cols(gk, LANE),
                         _swap_halves(qb, DK), _swap_halves(kb, DK)], axis=1).astype(BF16)
    wg = jnp.zeros((LANE, 2 * A_QK), F32)
    wg = wg.at[0:GATE_RANK, 0:A_QK].set(w_gk2[0]).at[GATE_RANK:2 * GATE_RANK, A_QK:2 * A_QK].set(w_gk2[1])
    bg = b_gk2.reshape(1, 2 * A_QK)
    return w, wg.astype(BF16), bg


def _odd_weights(w_in, w_q_b, w_kv_b):
    q_lat, ckv, kpe = w_in[:, :Q_LORA], w_in[:, Q_LORA:Q_LORA + KV_LORA], w_in[:, Q_LORA + KV_LORA:]
    win = jnp.concatenate([q_lat, ckv, _pad_cols(kpe, LANE), _pad_cols(_swap_halves(kpe, QK_ROPE), LANE)],
                          axis=1).astype(BF16)
    wq = w_q_b.reshape(Q_LORA, H_C, QK_NOPE + QK_ROPE)
    nope = wq[:, :, :QK_NOPE].reshape(Q_LORA, H_C * QK_NOPE)
    rope = wq[:, :, QK_NOPE:]
    pad = lambda r: jnp.pad(r, ((0, 0), (0, 0), (0, LANE - QK_ROPE))).reshape(Q_LORA, H_C * LANE)
    wqb = jnp.concatenate([nope, pad(rope), pad(_swap_halves(rope, QK_ROPE))], axis=1).astype(BF16)
    wkv = w_kv_b.reshape(KV_LORA, H_C, QK_NOPE + V_HEAD_C)
    wkvb = jnp.concatenate([wkv[:, :, :QK_NOPE].reshape(KV_LORA, H_C * QK_NOPE),
                            wkv[:, :, QK_NOPE:].reshape(KV_LORA, H_C * V_HEAD_C)], axis=1).astype(BF16)
    return win, wqb, wkvb


def _rope_tables(tok):
    rows = tok.l_lat // GRID_W
    row = jnp.repeat(jnp.arange(rows), GRID_W).astype(F32)
    col = jnp.tile(jnp.arange(GRID_W), rows).astype(F32)
    n_freq = QK_ROPE // 4
    inv = ROPE_BASE ** (-jnp.arange(n_freq, dtype=F32) / n_freq)
    ang = jnp.concatenate([row[:, None] * inv, col[:, None] * inv], axis=-1)
    cos, sin = jnp.cos(ang), jnp.sin(ang)
    cc = jnp.tile(jnp.concatenate([cos, cos], axis=-1), (tok.nb_lat, LANE // QK_ROPE))
    ss = jnp.tile(jnp.concatenate([-sin, sin], axis=-1), (tok.nb_lat, LANE // QK_ROPE))
    cc = jnp.concatenate([jnp.ones((tok.n_ctx, LANE), F32), cc], axis=0)
    ss = jnp.concatenate([jnp.zeros((tok.n_ctx, LANE), F32), ss], axis=0)
    return cc, ss


def _states_to_kernel(s):
    nb, _, heads = s.shape[:3]
    st = jnp.swapaxes(s.reshape(nb, 2, heads // 2, 2, DK, DV), -1, -2)
    z = jnp.zeros_like(st[:, :, :, 0])
    rows = [jnp.concatenate([st[:, :, :, 0], z], axis=-1), jnp.concatenate([z, st[:, :, :, 1]], axis=-1)]
    return jnp.concatenate(rows, axis=-2)


def _states_from_kernel(st):
    nb, _, pairs = st.shape[:3]
    heads = jnp.stack([st[:, :, :, :DV, :DK], st[:, :, :, DV:, DK:]], axis=3)
    return jnp.swapaxes(heads, -1, -2).reshape(nb, 2, 2 * pairs, DK, DV)


def kernel(x_prompt, x_sample, cache_ckv, cache_kpe, state_gla, state_ret, c, c_ctx, w_ada, b_ada, norm_mix_pre, norm_mix_post, norm_mlp_pre, norm_mlp_post, w_in_even, w_gk2, b_gk2, gla_norm, ret_decay, w_out_even, w_in_odd, q_a_norm, w_q_b, kv_a_norm, w_kv_b, w_out_odd, w_mlp1, w_mlp2):
    nb_ctx, l_ctx, d = x_prompt.shape
    nb_lat, l_lat, _ = x_sample.shape
    depth = w_ada.shape[0]
    tok = _Tokens(nb_ctx, l_ctx, nb_lat, l_lat, d)
    assert nb_lat < MOD_ROWS and tok.n_ctx % l_lat == 0 and l_ctx % CHUNK == 0 and l_lat % (2 * CHUNK) == 0

    cond = jnp.concatenate([c, c_ctx[None, :], jnp.zeros((MOD_ROWS - nb_lat - 1, d), F32)], axis=0)
    mods = _ada_call(cond, w_ada, b_ada).reshape(depth, MOD_ROWS, 6, 1, d)
    cc, ss = _rope_tables(tok)
    x = jnp.concatenate([x_prompt.reshape(tok.n_ctx, d), x_sample.reshape(tok.n_lat, d)], axis=0)
    vec = lambda a: a.reshape(1, -1)
    w_out_even16, w_out_odd16 = w_out_even.astype(BF16), w_out_odd.astype(BF16)
    w_mlp1_16, w_mlp2_16 = w_mlp1.astype(BF16), w_mlp2.astype(BF16)

    new_ckv, new_kpe, new_gla, new_ret = [], [], [], []
    for l in range(depth):
        i = l // 2
        if l % 2 == 0:
            w, wg, bg = _even_weights(w_in_even[i], w_gk2[i], b_gk2[i])
            dec = jnp.repeat(ret_decay[i], DK, axis=-1)
            q, k, lf, lb, gt, v = _even_proj_call(tok, x, mods, l, vec(norm_mix_pre[l]), w, wg, bg, dec, cc, ss)
            gain = jnp.concatenate([jnp.tile(gla_norm[i], N_HEAD_SCAN), jnp.ones((A_V,), F32)]).reshape(1, 2 * A_V)
            s0_lat = _states_to_kernel(jnp.concatenate([state_gla[:, i], state_ret[:, i]], axis=2))
            m_ctx, s_fin = _scan_call(tok, q, k, lf, lb, v, gt, gain)
            (m,) = _scan_call(tok, q, k, lf, lb, v, gt, gain, s0=s0_lat, out_prev=m_ctx)
            s_fin = _states_from_kernel(s_fin)
            new_gla.append(s_fin[:, :, :N_HEAD_SCAN])
            new_ret.append(s_fin[:, :, N_HEAD_SCAN:])
            w_out = w_out_even16
        else:
            win, wqb, wkvb = _odd_weights(w_in_odd[i], w_q_b[i], w_kv_b[i])
            q, k, v, ckv, kpe = _mla_proj_call(tok, x, mods, l, vec(norm_mix_pre[l]), win, vec(q_a_norm[i]), wqb,
                                               vec(kv_a_norm[i]), wkvb, cc, ss)
            past = cache_ckv.shape[2]
            kpe_pad = jnp.pad(cache_kpe[:, i].reshape(nb_lat * past, QK_ROPE), ((0, 0), (0, LANE - QK_ROPE)))
            k_c, v_c = _cache_expand_call(cache_ckv[:, i].reshape(nb_lat * past, KV_LORA), kpe_pad, wkvb)
            m = _attn_lat_call(tok, q, k, v, k_c, v_c, _attn_ctx_call(tok, q, k, v))
            new_ckv.append(ckv[:tok.n_ctx].reshape(nb_ctx, l_ctx, KV_LORA))
            new_kpe.append(kpe[:tok.n_ctx, :QK_ROPE].reshape(nb_ctx, l_ctx, QK_ROPE))
            w_out = w_out_odd16
        x = _mix_mlp_call(tok, m, x, mods, l, w_out, i, vec(norm_mix_post[l]), vec(norm_mlp_pre[l]), w_mlp1_16,
                          w_mlp2_16, vec(norm_mlp_post[l]))

    return (x[:tok.n_ctx].reshape(nb_ctx, l_ctx, d), x[tok.n_ctx:].reshape(nb_lat, l_lat, d),
            jnp.stack(new_ckv, axis=1), jnp.stack(new_kpe, axis=1),
            jnp.stack(new_gla, axis=1), jnp.stack(new_ret, axis=1))
```

```python
import functools

import numpy as np
import jax
import jax.numpy as jnp
from jax import lax
from jax.experimental import pallas as pl
from jax.experimental.pallas import tpu as pltpu

F32 = jnp.float32
BF16 = jnp.bfloat16

EPS = 1e-6
ROPE_BASE = 10000.0
GRID_W = 64
CHUNK = 64
GATE_RANK = 16
GATE_NORM = 16.0
N_HEAD_SCAN = 4
DK = 64
DV = 128
H_C = 8
Q_LORA = 256
KV_LORA = 256
QK_NOPE = 128
QK_ROPE = 64
V_HEAD_C = 128
LANE = 128
MOD_ROWS = 16

VMEM_LIMIT = 56 * 1024 * 1024


def _cparams(sem):
    return pltpu.CompilerParams(dimension_semantics=sem, vmem_limit_bytes=VMEM_LIMIT)


def _dot(a, b):
    return jnp.dot(a, b, preferred_element_type=F32)


def _dot_nt(a, b):
    return lax.dot_general(a, b, (((1,), (1,)), ((), ())), preferred_element_type=F32)


def _dot_tn(a, b):
    return lax.dot_general(a, b, (((0,), (0,)), ((), ())), preferred_element_type=F32)


def _rms(x):
    return x * lax.rsqrt(jnp.mean(x * x, axis=-1, keepdims=True) + EPS)


def _silu(x):
    return x * jax.nn.sigmoid(x)


def _full(shape):
    n = len(shape)
    return pl.BlockSpec(shape, lambda *_: (0,) * n)


def _ada_body(cond_ref, w_ref, b_ref, o_ref):
    s = _silu(cond_ref[...]).astype(BF16)
    o_ref[...] = _dot(s, w_ref[...].astype(BF16)) + b_ref[...]


def _ada_call(cond, w_ada, b_ada):
    depth, d, n = w_ada.shape
    tn = 1536
    return pl.pallas_call(
        _ada_body,
        grid=(depth, n // tn),
        in_specs=[
            pl.BlockSpec((MOD_ROWS, d), lambda l, j: (0, 0)),
            pl.BlockSpec((None, d, tn), lambda l, j: (l, 0, j)),
            pl.BlockSpec((None, 1, tn), lambda l, j: (l, 0, j)),
        ],
        out_specs=pl.BlockSpec((None, MOD_ROWS, tn), lambda l, j: (l, 0, j)),
        out_shape=jax.ShapeDtypeStruct((depth, MOD_ROWS, n), F32),
        compiler_params=_cparams(("arbitrary", "arbitrary")),
        name="ada_mod",
    )(cond, w_ada, b_ada.reshape(depth, 1, n))


class _Tokens:
    def __init__(self, nb_ctx, l_ctx, nb_lat, l_lat, d):
        self.nb_ctx, self.l_ctx, self.nb_lat, self.l_lat, self.d = nb_ctx, l_ctx, nb_lat, l_lat, d
        self.n_ctx = nb_ctx * l_ctx
        self.n_lat = nb_lat * l_lat
        self.n = self.n_ctx + self.n_lat
        self.ctx_row = nb_lat

    def tile(self, want):
        t = want
        while self.n_ctx % t or self.l_lat % t:
            t //= 2
        return t

    def mod_spec(self, layer, chunk, tm):
        n_ctx, l_lat, ctx_row = self.n_ctx, self.l_lat, self.ctx_row

        def idx(i, *_):
            start = i * tm
            row = jnp.where(start < n_ctx, ctx_row, (start - n_ctx) // l_lat)
            return (layer, row, chunk, 0, 0)

        return pl.BlockSpec((None, None, None, 1, self.d), idx)

    def x_specs(self, x, tm):
        if not isinstance(x, tuple):
            return [pl.BlockSpec((tm, self.d), lambda i: (i, 0))], [x]
        nct = self.n_ctx // tm
        return [pl.BlockSpec((tm, self.d), lambda i: (jnp.minimum(i, nct - 1), 0)),
                pl.BlockSpec((tm, self.d), lambda i: (jnp.maximum(i - nct, 0), 0))], list(x)

    def rope_spec(self, tm):
        n_ctx, l_lat = self.n_ctx, self.l_lat

        def idx(i):
            start = i * tm
            return (jnp.where(start < n_ctx, 0, 1 + ((start - n_ctx) % l_lat) // tm), 0)

        return pl.BlockSpec((tm, LANE), idx)


def _read_x(x_refs, n_ctx_tiles):
    if len(x_refs) == 1:
        return x_refs[0][...]
    return jnp.where(pl.program_id(0) < n_ctx_tiles, x_refs[0][...], x_refs[1][...])


A_QK = N_HEAD_SCAN * DK
A_V = N_HEAD_SCAN * DV
E_QA, E_KA, E_VA, E_GA = 0, A_QK, 2 * A_QK, 2 * A_QK + A_V
E_QB = E_GA + A_V
E_KB = E_QB + A_QK
E_VB = E_KB + A_QK
E_GB = E_VB + A_V
E_GK = E_GB + A_V
E_QBS = E_GK + LANE
E_KBS = E_QBS + A_QK
E_COLS = E_KBS + A_QK


def _log_sigmoid(x):
    return jnp.minimum(x, 0.0) - jnp.log1p(jnp.exp(-jnp.abs(x)))


def _even_proj_body(*refs, n_x, n_ctx_tiles):
    x = _read_x(refs[:n_x], n_ctx_tiles)
    (g_ref, sh_ref, sc_ref, w_ref, wg_ref, bg_ref, dec_ref, cc_ref, ss_ref,
     q_ref, k_ref, lf_ref, lb_ref, gt_ref, v_ref) = refs[n_x:]
    tm = x.shape[0]
    h = (_rms(x) * g_ref[...] * (1.0 + sc_ref[...]) + sh_ref[...]).astype(BF16)

    def proj(start, width):
        return _dot(h, w_ref[:, start:start + width])

    cc = cc_ref[...]
    ss = ss_ref[...]
    scale = DK ** -0.5
    q_ref[:, 0:A_QK] = proj(E_QA, A_QK) * scale
    k_ref[:, 0:A_QK] = proj(E_KA, A_QK)
    qb, qbs, kb, kbs = proj(E_QB, A_QK), proj(E_QBS, A_QK), proj(E_KB, A_QK), proj(E_KBS, A_QK)
    for j in range(A_QK // LANE):
        sl = slice(j * LANE, (j + 1) * LANE)
        o = A_QK + j * LANE
        q_ref[:, o:o + LANE] = qb[:, sl] * cc + qbs[:, sl] * ss
        k_ref[:, o:o + LANE] = (kb[:, sl] * cc + kbs[:, sl] * ss) * scale
    v_ref[:, 0:A_V] = proj(E_VA, A_V).astype(BF16)
    v_ref[:, A_V:2 * A_V] = proj(E_VB, A_V).astype(BF16)
    gt_ref[:, 0:A_V] = proj(E_GA, A_V)
    gt_ref[:, A_V:2 * A_V] = proj(E_GB, A_V)
    gk = proj(E_GK, LANE).astype(BF16)
    la = _log_sigmoid(_dot(gk, wg_ref[...]) + bg_ref[...]) * (1.0 / GATE_NORM)
    lf_ref[:, 0:A_QK] = la[:, 0:A_QK]
    lb_ref[:, 0:A_QK] = la[:, A_QK:2 * A_QK]
    log_g = -jnp.exp(dec_ref[...])
    lf_ref[:, A_QK:2 * A_QK] = jnp.broadcast_to(log_g[0:1, :], (tm, A_QK))
    lb_ref[:, A_QK:2 * A_QK] = jnp.broadcast_to(log_g[1:2, :], (tm, A_QK))


def _even_proj_call(tok, x, mods, layer, g_pre, w, wg, bg, dec, cc, ss):
    tm = tok.tile(512)
    d = tok.d
    row = lambda i: (i, 0)
    outs = [
        jax.ShapeDtypeStruct((tok.n, 2 * A_QK), F32),
        jax.ShapeDtypeStruct((tok.n, 2 * A_QK), F32),
        jax.ShapeDtypeStruct((tok.n, 2 * A_QK), F32),
        jax.ShapeDtypeStruct((tok.n, 2 * A_QK), F32),
        jax.ShapeDtypeStruct((tok.n, 2 * A_V), F32),
        jax.ShapeDtypeStruct((tok.n, 2 * A_V), BF16),
    ]
    x_specs, x_args = tok.x_specs(x, tm)
    return pl.pallas_call(
        functools.partial(_even_proj_body, n_x=len(x_args), n_ctx_tiles=tok.n_ctx // tm),
        grid=(tok.n // tm,),
        in_specs=x_specs + [
            _full((1, d)),
            tok.mod_spec(layer, 0, tm),
            tok.mod_spec(layer, 1, tm),
            _full(w.shape), _full(wg.shape), _full(bg.shape), _full(dec.shape),
            tok.rope_spec(tm),
            tok.rope_spec(tm),
        ],
        out_specs=[pl.BlockSpec((tm, o.shape[1]), row) for o in outs],
        out_shape=outs,
        compiler_params=_cparams(("arbitrary",)),
        name="even_proj",
    )(*x_args, g_pre, mods, mods, w, wg, bg, dec, cc, ss)


SCAN_GROUP = 8


def _chunk_cumsum(x):
    row = lax.broadcasted_iota(jnp.int32, x.shape, 0) % CHUNK
    s = 1
    while s < CHUNK:
        x = x + jnp.where(row >= s, pltpu.roll(x, s, axis=0), 0.0)
        s *= 2
    return x


def _scan_body(q_ref, k_ref, lf_ref, lb_ref, v_ref, gt_ref, gain_ref, *rest, seq_len, group, context):
    if context:
        m_ref, sfin_ref, st_ref, o_acc = rest
    else:
        s0_ref, _, m_ref, st_ref, o_acc = rest
    C = CHUNK
    blk = group * C
    nblk = seq_len // blk
    pair_w = 2 * DK
    head0 = lax.broadcasted_iota(jnp.int32, (blk, pair_w), 1) < DK
    t_in = lax.broadcasted_iota(jnp.int32, (C, pair_w), 0)
    j_in = lax.broadcasted_iota(jnp.int32, (C, pair_w), 1) % DK
    keep_fwd = t_in >= j_in
    keep_bwd = t_in <= j_in
    on_diag = ((lax.broadcasted_iota(jnp.int32, (2 * DV, pair_w), 0) < DV)
               == (lax.broadcasted_iota(jnp.int32, (2 * DV, pair_w), 1) < DK))
    zeros_v = jnp.zeros((C, DV), BF16)

    if context:
        st_ref[...] = jnp.zeros(st_ref.shape, F32)
    else:
        st_ref[...] = s0_ref[...]

    def one_direction(r0, d, log_ref, keep, reverse):
        g = log_ref[pl.ds(r0, blk), :]
        b = _chunk_cumsum(g)
        tots = [b[C * j + C - 1:C * j + C, :] for j in range(group)]
        totb = jnp.concatenate([jnp.broadcast_to(t, (C, pair_w)) for t in tots], axis=0)
        if reverse:
            b = totb - b + g
        q = q_ref[pl.ds(r0, blk), :]
        k = k_ref[pl.ds(r0, blk), :]
        vblk = v_ref[pl.ds(r0, blk), :]
        q_dec = (q * jnp.exp(b)).astype(BF16)
        k_inv = k * jnp.exp(-b)
        k_up = (k * jnp.exp(totb - b)).astype(BF16)
        k_inv0 = jnp.where(head0, k_inv, 0.0).astype(BF16)
        k_inv1 = jnp.where(head0, 0.0, k_inv).astype(BF16)
        st = st_ref[d]
        outs = [None] * group
        for j in (reversed(range(group)) if reverse else range(group)):
            sl = slice(C * j, C * (j + 1))
            k_bd = jnp.concatenate([k_inv0[sl], k_inv1[sl]], axis=0)
            a = jnp.where(keep, _dot_nt(q_dec[sl], k_bd), 0.0).astype(BF16)
            vc = vblk[sl]
            v_bd = jnp.concatenate([jnp.concatenate([vc[:, :DV], zeros_v], axis=1),
                                    jnp.concatenate([zeros_v, vc[:, DV:]], axis=1)], axis=0)
            outs[j] = _dot_nt(q_dec[sl], st.astype(BF16)) + _dot(a, v_bd)
            st = st * jnp.exp(tots[j]) + jnp.where(on_diag, _dot_tn(vc, k_up[sl]), 0.0)
        st_ref[d] = st
        return jnp.concatenate(outs, axis=0)

    def body(c, carry, accumulate):
        rf = pl.multiple_of(c * blk, blk)
        rb = pl.multiple_of((nblk - 1 - c) * blk, blk)
        o_f = one_direction(rf, 0, lf_ref, keep_fwd, False)
        o_b = one_direction(rb, 1, lb_ref, keep_bwd, True)
        if accumulate:
            o_acc[pl.ds(rf, blk), :] += o_f
            o_acc[pl.ds(rb, blk), :] += o_b
        else:
            o_acc[pl.ds(rf, blk), :] = o_f
            o_acc[pl.ds(rb, blk), :] = o_b
        return carry

    lax.fori_loop(0, nblk // 2, functools.partial(body, accumulate=False), 0)
    lax.fori_loop(nblk // 2, nblk, functools.partial(body, accumulate=True), 0)

    if context:
        sfin_ref[...] = st_ref[...]

    tr = min(seq_len, 256)

    def finish(i, carry):
        r0 = pl.multiple_of(i * tr, tr)
        o = o_acc[pl.ds(r0, tr), :]
        gate = _silu(gt_ref[pl.ds(r0, tr), :])
        gain = gain_ref[...]
        for hh in range(2):
            sl = slice(hh * DV, (hh + 1) * DV)
            m_ref[pl.ds(r0, tr), sl] = (_rms(o[:, sl]) * gain[:, sl] * gate[:, sl]).astype(BF16)
        return carry

    lax.fori_loop(0, seq_len // tr, finish, 0)


def _scan_call(tok, q, k, lf, lb, v, gt, gain, s0=None, out_prev=None):
    context = s0 is None
    if context:
        nb, seq_len, blk0 = tok.nb_ctx, tok.l_ctx, 0
    else:
        nb, seq_len, blk0 = tok.nb_lat, tok.l_lat, tok.n_ctx // tok.l_lat
    pairs = q.shape[1] // (2 * DK)
    group = min(SCAN_GROUP, seq_len // CHUNK // 2)
    seq = lambda b, p: (blk0 + b, p)
    st_spec = pl.BlockSpec((None, 2, None, 2 * DV, 2 * DK), lambda b, p: (b, 0, p, 0, 0))
    in_specs = [
        pl.BlockSpec((seq_len, 2 * DK), seq),
        pl.BlockSpec((seq_len, 2 * DK), seq),
        pl.BlockSpec((seq_len, 2 * DK), seq),
        pl.BlockSpec((seq_len, 2 * DK), seq),
        pl.BlockSpec((seq_len, 2 * DV), seq),
        pl.BlockSpec((seq_len, 2 * DV), seq),
        pl.BlockSpec((1, 2 * DV), lambda b, p: (0, p)),
    ]
    args = [q, k, lf, lb, v, gt, gain]
    out_shape = [jax.ShapeDtypeStruct((tok.n, pairs * 2 * DV), BF16)]
    out_specs = [pl.BlockSpec((seq_len, 2 * DV), seq)]
    aliases = {}
    if context:
        out_shape.append(jax.ShapeDtypeStruct((nb, 2, pairs, 2 * DV, 2 * DK), F32))
        out_specs.append(st_spec)
    else:
        in_specs += [st_spec, pl.BlockSpec(memory_space=pl.ANY)]
        args += [s0, out_prev]
        aliases = {len(args) - 1: 0}
    return pl.pallas_call(
        functools.partial(_scan_body, seq_len=seq_len, group=group, context=context),
        grid=(nb, pairs),
        in_specs=in_specs,
        out_specs=out_specs,
        out_shape=out_shape,
        input_output_aliases=aliases,
        scratch_shapes=[pltpu.VMEM((2, 2 * DV, 2 * DK), F32), pltpu.VMEM((seq_len, 2 * DV), F32)],
        compiler_params=_cparams(("arbitrary", "arbitrary")),
        name="scan_ctx" if context else "scan_lat",
    )(*args)


def _mix_mlp_body(*refs, n_x, n_out, n_ctx_tiles):
    x = _read_x(refs[:n_x], n_ctx_tiles)
    (m_ref, wo_ref, gmix_ref, gate1_ref, gpre_ref, sh_ref, sc_ref, w1_ref, w2_ref,
     gpost_ref, gate2_ref) = refs[n_x:len(refs) - n_out]
    out_refs = refs[len(refs) - n_out:]
    y = _dot(m_ref[...], wo_ref[...])
    x1 = x + gate1_ref[...] * (_rms(y) * gmix_ref[...])
    h = (_rms(x1) * gpre_ref[...] * (1.0 + sc_ref[...]) + sh_ref[...]).astype(BF16)
    u = jnp.maximum(_dot(h, w1_ref[...]), 0.0)
    z = _dot((u * u).astype(BF16), w2_ref[...])
    res = x1 + gate2_ref[...] * (_rms(z) * gpost_ref[...])
    if n_out == 1:
        out_refs[0][...] = res
    else:
        is_ctx = pl.program_id(0) < n_ctx_tiles

        @pl.when(is_ctx)
        def _():
            out_refs[0][...] = res

        @pl.when(jnp.logical_not(is_ctx))
        def _():
            out_refs[1][...] = res


def _mix_mlp_call(tok, m, x, mods, layer, w_out, w_out_idx, g_mix, g_pre, w1, w2, g_post, split_out=False):
    tm = tok.tile(512)
    d = tok.d
    row = lambda i: (i, 0)
    resident = lambda w, idx: pl.BlockSpec((None,) + w.shape[1:], lambda i: (idx, 0, 0),
                                           pipeline_mode=pl.Buffered(1))
    x_specs, x_args = tok.x_specs(x, tm)
    nct = tok.n_ctx // tm
    if split_out:
        out_specs = [pl.BlockSpec((tm, d), lambda i: (jnp.minimum(i, nct - 1), 0)),
                     pl.BlockSpec((tm, d), lambda i: (jnp.maximum(i - nct, 0), 0))]
        out_shape = [jax.ShapeDtypeStruct((tok.n_ctx, d), F32), jax.ShapeDtypeStruct((tok.n_lat, d), F32)]
    else:
        out_specs = [pl.BlockSpec((tm, d), row)]
        out_shape = [jax.ShapeDtypeStruct((tok.n, d), F32)]
    return pl.pallas_call(
        functools.partial(_mix_mlp_body, n_x=len(x_args), n_out=len(out_shape), n_ctx_tiles=nct),
        grid=(tok.n // tm,),
        in_specs=x_specs + [
            pl.BlockSpec((tm, m.shape[1]), row),
            resident(w_out, w_out_idx),
            _full((1, d)),
            tok.mod_spec(layer, 2, tm),
            _full((1, d)),
            tok.mod_spec(layer, 3, tm),
            tok.mod_spec(layer, 4, tm),
            resident(w1, layer),
            resident(w2, layer),
            _full((1, d)),
            tok.mod_spec(layer, 5, tm),
        ],
        out_specs=out_specs,
        out_shape=out_shape,
        compiler_params=_cparams(("arbitrary",)),
        name="mix_mlp",
    )(*x_args, m, w_out, g_mix, mods, g_pre, mods, mods, w1, w2, g_post, mods)


HEAD_W = 2 * LANE
O_QLAT, O_CKV, O_KPE, O_KPES, O_COLS = 0, Q_LORA, Q_LORA + KV_LORA, Q_LORA + KV_LORA + LANE, Q_LORA + KV_LORA + 2 * LANE
QB_NOPE, QB_ROPE, QB_SWAP = 0, H_C * LANE, 2 * H_C * LANE


def _expand_kv(cb, kper, wkvb_ref, k_ref, v_ref):
    for hp in range(H_C // 2):
        nope2 = _dot(cb, wkvb_ref[:, hp * 2 * LANE:(hp + 1) * 2 * LANE])
        for j in range(2):
            hh = 2 * hp + j
            k_ref[:, hh * HEAD_W:hh * HEAD_W + LANE] = nope2[:, j * LANE:(j + 1) * LANE].astype(BF16)
            k_ref[:, hh * HEAD_W + LANE:(hh + 1) * HEAD_W] = kper
    v_ref[...] = _dot(cb, wkvb_ref[:, H_C * LANE:2 * H_C * LANE]).astype(BF16)


def _mla_proj_body(x_ref, g_ref, sh_ref, sc_ref, win_ref, qn_ref, wqb_ref, kvn_ref, wkvb_ref, cc_ref, ss_ref,
                   q_ref, k_ref, v_ref, ckv_ref, kpe_ref):
    h = (_rms(x_ref[...]) * g_ref[...] * (1.0 + sc_ref[...]) + sh_ref[...]).astype(BF16)
    cc = cc_ref[...]
    ss = ss_ref[...]
    qn = (_rms(_dot(h, win_ref[:, O_QLAT:O_QLAT + Q_LORA])) * qn_ref[...]).astype(BF16)
    ckvn = _rms(_dot(h, win_ref[:, O_CKV:O_CKV + KV_LORA])) * kvn_ref[...]
    kpe2 = _dot(h, win_ref[:, O_KPE:O_KPE + 2 * LANE])
    kpe = kpe2[:, 0:LANE]
    kper = (kpe * cc + kpe2[:, LANE:2 * LANE] * ss).astype(BF16)
    ckv_ref[...] = ckvn
    kpe_ref[...] = kpe
    for hp in range(H_C // 2):
        o = hp * 2 * LANE
        nope2 = _dot(qn, wqb_ref[:, QB_NOPE + o:QB_NOPE + o + 2 * LANE])
        rope2 = _dot(qn, wqb_ref[:, QB_ROPE + o:QB_ROPE + o + 2 * LANE])
        swap2 = _dot(qn, wqb_ref[:, QB_SWAP + o:QB_SWAP + o + 2 * LANE])
        for j in range(2):
            hh = 2 * hp + j
            sl = slice(j * LANE, (j + 1) * LANE)
            rot = rope2[:, sl] * cc + swap2[:, sl] * ss
            q_ref[:, hh * HEAD_W:hh * HEAD_W + LANE] = (nope2[:, sl] * ATTN_Q_SCALE).astype(BF16)
            q_ref[:, hh * HEAD_W + LANE:(hh + 1) * HEAD_W] = (rot * ATTN_Q_SCALE).astype(BF16)
    _expand_kv(ckvn.astype(BF16), kper, wkvb_ref, k_ref, v_ref)


def _mla_proj_call(tok, x, mods, layer, g_pre, win, qn, wqb, kvn, wkvb, cc, ss):
    tm = tok.tile(512)
    d = tok.d
    row = lambda i: (i, 0)
    outs = [
        jax.ShapeDtypeStruct((tok.n, H_C * HEAD_W), BF16),
        jax.ShapeDtypeStruct((tok.n, H_C * HEAD_W), BF16),
        jax.ShapeDtypeStruct((tok.n, H_C * V_HEAD_C), BF16),
        jax.ShapeDtypeStruct((tok.n, KV_LORA), F32),
        jax.ShapeDtypeStruct((tok.n, LANE), F32),
    ]
    return pl.pallas_call(
        _mla_proj_body,
        grid=(tok.n // tm,),
        in_specs=[
            pl.BlockSpec((tm, d), row),
            _full((1, d)),
            tok.mod_spec(layer, 0, tm),
            tok.mod_spec(layer, 1, tm),
            _full(win.shape), _full(qn.shape), _full(wqb.shape), _full(kvn.shape), _full(wkvb.shape),
            tok.rope_spec(tm),
            tok.rope_spec(tm),
        ],
        out_specs=[pl.BlockSpec((tm, o.shape[1]), row) for o in outs],
        out_shape=outs,
        compiler_params=_cparams(("arbitrary",)),
        name="mla_proj",
    )(x, g_pre, mods, mods, win, qn, wqb, kvn, wkvb, cc, ss)


def _cache_expand_body(ckv_ref, kpe_ref, wkvb_ref, k_ref, v_ref):
    _expand_kv(ckv_ref[...].astype(BF16), kpe_ref[...].astype(BF16), wkvb_ref, k_ref, v_ref)


def _cache_expand_call(ckv, kpe_pad, wkvb):
    n = ckv.shape[0]
    tm = 512
    while n % tm:
        tm //= 2
    row = lambda i: (i, 0)
    outs = [jax.ShapeDtypeStruct((n, H_C * HEAD_W), BF16), jax.ShapeDtypeStruct((n, H_C * V_HEAD_C), BF16)]
    return pl.pallas_call(
        _cache_expand_body,
        grid=(n // tm,),
        in_specs=[pl.BlockSpec((tm, KV_LORA), row), pl.BlockSpec((tm, LANE), row), _full(wkvb.shape)],
        out_specs=[pl.BlockSpec((tm, o.shape[1]), row) for o in outs],
        out_shape=outs,
        compiler_params=_cparams(("arbitrary",)),
        name="cache_expand",
    )(ckv, kpe_pad, wkvb)


ATTN_TQ = 256
ATTN_TK = 512
ATTN_HEADS_PER_STEP = 2
ATTN_Q_SCALE = (QK_NOPE + QK_ROPE) ** -0.5 * float(np.log2(np.e))


def _softmax_pv(s, values):
    p = jnp.exp2(s - jnp.max(s, axis=-1, keepdims=True))
    den = jnp.sum(p, axis=-1, keepdims=True)
    p = p.astype(BF16)
    acc = functools.reduce(jnp.add, [_dot(p[:, k0:k0 + vb.shape[0]], vb) for vb, k0 in values])
    return (acc / den).astype(BF16)


def _attn_ctx_body(q_ref, k_ref, v_ref, o_ref):
    for hh in range(H_C):
        s = _dot_nt(q_ref[:, hh * HEAD_W:(hh + 1) * HEAD_W], k_ref[:, hh * HEAD_W:(hh + 1) * HEAD_W])
        o_ref[:, hh * V_HEAD_C:(hh + 1) * V_HEAD_C] = _softmax_pv(s, [(v_ref[:, hh * V_HEAD_C:(hh + 1) * V_HEAD_C], 0)])


def _lane_groups(x):
    return [x[:, g:g + LANE] for g in range(0, x.shape[1], LANE)]


def _attn_lat_body(q_ref, kc_ref, vc_ref, k_ref, v_ref, prev_ref, o_ref, sa_ref, sb_ref, ma_ref, mb_ref, vx_ref):
    del prev_ref
    past, n_self = kc_ref.shape[0], k_ref.shape[0]
    heads = vx_ref.shape[0]
    tq = sa_ref.shape[0]
    n = q_ref.shape[0] // tq
    tk = min(ATTN_TK, n_self)
    blocks = [(kc_ref, 0, past, 0)] + [(k_ref, k0, tk, past + k0) for k0 in range(0, n_self, tk)]

    for h in range(heads):
        vx_ref[h, 0:past, 0:V_HEAD_C] = vc_ref[:, h * V_HEAD_C:(h + 1) * V_HEAD_C]
        vx_ref[h, past:past + n_self, 0:V_HEAD_C] = v_ref[:, h * V_HEAD_C:(h + 1) * V_HEAD_C]
        vx_ref[h, :, V_HEAD_C:2 * V_HEAD_C] = jnp.ones((past + n_self, V_HEAD_C), BF16)

    def scores(h, i, s_ref, m_ref):
        q = q_ref[pl.ds(pl.multiple_of(i * tq, tq), tq), h * HEAD_W:(h + 1) * HEAD_W]
        mx = None
        for kk_ref, k0, size, col in blocks:
            s = _dot_nt(q, kk_ref[k0:k0 + size, h * HEAD_W:(h + 1) * HEAD_W])
            s_ref[:, col:col + size] = s
            mx = functools.reduce(jnp.maximum, _lane_groups(s) + ([] if mx is None else [mx]))
        m_ref[...] = mx

    def finish(h, i, s_ref, m_ref):
        m = jnp.max(m_ref[...], axis=-1, keepdims=True)
        acc = None
        for _, _, size, col in blocks:
            p = jnp.exp2(s_ref[:, col:col + size] - m).astype(BF16)
            pv = _dot(p, vx_ref[h, col:col + size, :])
            acc = pv if acc is None else acc + pv
        out = acc[:, 0:V_HEAD_C] / acc[:, V_HEAD_C:2 * V_HEAD_C]
        o_ref[pl.ds(pl.multiple_of(i * tq, tq), tq), h * V_HEAD_C:(h + 1) * V_HEAD_C] = out.astype(BF16)

    scores(0, 0, sa_ref, ma_ref)
    for h in range(heads):

        def two_tiles(j, carry, h=h):
            i = 2 * j
            scores(h, i + 1, sb_ref, mb_ref)
            finish(h, i, sa_ref, ma_ref)
            scores(h, i + 2, sa_ref, ma_ref)
            finish(h, i + 1, sb_ref, mb_ref)
            return carry

        lax.fori_loop(0, n // 2 - 1, two_tiles, 0)
        scores(h, n - 1, sb_ref, mb_ref)
        finish(h, n - 2, sa_ref, ma_ref)
        if h + 1 < heads:
            scores(h + 1, 0, sa_ref, ma_ref)
        finish(h, n - 1, sb_ref, mb_ref)


def _attn_ctx_call(tok, q, k, v):
    seq = lambda b: (b, 0)
    return pl.pallas_call(
        _attn_ctx_body,
        grid=(tok.nb_ctx,),
        in_specs=[pl.BlockSpec((tok.l_ctx, H_C * HEAD_W), seq),
                  pl.BlockSpec((tok.l_ctx, H_C * HEAD_W), seq),
                  pl.BlockSpec((tok.l_ctx, H_C * V_HEAD_C), seq)],
        out_specs=pl.BlockSpec((tok.l_ctx, H_C * V_HEAD_C), seq),
        out_shape=jax.ShapeDtypeStruct((tok.n, H_C * V_HEAD_C), BF16),
        compiler_params=_cparams(("arbitrary",)),
        name="attn_ctx",
    )(q, k, v)


def _attn_lat_call(tok, q, k, v, k_cache, v_cache, out_prev):
    blk0 = tok.n_ctx // tok.l_lat
    past = k_cache.shape[0] // tok.nb_lat
    tq = min(ATTN_TQ, tok.l_lat // 2)
    hg = ATTN_HEADS_PER_STEP
    seq = lambda b, hh: (blk0 + b, hh)
    cache = lambda b, hh: (b, hh)
    return pl.pallas_call(
        _attn_lat_body,
        grid=(tok.nb_lat, H_C // hg),
        in_specs=[pl.BlockSpec((tok.l_lat, hg * HEAD_W), seq),
                  pl.BlockSpec((past, hg * HEAD_W), cache),
                  pl.BlockSpec((past, hg * V_HEAD_C), cache),
                  pl.BlockSpec((tok.l_lat, hg * HEAD_W), seq),
                  pl.BlockSpec((tok.l_lat, hg * V_HEAD_C), seq),
                  pl.BlockSpec(memory_space=pl.ANY)],
        out_specs=pl.BlockSpec((tok.l_lat, hg * V_HEAD_C), seq),
        out_shape=jax.ShapeDtypeStruct(out_prev.shape, out_prev.dtype),
        input_output_aliases={5: 0},
        scratch_shapes=([pltpu.VMEM((tq, past + tok.l_lat), F32)] * 2 + [pltpu.VMEM((tq, LANE), F32)] * 2
                        + [pltpu.VMEM((hg, past + tok.l_lat, 2 * V_HEAD_C), BF16)]),
        compiler_params=_cparams(("arbitrary", "arbitrary")),
        name="attn_lat",
    )(q, k_cache, v_cache, k, v, out_prev)


def _swap_halves(w, head_dim):
    n = w.shape[-1]
    idx = np.arange(n)
    idx = (idx // head_dim) * head_dim + (idx % head_dim + head_dim // 2) % head_dim
    return w[..., idx]


def _pad_cols(w, width):
    return jnp.pad(w, ((0, 0), (0, width - w.shape[1])))


def _even_weights(w_in, w_gk2, b_gk2):
    sizes = (A_QK, A_QK, A_V, A_V, 2 * GATE_RANK, A_QK, A_QK, A_V, A_V)
    qa, ka, va, ga, gk, qb, kb, vb, gb = jnp.split(w_in, np.cumsum(sizes)[:-1].tolist(), axis=1)
    w = jnp.concatenate([qa, ka, va, ga, qb, kb, vb, gb, _pad_cols(gk, LANE),
                         _swap_halves(qb, DK), _swap_halves(kb, DK)], axis=1).astype(BF16)
    wg = jnp.zeros((LANE, 2 * A_QK), F32)
    wg = wg.at[0:GATE_RANK, 0:A_QK].set(w_gk2[0]).at[GATE_RANK:2 * GATE_RANK, A_QK:2 * A_QK].set(w_gk2[1])
    bg = b_gk2.reshape(1, 2 * A_QK)
    return w, wg.astype(BF16), bg


def _odd_weights(w_in, w_q_b, w_kv_b):
    q_lat, ckv, kpe = w_in[:, :Q_LORA], w_in[:, Q_LORA:Q_LORA + KV_LORA], w_in[:, Q_LORA + KV_LORA:]
    win = jnp.concatenate([q_lat, ckv, _pad_cols(kpe, LANE), _pad_cols(_swap_halves(kpe, QK_ROPE), LANE)],
                          axis=1).astype(BF16)
    wq = w_q_b.reshape(Q_LORA, H_C, QK_NOPE + QK_ROPE)
    nope = wq[:, :, :QK_NOPE].reshape(Q_LORA, H_C * QK_NOPE)
    rope = wq[:, :, QK_NOPE:]
    pad = lambda r: jnp.pad(r, ((0, 0), (0, 0), (0, LANE - QK_ROPE))).reshape(Q_LORA, H_C * LANE)
    wqb = jnp.concatenate([nope, pad(rope), pad(_swap_halves(rope, QK_ROPE))], axis=1).astype(BF16)
    wkv = w_kv_b.reshape(KV_LORA, H_C, QK_NOPE + V_HEAD_C)
    wkvb = jnp.concatenate([wkv[:, :, :QK_NOPE].reshape(KV_LORA, H_C * QK_NOPE),
                            wkv[:, :, QK_NOPE:].reshape(KV_LORA, H_C * V_HEAD_C)], axis=1).astype(BF16)
    return win, wqb, wkvb


def _rope_tables(tok, tm):
    rows = tok.l_lat // GRID_W
    row = jnp.repeat(jnp.arange(rows), GRID_W).astype(F32)
    col = jnp.tile(jnp.arange(GRID_W), rows).astype(F32)
    n_freq = QK_ROPE // 4
    inv = ROPE_BASE ** (-jnp.arange(n_freq, dtype=F32) / n_freq)
    ang = jnp.concatenate([row[:, None] * inv, col[:, None] * inv], axis=-1)
    cos, sin = jnp.cos(ang), jnp.sin(ang)
    cc = jnp.tile(jnp.concatenate([cos, cos], axis=-1), (1, LANE // QK_ROPE))
    ss = jnp.tile(jnp.concatenate([-sin, sin], axis=-1), (1, LANE // QK_ROPE))
    cc = jnp.concatenate([jnp.ones((tm, LANE), F32), cc], axis=0)
    ss = jnp.concatenate([jnp.zeros((tm, LANE), F32), ss], axis=0)
    return cc, ss


def _states_to_kernel(s):
    nb, _, heads = s.shape[:3]
    st = jnp.swapaxes(s.reshape(nb, 2, heads // 2, 2, DK, DV), -1, -2)
    z = jnp.zeros_like(st[:, :, :, 0])
    rows = [jnp.concatenate([st[:, :, :, 0], z], axis=-1), jnp.concatenate([z, st[:, :, :, 1]], axis=-1)]
    return jnp.concatenate(rows, axis=-2)


def _states_from_kernel(st):
    nb, _, pairs = st.shape[:3]
    heads = jnp.stack([st[:, :, :, :DV, :DK], st[:, :, :, DV:, DK:]], axis=3)
    return jnp.swapaxes(heads, -1, -2).reshape(nb, 2, 2 * pairs, DK, DV)


def kernel(x_prompt, x_sample, cache_ckv, cache_kpe, state_gla, state_ret, c, c_ctx, w_ada, b_ada, norm_mix_pre, norm_mix_post, norm_mlp_pre, norm_mlp_post, w_in_even, w_gk2, b_gk2, gla_norm, ret_decay, w_out_even, w_in_odd, q_a_norm, w_q_b, kv_a_norm, w_kv_b, w_out_odd, w_mlp1, w_mlp2):
    nb_ctx, l_ctx, d = x_prompt.shape
    nb_lat, l_lat, _ = x_sample.shape
    depth = w_ada.shape[0]
    tok = _Tokens(nb_ctx, l_ctx, nb_lat, l_lat, d)
    assert nb_lat < MOD_ROWS and tok.n_ctx % l_lat == 0 and l_ctx % CHUNK == 0 and l_lat % (2 * CHUNK) == 0

    cond = jnp.concatenate([c, c_ctx[None, :], jnp.zeros((MOD_ROWS - nb_lat - 1, d), F32)], axis=0)
    mods = _ada_call(cond, w_ada, b_ada).reshape(depth, MOD_ROWS, 6, 1, d)
    cc, ss = _rope_tables(tok, tok.tile(512))
    x = (x_prompt.reshape(tok.n_ctx, d), x_sample.reshape(tok.n_lat, d))
    vec = lambda a: a.reshape(1, -1)
    w_out_even16, w_out_odd16 = w_out_even.astype(BF16), w_out_odd.astype(BF16)
    w_mlp1_16, w_mlp2_16 = w_mlp1.astype(BF16), w_mlp2.astype(BF16)

    new_ckv, new_kpe, new_gla, new_ret = [], [], [], []
    for l in range(depth):
        i = l // 2
        if l % 2 == 0:
            w, wg, bg = _even_weights(w_in_even[i], w_gk2[i], b_gk2[i])
            dec = jnp.repeat(ret_decay[i], DK, axis=-1)
            q, k, lf, lb, gt, v = _even_proj_call(tok, x, mods, l, vec(norm_mix_pre[l]), w, wg, bg, dec, cc, ss)
            gain = jnp.concatenate([jnp.tile(gla_norm[i], N_HEAD_SCAN), jnp.ones((A_V,), F32)]).reshape(1, 2 * A_V)
            s0_lat = _states_to_kernel(jnp.concatenate([state_gla[:, i], state_ret[:, i]], axis=2))
            m_ctx, s_fin = _scan_call(tok, q, k, lf, lb, v, gt, gain)
            (m,) = _scan_call(tok, q, k, lf, lb, v, gt, gain, s0=s0_lat, out_prev=m_ctx)
            s_fin = _states_from_kernel(s_fin)
            new_gla.append(s_fin[:, :, :N_HEAD_SCAN])
            new_ret.append(s_fin[:, :, N_HEAD_SCAN:])
            w_out = w_out_even16
        else:
            win, wqb, wkvb = _odd_weights(w_in_odd[i], w_q_b[i], w_kv_b[i])
            q, k, v, ckv, kpe = _mla_proj_call(tok, x, mods, l, vec(norm_mix_pre[l]), win, vec(q_a_norm[i]), wqb,
                                               vec(kv_a_norm[i]), wkvb, cc, ss)
            past = cache_ckv.shape[2]
            kpe_pad = jnp.pad(cache_kpe[:, i].reshape(nb_lat * past, QK_ROPE), ((0, 0), (0, LANE - QK_ROPE)))
            k_c, v_c = _cache_expand_call(cache_ckv[:, i].reshape(nb_lat * past, KV_LORA), kpe_pad, wkvb)
            m = _attn_lat_call(tok, q, k, v, k_c, v_c, _attn_ctx_call(tok, q, k, v))
            new_ckv.append(ckv[:tok.n_ctx].reshape(nb_ctx, l_ctx, KV_LORA))
            new_kpe.append(kpe[:tok.n_ctx, :QK_ROPE].reshape(nb_ctx, l_ctx, QK_ROPE))
            w_out = w_out_odd16
        x = _mix_mlp_call(tok, m, x, mods, l, w_out, i, vec(norm_mix_post[l]), vec(norm_mlp_pre[l]), w_mlp1_16,
                          w_mlp2_16, vec(norm_mlp_post[l]), split_out=(l == depth - 1))
        x = x[0] if len(x) == 1 else tuple(x)

    return (x[0].reshape(nb_ctx, l_ctx, d), x[1].reshape(nb_lat, l_lat, d),
            jnp.stack(new_ckv, axis=1), jnp.stack(new_kpe, axis=1),
            jnp.stack(new_gla, axis=1), jnp.stack(new_ret, axis=1))
```

```python
import functools

import numpy as np
import jax
import jax.numpy as jnp
from jax import lax
from jax.experimental import pallas as pl
from jax.experimental.pallas import tpu as pltpu

F32 = jnp.float32
BF16 = jnp.bfloat16

EPS = 1e-6
ROPE_BASE = 10000.0
GRID_W = 64
CHUNK = 64
GATE_RANK = 16
GATE_NORM = 16.0
N_HEAD_SCAN = 4
DK = 64
DV = 128
H_C = 8
Q_LORA = 256
KV_LORA = 256
QK_NOPE = 128
QK_ROPE = 64
V_HEAD_C = 128
LANE = 128
MOD_ROWS = 16

VMEM_LIMIT = 56 * 1024 * 1024


def _cparams(sem):
    return pltpu.CompilerParams(dimension_semantics=sem, vmem_limit_bytes=VMEM_LIMIT)


def _dot(a, b):
    return jnp.dot(a, b, preferred_element_type=F32)


def _dot_nt(a, b):
    return lax.dot_general(a, b, (((1,), (1,)), ((), ())), preferred_element_type=F32)


def _dot_tn(a, b):
    return lax.dot_general(a, b, (((0,), (0,)), ((), ())), preferred_element_type=F32)


def _rms(x):
    return x * lax.rsqrt(jnp.mean(x * x, axis=-1, keepdims=True) + EPS)


def _silu(x):
    return x * jax.nn.sigmoid(x)


def _full(shape):
    n = len(shape)
    return pl.BlockSpec(shape, lambda *_: (0,) * n)


def _ada_body(cond_ref, w_ref, b_ref, o_ref):
    s = _silu(cond_ref[...]).astype(BF16)
    o_ref[...] = _dot(s, w_ref[...].astype(BF16)) + b_ref[...]


def _ada_call(cond, w_ada, b_ada):
    depth, d, n = w_ada.shape
    tn = 1536
    return pl.pallas_call(
        _ada_body,
        grid=(depth, n // tn),
        in_specs=[
            pl.BlockSpec((MOD_ROWS, d), lambda l, j: (0, 0)),
            pl.BlockSpec((None, d, tn), lambda l, j: (l, 0, j)),
            pl.BlockSpec((None, 1, tn), lambda l, j: (l, 0, j)),
        ],
        out_specs=pl.BlockSpec((None, MOD_ROWS, tn), lambda l, j: (l, 0, j)),
        out_shape=jax.ShapeDtypeStruct((depth, MOD_ROWS, n), F32),
        compiler_params=_cparams(("arbitrary", "arbitrary")),
        name="ada_mod",
    )(cond, w_ada, b_ada.reshape(depth, 1, n))


class _Tokens:
    def __init__(self, nb_ctx, l_ctx, nb_lat, l_lat, d):
        self.nb_ctx, self.l_ctx, self.nb_lat, self.l_lat, self.d = nb_ctx, l_ctx, nb_lat, l_lat, d
        self.n_ctx = nb_ctx * l_ctx
        self.n_lat = nb_lat * l_lat
        self.n = self.n_ctx + self.n_lat
        self.ctx_row = nb_lat

    def tile(self, want):
        t = want
        while self.n_ctx % t or self.l_lat % t:
            t //= 2
        return t

    def mod_spec(self, layer, chunk, tm):
        n_ctx, l_lat, ctx_row = self.n_ctx, self.l_lat, self.ctx_row

        def idx(i, *_):
            start = i * tm
            row = jnp.where(start < n_ctx, ctx_row, (start - n_ctx) // l_lat)
            return (layer, row, chunk, 0, 0)

        return pl.BlockSpec((None, None, None, 1, self.d), idx)

    def x_specs(self, x, tm):
        if not isinstance(x, tuple):
            return [pl.BlockSpec((tm, self.d), lambda i: (i, 0))], [x]
        nct = self.n_ctx // tm
        return [pl.BlockSpec((tm, self.d), lambda i: (jnp.minimum(i, nct - 1), 0)),
                pl.BlockSpec((tm, self.d), lambda i: (jnp.maximum(i - nct, 0), 0))], list(x)

    def rope_spec(self, tm):
        n_ctx, l_lat = self.n_ctx, self.l_lat

        def idx(i):
            start = i * tm
            return (jnp.where(start < n_ctx, 0, 1 + ((start - n_ctx) % l_lat) // tm), 0)

        return pl.BlockSpec((tm, LANE), idx)


def _read_x(x_refs, n_ctx_tiles):
    if len(x_refs) == 1:
        return x_refs[0][...]
    return jnp.where(pl.program_id(0) < n_ctx_tiles, x_refs[0][...], x_refs[1][...])


A_QK = N_HEAD_SCAN * DK
A_V = N_HEAD_SCAN * DV
E_QA, E_KA, E_VA, E_GA = 0, A_QK, 2 * A_QK, 2 * A_QK + A_V
E_QB = E_GA + A_V
E_KB = E_QB + A_QK
E_VB = E_KB + A_QK
E_GB = E_VB + A_V
E_GK = E_GB + A_V
E_QBS = E_GK + LANE
E_KBS = E_QBS + A_QK
E_COLS = E_KBS + A_QK


def _log_sigmoid(x):
    return jnp.minimum(x, 0.0) - jnp.log1p(jnp.exp(-jnp.abs(x)))


def _even_proj_body(*refs, n_x, n_ctx_tiles):
    x = _read_x(refs[:n_x], n_ctx_tiles)
    (g_ref, sh_ref, sc_ref, w_ref, wg_ref, bg_ref, dec_ref, cc_ref, ss_ref,
     q_ref, k_ref, lf_ref, lb_ref, gt_ref, v_ref) = refs[n_x:]
    tm = x.shape[0]
    h = (_rms(x) * g_ref[...] * (1.0 + sc_ref[...]) + sh_ref[...]).astype(BF16)

    def proj(start, width):
        return _dot(h, w_ref[:, start:start + width])

    cc = cc_ref[...]
    ss = ss_ref[...]
    scale = DK ** -0.5
    q_ref[:, 0:A_QK] = proj(E_QA, A_QK) * scale
    k_ref[:, 0:A_QK] = proj(E_KA, A_QK)
    qb, qbs, kb, kbs = proj(E_QB, A_QK), proj(E_QBS, A_QK), proj(E_KB, A_QK), proj(E_KBS, A_QK)
    for j in range(A_QK // LANE):
        sl = slice(j * LANE, (j + 1) * LANE)
        o = A_QK + j * LANE
        q_ref[:, o:o + LANE] = qb[:, sl] * cc + qbs[:, sl] * ss
        k_ref[:, o:o + LANE] = (kb[:, sl] * cc + kbs[:, sl] * ss) * scale
    v_ref[:, 0:A_V] = proj(E_VA, A_V).astype(BF16)
    v_ref[:, A_V:2 * A_V] = proj(E_VB, A_V).astype(BF16)
    gt_ref[:, 0:A_V] = proj(E_GA, A_V)
    gt_ref[:, A_V:2 * A_V] = proj(E_GB, A_V)
    gk = proj(E_GK, LANE).astype(BF16)
    la = _log_sigmoid(_dot(gk, wg_ref[...]) + bg_ref[...]) * (1.0 / GATE_NORM)
    lf_ref[:, 0:A_QK] = la[:, 0:A_QK]
    lb_ref[:, 0:A_QK] = la[:, A_QK:2 * A_QK]
    log_g = -jnp.exp(dec_ref[...])
    lf_ref[:, A_QK:2 * A_QK] = jnp.broadcast_to(log_g[0:1, :], (tm, A_QK))
    lb_ref[:, A_QK:2 * A_QK] = jnp.broadcast_to(log_g[1:2, :], (tm, A_QK))


def _even_proj_call(tok, x, mods, layer, g_pre, w, wg, bg, dec, cc, ss):
    tm = tok.tile(512)
    d = tok.d
    row = lambda i: (i, 0)
    outs = [
        jax.ShapeDtypeStruct((tok.n, 2 * A_QK), F32),
        jax.ShapeDtypeStruct((tok.n, 2 * A_QK), F32),
        jax.ShapeDtypeStruct((tok.n, 2 * A_QK), F32),
        jax.ShapeDtypeStruct((tok.n, 2 * A_QK), F32),
        jax.ShapeDtypeStruct((tok.n, 2 * A_V), F32),
        jax.ShapeDtypeStruct((tok.n, 2 * A_V), BF16),
    ]
    x_specs, x_args = tok.x_specs(x, tm)
    return pl.pallas_call(
        functools.partial(_even_proj_body, n_x=len(x_args), n_ctx_tiles=tok.n_ctx // tm),
        grid=(tok.n // tm,),
        in_specs=x_specs + [
            _full((1, d)),
            tok.mod_spec(layer, 0, tm),
            tok.mod_spec(layer, 1, tm),
            _full(w.shape), _full(wg.shape), _full(bg.shape), _full(dec.shape),
            tok.rope_spec(tm),
            tok.rope_spec(tm),
        ],
        out_specs=[pl.BlockSpec((tm, o.shape[1]), row) for o in outs],
        out_shape=outs,
        compiler_params=_cparams(("arbitrary",)),
        name="even_proj",
    )(*x_args, g_pre, mods, mods, w, wg, bg, dec, cc, ss)


SCAN_GROUP = 8


def _chunk_cumsum(x):
    row = lax.broadcasted_iota(jnp.int32, x.shape, 0) % CHUNK
    s = 1
    while s < CHUNK:
        x = x + jnp.where(row >= s, pltpu.roll(x, s, axis=0), 0.0)
        s *= 2
    return x


def _scan_body(q_ref, k_ref, lf_ref, lb_ref, v_ref, gt_ref, gain_ref, *rest, seq_len, group, context):
    if context:
        m_ref, sfin_ref, st_ref, o_acc = rest
    else:
        s0_ref, m_ref, st_ref, o_acc = rest
    C = CHUNK
    blk = group * C
    nblk = seq_len // blk
    pair_w = 2 * DK
    head0 = lax.broadcasted_iota(jnp.int32, (blk, pair_w), 1) < DK
    t_in = lax.broadcasted_iota(jnp.int32, (C, pair_w), 0)
    j_in = lax.broadcasted_iota(jnp.int32, (C, pair_w), 1) % DK
    keep_fwd = t_in >= j_in
    keep_bwd = t_in <= j_in
    on_diag = ((lax.broadcasted_iota(jnp.int32, (2 * DV, pair_w), 0) < DV)
               == (lax.broadcasted_iota(jnp.int32, (2 * DV, pair_w), 1) < DK))
    zeros_v = jnp.zeros((C, DV), BF16)

    if context:
        st_ref[...] = jnp.zeros(st_ref.shape, F32)
    else:
        st_ref[...] = s0_ref[...]

    def one_direction(r0, d, log_ref, keep, reverse):
        g = log_ref[pl.ds(r0, blk), :]
        b = _chunk_cumsum(g)
        tots = [b[C * j + C - 1:C * j + C, :] for j in range(group)]
        totb = jnp.concatenate([jnp.broadcast_to(t, (C, pair_w)) for t in tots], axis=0)
        if reverse:
            b = totb - b + g
        q = q_ref[pl.ds(r0, blk), :]
        k = k_ref[pl.ds(r0, blk), :]
        vblk = v_ref[pl.ds(r0, blk), :]
        q_dec = (q * jnp.exp(b)).astype(BF16)
        k_inv = k * jnp.exp(-b)
        k_up = (k * jnp.exp(totb - b)).astype(BF16)
        k_inv0 = jnp.where(head0, k_inv, 0.0).astype(BF16)
        k_inv1 = jnp.where(head0, 0.0, k_inv).astype(BF16)
        st = st_ref[d]
        outs = [None] * group
        for j in (reversed(range(group)) if reverse else range(group)):
            sl = slice(C * j, C * (j + 1))
            k_bd = jnp.concatenate([k_inv0[sl], k_inv1[sl]], axis=0)
            a = jnp.where(keep, _dot_nt(q_dec[sl], k_bd), 0.0).astype(BF16)
            vc = vblk[sl]
            v_bd = jnp.concatenate([jnp.concatenate([vc[:, :DV], zeros_v], axis=1),
                                    jnp.concatenate([zeros_v, vc[:, DV:]], axis=1)], axis=0)
            outs[j] = _dot_nt(q_dec[sl], st.astype(BF16)) + _dot(a, v_bd)
            st = st * jnp.exp(tots[j]) + jnp.where(on_diag, _dot_tn(vc, k_up[sl]), 0.0)
        st_ref[d] = st
        return jnp.concatenate(outs, axis=0)

    def body(c, carry, accumulate):
        rf = pl.multiple_of(c * blk, blk)
        rb = pl.multiple_of((nblk - 1 - c) * blk, blk)
        o_f = one_direction(rf, 0, lf_ref, keep_fwd, False)
        o_b = one_direction(rb, 1, lb_ref, keep_bwd, True)
        if accumulate:
            o_acc[pl.ds(rf, blk), :] += o_f
            o_acc[pl.ds(rb, blk), :] += o_b
        else:
            o_acc[pl.ds(rf, blk), :] = o_f
            o_acc[pl.ds(rb, blk), :] = o_b
        return carry

    lax.fori_loop(0, nblk // 2, functools.partial(body, accumulate=False), 0)
    lax.fori_loop(nblk // 2, nblk, functools.partial(body, accumulate=True), 0)

    if context:
        sfin_ref[...] = st_ref[...]

    tr = min(seq_len, 256)

    def finish(i, carry):
        r0 = pl.multiple_of(i * tr, tr)
        o = o_acc[pl.ds(r0, tr), :]
        gate = _silu(gt_ref[pl.ds(r0, tr), :])
        gain = gain_ref[...]
        for hh in range(2):
            sl = slice(hh * DV, (hh + 1) * DV)
            m_ref[pl.ds(r0, tr), sl] = (_rms(o[:, sl]) * gain[:, sl] * gate[:, sl]).astype(BF16)
        return carry

    lax.fori_loop(0, seq_len // tr, finish, 0)


def _scan_call(tok, q, k, lf, lb, v, gt, gain, s0=None):
    context = s0 is None
    if context:
        nb, seq_len, blk0 = tok.nb_ctx, tok.l_ctx, 0
    else:
        nb, seq_len, blk0 = tok.nb_lat, tok.l_lat, tok.n_ctx // tok.l_lat
    pairs = q.shape[1] // (2 * DK)
    group = min(SCAN_GROUP, seq_len // CHUNK // 2)
    seq = lambda b, p: (blk0 + b, p)
    st_spec = pl.BlockSpec((None, 2, None, 2 * DV, 2 * DK), lambda b, p: (b, 0, p, 0, 0))
    in_specs = [
        pl.BlockSpec((seq_len, 2 * DK), seq),
        pl.BlockSpec((seq_len, 2 * DK), seq),
        pl.BlockSpec((seq_len, 2 * DK), seq),
        pl.BlockSpec((seq_len, 2 * DK), seq),
        pl.BlockSpec((seq_len, 2 * DV), seq),
        pl.BlockSpec((seq_len, 2 * DV), seq),
        pl.BlockSpec((1, 2 * DV), lambda b, p: (0, p)),
    ]
    args = [q, k, lf, lb, v, gt, gain]
    out_shape = [jax.ShapeDtypeStruct(v.shape, v.dtype)]
    out_specs = [pl.BlockSpec((seq_len, 2 * DV), seq)]
    if context:
        out_shape.append(jax.ShapeDtypeStruct((nb, 2, pairs, 2 * DV, 2 * DK), F32))
        out_specs.append(st_spec)
    else:
        in_specs.append(st_spec)
        args.append(s0)
    return pl.pallas_call(
        functools.partial(_scan_body, seq_len=seq_len, group=group, context=context),
        grid=(nb, pairs),
        in_specs=in_specs,
        out_specs=out_specs,
        out_shape=out_shape,
        input_output_aliases={4: 0},
        scratch_shapes=[pltpu.VMEM((2, 2 * DV, 2 * DK), F32), pltpu.VMEM((seq_len, 2 * DV), F32)],
        compiler_params=_cparams(("arbitrary", "arbitrary")),
        name="scan_ctx" if context else "scan_lat",
    )(*args)


def _mix_mlp_body(*refs, n_x, n_out, n_ctx_tiles):
    x = _read_x(refs[:n_x], n_ctx_tiles)
    (m_ref, wo_ref, gmix_ref, gate1_ref, gpre_ref, sh_ref, sc_ref, w1_ref, w2_ref,
     gpost_ref, gate2_ref) = refs[n_x:len(refs) - n_out]
    out_refs = refs[len(refs) - n_out:]
    y = _dot(m_ref[...], wo_ref[...])
    x1 = x + gate1_ref[...] * (_rms(y) * gmix_ref[...])
    h = (_rms(x1) * gpre_ref[...] * (1.0 + sc_ref[...]) + sh_ref[...]).astype(BF16)
    u = jnp.maximum(_dot(h, w1_ref[...]), 0.0)
    z = _dot((u * u).astype(BF16), w2_ref[...])
    res = x1 + gate2_ref[...] * (_rms(z) * gpost_ref[...])
    if n_out == 1:
        out_refs[0][...] = res
    else:
        is_ctx = pl.program_id(0) < n_ctx_tiles

        @pl.when(is_ctx)
        def _():
            out_refs[0][...] = res

        @pl.when(jnp.logical_not(is_ctx))
        def _():
            out_refs[1][...] = res


def _mix_mlp_call(tok, m, x, mods, layer, w_out, w_out_idx, g_mix, g_pre, w1, w2, g_post, split_out=False):
    tm = tok.tile(512)
    d = tok.d
    row = lambda i: (i, 0)
    resident = lambda w, idx: pl.BlockSpec((None,) + w.shape[1:], lambda i: (idx, 0, 0),
                                           pipeline_mode=pl.Buffered(1))
    x_specs, x_args = tok.x_specs(x, tm)
    nct = tok.n_ctx // tm
    if split_out:
        out_specs = [pl.BlockSpec((tm, d), lambda i: (jnp.minimum(i, nct - 1), 0)),
                     pl.BlockSpec((tm, d), lambda i: (jnp.maximum(i - nct, 0), 0))]
        out_shape = [jax.ShapeDtypeStruct((tok.n_ctx, d), F32), jax.ShapeDtypeStruct((tok.n_lat, d), F32)]
    else:
        out_specs = [pl.BlockSpec((tm, d), row)]
        out_shape = [jax.ShapeDtypeStruct((tok.n, d), F32)]
    return pl.pallas_call(
        functools.partial(_mix_mlp_body, n_x=len(x_args), n_out=len(out_shape), n_ctx_tiles=nct),
        grid=(tok.n // tm,),
        in_specs=x_specs + [
            pl.BlockSpec((tm, m.shape[1]), row),
            resident(w_out, w_out_idx),
            _full((1, d)),
            tok.mod_spec(layer, 2, tm),
            _full((1, d)),
            tok.mod_spec(layer, 3, tm),
            tok.mod_spec(layer, 4, tm),
            resident(w1, layer),
            resident(w2, layer),
            _full((1, d)),
            tok.mod_spec(layer, 5, tm),
        ],
        out_specs=out_specs,
        out_shape=out_shape,
        compiler_params=_cparams(("arbitrary",)),
        name="mix_mlp",
    )(*x_args, m, w_out, g_mix, mods, g_pre, mods, mods, w1, w2, g_post, mods)


HEAD_W = 2 * LANE
O_QLAT, O_CKV, O_KPE, O_KPES, O_COLS = 0, Q_LORA, Q_LORA + KV_LORA, Q_LORA + KV_LORA + LANE, Q_LORA + KV_LORA + 2 * LANE
QB_NOPE, QB_ROPE, QB_SWAP = 0, H_C * LANE, 2 * H_C * LANE


def _expand_kv(cb, kper, wkvb_ref, k_ref, v_ref):
    for hp in range(H_C // 2):
        nope2 = _dot(cb, wkvb_ref[:, hp * 2 * LANE:(hp + 1) * 2 * LANE])
        for j in range(2):
            hh = 2 * hp + j
            k_ref[:, hh * HEAD_W:hh * HEAD_W + LANE] = nope2[:, j * LANE:(j + 1) * LANE].astype(BF16)
            k_ref[:, hh * HEAD_W + LANE:(hh + 1) * HEAD_W] = kper
    v_ref[...] = _dot(cb, wkvb_ref[:, H_C * LANE:2 * H_C * LANE]).astype(BF16)


def _mla_proj_body(x_ref, g_ref, sh_ref, sc_ref, win_ref, qn_ref, wqb_ref, kvn_ref, wkvb_ref, cc_ref, ss_ref,
                   q_ref, k_ref, v_ref, ckv_ref, kpe_ref):
    h = (_rms(x_ref[...]) * g_ref[...] * (1.0 + sc_ref[...]) + sh_ref[...]).astype(BF16)
    cc = cc_ref[...]
    ss = ss_ref[...]
    qn = (_rms(_dot(h, win_ref[:, O_QLAT:O_QLAT + Q_LORA])) * qn_ref[...]).astype(BF16)
    ckvn = _rms(_dot(h, win_ref[:, O_CKV:O_CKV + KV_LORA])) * kvn_ref[...]
    kpe2 = _dot(h, win_ref[:, O_KPE:O_KPE + 2 * LANE])
    kpe = kpe2[:, 0:LANE]
    kper = (kpe * cc + kpe2[:, LANE:2 * LANE] * ss).astype(BF16)
    ckv_ref[...] = ckvn
    kpe_ref[...] = kpe
    for hp in range(H_C // 2):
        o = hp * 2 * LANE
        nope2 = _dot(qn, wqb_ref[:, QB_NOPE + o:QB_NOPE + o + 2 * LANE])
        rope2 = _dot(qn, wqb_ref[:, QB_ROPE + o:QB_ROPE + o + 2 * LANE])
        swap2 = _dot(qn, wqb_ref[:, QB_SWAP + o:QB_SWAP + o + 2 * LANE])
        for j in range(2):
            hh = 2 * hp + j
            sl = slice(j * LANE, (j + 1) * LANE)
            rot = rope2[:, sl] * cc + swap2[:, sl] * ss
            q_ref[:, hh * HEAD_W:hh * HEAD_W + LANE] = (nope2[:, sl] * ATTN_Q_SCALE).astype(BF16)
            q_ref[:, hh * HEAD_W + LANE:(hh + 1) * HEAD_W] = (rot * ATTN_Q_SCALE).astype(BF16)
    _expand_kv(ckvn.astype(BF16), kper, wkvb_ref, k_ref, v_ref)


def _mla_proj_call(tok, x, mods, layer, g_pre, win, qn, wqb, kvn, wkvb, cc, ss):
    tm = tok.tile(512)
    d = tok.d
    row = lambda i: (i, 0)
    outs = [
        jax.ShapeDtypeStruct((tok.n, H_C * HEAD_W), BF16),
        jax.ShapeDtypeStruct((tok.n, H_C * HEAD_W), BF16),
        jax.ShapeDtypeStruct((tok.n, H_C * V_HEAD_C), BF16),
        jax.ShapeDtypeStruct((tok.n, KV_LORA), F32),
        jax.ShapeDtypeStruct((tok.n, LANE), F32),
    ]
    return pl.pallas_call(
        _mla_proj_body,
        grid=(tok.n // tm,),
        in_specs=[
            pl.BlockSpec((tm, d), row),
            _full((1, d)),
            tok.mod_spec(layer, 0, tm),
            tok.mod_spec(layer, 1, tm),
            _full(win.shape), _full(qn.shape), _full(wqb.shape), _full(kvn.shape), _full(wkvb.shape),
            tok.rope_spec(tm),
            tok.rope_spec(tm),
        ],
        out_specs=[pl.BlockSpec((tm, o.shape[1]), row) for o in outs],
        out_shape=outs,
        compiler_params=_cparams(("arbitrary",)),
        name="mla_proj",
    )(x, g_pre, mods, mods, win, qn, wqb, kvn, wkvb, cc, ss)


def _cache_expand_body(ckv_ref, kpe_ref, wkvb_ref, k_ref, v_ref):
    _expand_kv(ckv_ref[...].astype(BF16), kpe_ref[...].astype(BF16), wkvb_ref, k_ref, v_ref)


def _cache_expand_call(ckv, kpe_pad, wkvb):
    n = ckv.shape[0]
    tm = 512
    while n % tm:
        tm //= 2
    row = lambda i: (i, 0)
    outs = [jax.ShapeDtypeStruct((n, H_C * HEAD_W), BF16), jax.ShapeDtypeStruct((n, H_C * V_HEAD_C), BF16)]
    return pl.pallas_call(
        _cache_expand_body,
        grid=(n // tm,),
        in_specs=[pl.BlockSpec((tm, KV_LORA), row), pl.BlockSpec((tm, LANE), row), _full(wkvb.shape)],
        out_specs=[pl.BlockSpec((tm, o.shape[1]), row) for o in outs],
        out_shape=outs,
        compiler_params=_cparams(("arbitrary",)),
        name="cache_expand",
    )(ckv, kpe_pad, wkvb)


ATTN_TQ = 512
ATTN_TK = 512
ATTN_HEADS_PER_STEP = 2
ATTN_Q_SCALE = (QK_NOPE + QK_ROPE) ** -0.5 * float(np.log2(np.e))


def _softmax_pv(s, values):
    p = jnp.exp2(s - jnp.max(s, axis=-1, keepdims=True))
    den = jnp.sum(p, axis=-1, keepdims=True)
    p = p.astype(BF16)
    acc = functools.reduce(jnp.add, [_dot(p[:, k0:k0 + vb.shape[0]], vb) for vb, k0 in values])
    return (acc / den).astype(BF16)


def _attn_ctx_body(q_ref, k_ref, v_ref, o_ref):
    for hh in range(H_C):
        s = _dot_nt(q_ref[:, hh * HEAD_W:(hh + 1) * HEAD_W], k_ref[:, hh * HEAD_W:(hh + 1) * HEAD_W])
        o_ref[:, hh * V_HEAD_C:(hh + 1) * V_HEAD_C] = _softmax_pv(s, [(v_ref[:, hh * V_HEAD_C:(hh + 1) * V_HEAD_C], 0)])


def _lane_groups(x):
    return [x[:, g:g + LANE] for g in range(0, x.shape[1], LANE)]


def _attn_lat_body(q_ref, kc_ref, vc_ref, k_ref, v_ref, o_ref, sa_ref, sb_ref, ma_ref, mb_ref, vx_ref):
    past, n_self = kc_ref.shape[0], k_ref.shape[0]
    heads = vx_ref.shape[0]
    tq = sa_ref.shape[0]
    n = q_ref.shape[0] // tq
    tk = min(ATTN_TK, n_self)
    blocks = [(kc_ref, 0, past, 0)] + [(k_ref, k0, tk, past + k0) for k0 in range(0, n_self, tk)]

    for h in range(heads):
        vx_ref[h, 0:past, 0:V_HEAD_C] = vc_ref[:, h * V_HEAD_C:(h + 1) * V_HEAD_C]
        vx_ref[h, past:past + n_self, 0:V_HEAD_C] = v_ref[:, h * V_HEAD_C:(h + 1) * V_HEAD_C]
        vx_ref[h, :, V_HEAD_C:2 * V_HEAD_C] = jnp.ones((past + n_self, V_HEAD_C), BF16)

    def scores(h, i, s_ref, m_ref):
        q = q_ref[pl.ds(pl.multiple_of(i * tq, tq), tq), h * HEAD_W:(h + 1) * HEAD_W]
        mx = None
        for kk_ref, k0, size, col in blocks:
            s = _dot_nt(q, kk_ref[k0:k0 + size, h * HEAD_W:(h + 1) * HEAD_W])
            s_ref[:, col:col + size] = s
            mx = functools.reduce(jnp.maximum, _lane_groups(s) + ([] if mx is None else [mx]))
        m_ref[...] = mx

    def finish(h, i, s_ref, m_ref):
        m = jnp.max(m_ref[...], axis=-1, keepdims=True)
        acc = None
        for _, _, size, col in blocks:
            p = jnp.exp2(s_ref[:, col:col + size] - m).astype(BF16)
            pv = _dot(p, vx_ref[h, col:col + size, :])
            acc = pv if acc is None else acc + pv
        out = acc[:, 0:V_HEAD_C] / acc[:, V_HEAD_C:2 * V_HEAD_C]
        o_ref[pl.ds(pl.multiple_of(i * tq, tq), tq), h * V_HEAD_C:(h + 1) * V_HEAD_C] = out.astype(BF16)

    scores(0, 0, sa_ref, ma_ref)
    for h in range(heads):

        def two_tiles(j, carry, h=h):
            i = 2 * j
            scores(h, i + 1, sb_ref, mb_ref)
            finish(h, i, sa_ref, ma_ref)
            scores(h, i + 2, sa_ref, ma_ref)
            finish(h, i + 1, sb_ref, mb_ref)
            return carry

        lax.fori_loop(0, n // 2 - 1, two_tiles, 0)
        scores(h, n - 1, sb_ref, mb_ref)
        finish(h, n - 2, sa_ref, ma_ref)
        if h + 1 < heads:
            scores(h + 1, 0, sa_ref, ma_ref)
        finish(h, n - 1, sb_ref, mb_ref)


def _attn_ctx_call(tok, q, k, v):
    seq = lambda b: (b, 0)
    return pl.pallas_call(
        _attn_ctx_body,
        grid=(tok.nb_ctx,),
        in_specs=[pl.BlockSpec((tok.l_ctx, H_C * HEAD_W), seq),
                  pl.BlockSpec((tok.l_ctx, H_C * HEAD_W), seq),
                  pl.BlockSpec((tok.l_ctx, H_C * V_HEAD_C), seq)],
        out_specs=pl.BlockSpec((tok.l_ctx, H_C * V_HEAD_C), seq),
        out_shape=jax.ShapeDtypeStruct(v.shape, v.dtype),
        input_output_aliases={2: 0},
        compiler_params=_cparams(("arbitrary",)),
        name="attn_ctx",
    )(q, k, v)


def _attn_lat_call(tok, q, k, v, k_cache, v_cache):
    blk0 = tok.n_ctx // tok.l_lat
    past = k_cache.shape[0] // tok.nb_lat
    tq = min(ATTN_TQ, tok.l_lat // 2)
    hg = ATTN_HEADS_PER_STEP
    seq = lambda b, hh: (blk0 + b, hh)
    cache = lambda b, hh: (b, hh)
    return pl.pallas_call(
        _attn_lat_body,
        grid=(tok.nb_lat, H_C // hg),
        in_specs=[pl.BlockSpec((tok.l_lat, hg * HEAD_W), seq),
                  pl.BlockSpec((past, hg * HEAD_W), cache),
                  pl.BlockSpec((past, hg * V_HEAD_C), cache),
                  pl.BlockSpec((tok.l_lat, hg * HEAD_W), seq),
                  pl.BlockSpec((tok.l_lat, hg * V_HEAD_C), seq)],
        out_specs=pl.BlockSpec((tok.l_lat, hg * V_HEAD_C), seq),
        out_shape=jax.ShapeDtypeStruct(v.shape, v.dtype),
        input_output_aliases={4: 0},
        scratch_shapes=([pltpu.VMEM((tq, past + tok.l_lat), F32)] * 2 + [pltpu.VMEM((tq, LANE), F32)] * 2
                        + [pltpu.VMEM((hg, past + tok.l_lat, 2 * V_HEAD_C), BF16)]),
        compiler_params=_cparams(("arbitrary", "arbitrary")),
        name="attn_lat",
    )(q, k_cache, v_cache, k, v)


def _swap_halves(w, head_dim):
    n = w.shape[-1]
    idx = np.arange(n)
    idx = (idx // head_dim) * head_dim + (idx % head_dim + head_dim // 2) % head_dim
    return w[..., idx]


def _pad_cols(w, width):
    return jnp.pad(w, ((0, 0), (0, width - w.shape[1])))


def _even_weights(w_in, w_gk2, b_gk2):
    sizes = (A_QK, A_QK, A_V, A_V, 2 * GATE_RANK, A_QK, A_QK, A_V, A_V)
    qa, ka, va, ga, gk, qb, kb, vb, gb = jnp.split(w_in, np.cumsum(sizes)[:-1].tolist(), axis=1)
    w = jnp.concatenate([qa, ka, va, ga, qb, kb, vb, gb, _pad_cols(gk, LANE),
                         _swap_halves(qb, DK), _swap_halves(kb, DK)], axis=1).astype(BF16)
    wg = jnp.zeros((LANE, 2 * A_QK), F32)
    wg = wg.at[0:GATE_RANK, 0:A_QK].set(w_gk2[0]).at[GATE_RANK:2 * GATE_RANK, A_QK:2 * A_QK].set(w_gk2[1])
    bg = b_gk2.reshape(1, 2 * A_QK)
    return w, wg.astype(BF16), bg


def _odd_weights(w_in, w_q_b, w_kv_b):
    q_lat, ckv, kpe = w_in[:, :Q_LORA], w_in[:, Q_LORA:Q_LORA + KV_LORA], w_in[:, Q_LORA + KV_LORA:]
    win = jnp.concatenate([q_lat, ckv, _pad_cols(kpe, LANE), _pad_cols(_swap_halves(kpe, QK_ROPE), LANE)],
                          axis=1).astype(BF16)
    wq = w_q_b.reshape(Q_LORA, H_C, QK_NOPE + QK_ROPE)
    nope = wq[:, :, :QK_NOPE].reshape(Q_LORA, H_C * QK_NOPE)
    rope = wq[:, :, QK_NOPE:]
    pad = lambda r: jnp.pad(r, ((0, 0), (0, 0), (0, LANE - QK_ROPE))).reshape(Q_LORA, H_C * LANE)
    wqb = jnp.concatenate([nope, pad(rope), pad(_swap_halves(rope, QK_ROPE))], axis=1).astype(BF16)
    wkv = w_kv_b.reshape(KV_LORA, H_C, QK_NOPE + V_HEAD_C)
    wkvb = jnp.concatenate([wkv[:, :, :QK_NOPE].reshape(KV_LORA, H_C * QK_NOPE),
                            wkv[:, :, QK_NOPE:].reshape(KV_LORA, H_C * V_HEAD_C)], axis=1).astype(BF16)
    return win, wqb, wkvb


def _rope_tables(tok, tm):
    rows = tok.l_lat // GRID_W
    row = jnp.repeat(jnp.arange(rows), GRID_W).astype(F32)
    col = jnp.tile(jnp.arange(GRID_W), rows).astype(F32)
    n_freq = QK_ROPE // 4
    inv = ROPE_BASE ** (-jnp.arange(n_freq, dtype=F32) / n_freq)
    ang = jnp.concatenate([row[:, None] * inv, col[:, None] * inv], axis=-1)
    cos, sin = jnp.cos(ang), jnp.sin(ang)
    cc = jnp.tile(jnp.concatenate([cos, cos], axis=-1), (1, LANE // QK_ROPE))
    ss = jnp.tile(jnp.concatenate([-sin, sin], axis=-1), (1, LANE // QK_ROPE))
    cc = jnp.concatenate([jnp.ones((tm, LANE), F32), cc], axis=0)
    ss = jnp.concatenate([jnp.zeros((tm, LANE), F32), ss], axis=0)
    return cc, ss


def _states_to_kernel(s):
    nb, _, heads = s.shape[:3]
    st = jnp.swapaxes(s.reshape(nb, 2, heads // 2, 2, DK, DV), -1, -2)
    z = jnp.zeros_like(st[:, :, :, 0])
    rows = [jnp.concatenate([st[:, :, :, 0], z], axis=-1), jnp.concatenate([z, st[:, :, :, 1]], axis=-1)]
    return jnp.concatenate(rows, axis=-2)


def _states_from_kernel(st):
    nb, _, pairs = st.shape[:3]
    heads = jnp.stack([st[:, :, :, :DV, :DK], st[:, :, :, DV:, DK:]], axis=3)
    return jnp.swapaxes(heads, -1, -2).reshape(nb, 2, 2 * pairs, DK, DV)


def kernel(x_prompt, x_sample, cache_ckv, cache_kpe, state_gla, state_ret, c, c_ctx, w_ada, b_ada, norm_mix_pre, norm_mix_post, norm_mlp_pre, norm_mlp_post, w_in_even, w_gk2, b_gk2, gla_norm, ret_decay, w_out_even, w_in_odd, q_a_norm, w_q_b, kv_a_norm, w_kv_b, w_out_odd, w_mlp1, w_mlp2):
    nb_ctx, l_ctx, d = x_prompt.shape
    nb_lat, l_lat, _ = x_sample.shape
    depth = w_ada.shape[0]
    tok = _Tokens(nb_ctx, l_ctx, nb_lat, l_lat, d)
    assert nb_lat < MOD_ROWS and tok.n_ctx % l_lat == 0 and l_ctx % CHUNK == 0 and l_lat % (2 * CHUNK) == 0

    cond = jnp.concatenate([c, c_ctx[None, :], jnp.zeros((MOD_ROWS - nb_lat - 1, d), F32)], axis=0)
    mods = _ada_call(cond, w_ada, b_ada).reshape(depth, MOD_ROWS, 6, 1, d)
    cc, ss = _rope_tables(tok, tok.tile(512))
    x = (x_prompt.reshape(tok.n_ctx, d), x_sample.reshape(tok.n_lat, d))
    vec = lambda a: a.reshape(1, -1)
    w_out_even16, w_out_odd16 = w_out_even.astype(BF16), w_out_odd.astype(BF16)
    w_mlp1_16, w_mlp2_16 = w_mlp1.astype(BF16), w_mlp2.astype(BF16)

    new_ckv, new_kpe, new_gla, new_ret = [], [], [], []
    for l in range(depth):
        i = l // 2
        if l % 2 == 0:
            w, wg, bg = _even_weights(w_in_even[i], w_gk2[i], b_gk2[i])
            dec = jnp.repeat(ret_decay[i], DK, axis=-1)
            q, k, lf, lb, gt, v = _even_proj_call(tok, x, mods, l, vec(norm_mix_pre[l]), w, wg, bg, dec, cc, ss)
            gain = jnp.concatenate([jnp.tile(gla_norm[i], N_HEAD_SCAN), jnp.ones((A_V,), F32)]).reshape(1, 2 * A_V)
            s0_lat = _states_to_kernel(jnp.concatenate([state_gla[:, i], state_ret[:, i]], axis=2))
            m_ctx, s_fin = _scan_call(tok, q, k, lf, lb, v, gt, gain)
            (m,) = _scan_call(tok, q, k, lf, lb, m_ctx, gt, gain, s0=s0_lat)
            s_fin = _states_from_kernel(s_fin)
            new_gla.append(s_fin[:, :, :N_HEAD_SCAN])
            new_ret.append(s_fin[:, :, N_HEAD_SCAN:])
            w_out = w_out_even16
        else:
            win, wqb, wkvb = _odd_weights(w_in_odd[i], w_q_b[i], w_kv_b[i])
            q, k, v, ckv, kpe = _mla_proj_call(tok, x, mods, l, vec(norm_mix_pre[l]), win, vec(q_a_norm[i]), wqb,
                                               vec(kv_a_norm[i]), wkvb, cc, ss)
            past = cache_ckv.shape[2]
            kpe_pad = jnp.pad(cache_kpe[:, i].reshape(nb_lat * past, QK_ROPE), ((0, 0), (0, LANE - QK_ROPE)))
            k_c, v_c = _cache_expand_call(cache_ckv[:, i].reshape(nb_lat * past, KV_LORA), kpe_pad, wkvb)
            m = _attn_lat_call(tok, q, k, _attn_ctx_call(tok, q, k, v), k_c, v_c)
            new_ckv.append(ckv[:tok.n_ctx].reshape(nb_ctx, l_ctx, KV_LORA))
            new_kpe.append(kpe[:tok.n_ctx, :QK_ROPE].reshape(nb_ctx, l_ctx, QK_ROPE))
            w_out = w_out_odd16
        x = _mix_mlp_call(tok, m, x, mods, l, w_out, i, vec(norm_mix_post[l]), vec(norm_mlp_pre[l]), w_mlp1_16,
                          w_mlp2_16, vec(norm_mlp_post[l]), split_out=(l == depth - 1))
        x = x[0] if len(x) == 1 else tuple(x)

    return (x[0].reshape(nb_ctx, l_ctx, d), x[1].reshape(nb_lat, l_lat, d),
            jnp.stack(new_ckv, axis=1), jnp.stack(new_kpe, axis=1),
            jnp.stack(new_gla, axis=1), jnp.stack(new_ret, axis=1))
```

```python
import functools

import numpy as np
import jax
import jax.numpy as jnp
from jax import lax
from jax.experimental import pallas as pl
from jax.experimental.pallas import tpu as pltpu

F32 = jnp.float32
BF16 = jnp.bfloat16

EPS = 1e-6
ROPE_BASE = 10000.0
GRID_W = 64
CHUNK = 64
GATE_RANK = 16
GATE_NORM = 16.0
N_HEAD_SCAN = 4
DK = 64
DV = 128
H_C = 8
Q_LORA = 256
KV_LORA = 256
QK_NOPE = 128
QK_ROPE = 64
V_HEAD_C = 128
LANE = 128
MOD_ROWS = 16
EVEN_TILE = 512
MLA_TILE = 1024

VMEM_LIMIT = 56 * 1024 * 1024


def _cparams(sem):
    return pltpu.CompilerParams(dimension_semantics=sem, vmem_limit_bytes=VMEM_LIMIT)


def _dot(a, b):
    return jnp.dot(a, b, preferred_element_type=F32)


def _dot_nt(a, b):
    return lax.dot_general(a, b, (((1,), (1,)), ((), ())), preferred_element_type=F32)


def _dot_tn(a, b):
    return lax.dot_general(a, b, (((0,), (0,)), ((), ())), preferred_element_type=F32)


def _rms(x):
    return x * lax.rsqrt(jnp.mean(x * x, axis=-1, keepdims=True) + EPS)


def _silu(x):
    return x * jax.nn.sigmoid(x)


def _full(shape):
    n = len(shape)
    return pl.BlockSpec(shape, lambda *_: (0,) * n, pipeline_mode=pl.Buffered(1))


def _ada_body(cond_ref, w_ref, b_ref, o_ref):
    s = _silu(cond_ref[...]).astype(BF16)
    o_ref[...] = _dot(s, w_ref[...].astype(BF16)) + b_ref[...]


def _ada_call(cond, w_ada, b_ada):
    depth, d, n = w_ada.shape
    tn = 1536
    return pl.pallas_call(
        _ada_body,
        grid=(depth, n // tn),
        in_specs=[
            pl.BlockSpec((MOD_ROWS, d), lambda l, j: (0, 0)),
            pl.BlockSpec((None, d, tn), lambda l, j: (l, 0, j)),
            pl.BlockSpec((None, 1, tn), lambda l, j: (l, 0, j)),
        ],
        out_specs=pl.BlockSpec((None, MOD_ROWS, tn), lambda l, j: (l, 0, j)),
        out_shape=jax.ShapeDtypeStruct((depth, MOD_ROWS, n), F32),
        compiler_params=_cparams(("arbitrary", "arbitrary")),
        name="ada_mod",
    )(cond, w_ada, b_ada.reshape(depth, 1, n))


class _Tokens:
    def __init__(self, nb_ctx, l_ctx, nb_lat, l_lat, d):
        self.nb_ctx, self.l_ctx, self.nb_lat, self.l_lat, self.d = nb_ctx, l_ctx, nb_lat, l_lat, d
        self.n_ctx = nb_ctx * l_ctx
        self.n_lat = nb_lat * l_lat
        self.n = self.n_ctx + self.n_lat
        self.ctx_row = nb_lat

    def tile(self, want):
        t = want
        while self.n_ctx % t or self.l_lat % t:
            t //= 2
        return t

    def mod_spec(self, layer, chunk, tm):
        n_ctx, l_lat, ctx_row = self.n_ctx, self.l_lat, self.ctx_row

        def idx(i, *_):
            start = i * tm
            row = jnp.where(start < n_ctx, ctx_row, (start - n_ctx) // l_lat)
            return (layer, row, chunk, 0, 0)

        return pl.BlockSpec((None, None, None, 1, self.d), idx)

    def x_specs(self, x, tm):
        if not isinstance(x, tuple):
            return [pl.BlockSpec((tm, self.d), lambda i: (i, 0))], [x]
        nct = self.n_ctx // tm
        return [pl.BlockSpec((tm, self.d), lambda i: (jnp.minimum(i, nct - 1), 0)),
                pl.BlockSpec((tm, self.d), lambda i: (jnp.maximum(i - nct, 0), 0))], list(x)

    def rope_spec(self, tm):
        n_ctx, l_lat = self.n_ctx, self.l_lat

        def idx(i):
            start = i * tm
            return (jnp.where(start < n_ctx, 0, 1 + ((start - n_ctx) % l_lat) // tm), 0)

        return pl.BlockSpec((tm, LANE), idx)


def _read_x(x_refs, n_ctx_tiles):
    if len(x_refs) == 1:
        return x_refs[0][...]
    return jnp.where(pl.program_id(0) < n_ctx_tiles, x_refs[0][...], x_refs[1][...])


A_QK = N_HEAD_SCAN * DK
A_V = N_HEAD_SCAN * DV
E_QA, E_KA, E_VA, E_GA = 0, A_QK, 2 * A_QK, 2 * A_QK + A_V
E_QB = E_GA + A_V
E_KB = E_QB + A_QK
E_VB = E_KB + A_QK
E_GB = E_VB + A_V
E_GK = E_GB + A_V
E_QBS = E_GK + LANE
E_KBS = E_QBS + A_QK
E_COLS = E_KBS + A_QK


def _log_sigmoid(x):
    return jnp.minimum(x, 0.0) - jnp.log1p(jnp.exp(-jnp.abs(x)))


def _even_proj_body(*refs, n_x, n_ctx_tiles):
    x = _read_x(refs[:n_x], n_ctx_tiles)
    (g_ref, sh_ref, sc_ref, w_ref, wg_ref, bg_ref, dec_ref, cc_ref, ss_ref,
     q_ref, k_ref, lf_ref, lb_ref, gt_ref, v_ref) = refs[n_x:]
    tm = x.shape[0]
    h = (_rms(x) * g_ref[...] * (1.0 + sc_ref[...]) + sh_ref[...]).astype(BF16)

    def proj(start, width):
        return _dot(h, w_ref[:, start:start + width])

    cc = cc_ref[...]
    ss = ss_ref[...]
    scale = DK ** -0.5
    q_ref[:, 0:A_QK] = proj(E_QA, A_QK) * scale
    k_ref[:, 0:A_QK] = proj(E_KA, A_QK)
    qb, qbs, kb, kbs = proj(E_QB, A_QK), proj(E_QBS, A_QK), proj(E_KB, A_QK), proj(E_KBS, A_QK)
    for j in range(A_QK // LANE):
        sl = slice(j * LANE, (j + 1) * LANE)
        o = A_QK + j * LANE
        q_ref[:, o:o + LANE] = qb[:, sl] * cc + qbs[:, sl] * ss
        k_ref[:, o:o + LANE] = (kb[:, sl] * cc + kbs[:, sl] * ss) * scale
    v_ref[:, 0:A_V] = proj(E_VA, A_V).astype(BF16)
    v_ref[:, A_V:2 * A_V] = proj(E_VB, A_V).astype(BF16)
    gt_ref[:, 0:A_V] = proj(E_GA, A_V)
    gt_ref[:, A_V:2 * A_V] = proj(E_GB, A_V)
    gk = proj(E_GK, LANE).astype(BF16)
    la = _log_sigmoid(_dot(gk, wg_ref[...]) + bg_ref[...]) * (1.0 / GATE_NORM)
    lf_ref[:, 0:A_QK] = la[:, 0:A_QK]
    lb_ref[:, 0:A_QK] = la[:, A_QK:2 * A_QK]
    log_g = -jnp.exp(dec_ref[...])
    lf_ref[:, A_QK:2 * A_QK] = jnp.broadcast_to(log_g[0:1, :], (tm, A_QK))
    lb_ref[:, A_QK:2 * A_QK] = jnp.broadcast_to(log_g[1:2, :], (tm, A_QK))


def _even_proj_call(tok, x, mods, layer, g_pre, w, wg, bg, dec, cc, ss):
    tm = tok.tile(EVEN_TILE)
    d = tok.d
    row = lambda i: (i, 0)
    outs = [
        jax.ShapeDtypeStruct((tok.n, 2 * A_QK), F32),
        jax.ShapeDtypeStruct((tok.n, 2 * A_QK), F32),
        jax.ShapeDtypeStruct((tok.n, 2 * A_QK), F32),
        jax.ShapeDtypeStruct((tok.n, 2 * A_QK), F32),
        jax.ShapeDtypeStruct((tok.n, 2 * A_V), F32),
        jax.ShapeDtypeStruct((tok.n, 2 * A_V), BF16),
    ]
    x_specs, x_args = tok.x_specs(x, tm)
    return pl.pallas_call(
        functools.partial(_even_proj_body, n_x=len(x_args), n_ctx_tiles=tok.n_ctx // tm),
        grid=(tok.n // tm,),
        in_specs=x_specs + [
            _full((1, d)),
            tok.mod_spec(layer, 0, tm),
            tok.mod_spec(layer, 1, tm),
            _full(w.shape), _full(wg.shape), _full(bg.shape), _full(dec.shape),
            tok.rope_spec(tm),
            tok.rope_spec(tm),
        ],
        out_specs=[pl.BlockSpec((tm, o.shape[1]), row) for o in outs],
        out_shape=outs,
        compiler_params=_cparams(("arbitrary",)),
        name="even_proj",
    )(*x_args, g_pre, mods, mods, w, wg, bg, dec, cc, ss)


SCAN_GROUP = 8


def _chunk_cumsum(x):
    row = lax.broadcasted_iota(jnp.int32, x.shape, 0) % CHUNK
    s = 1
    while s < CHUNK:
        x = x + jnp.where(row >= s, pltpu.roll(x, s, axis=0), 0.0)
        s *= 2
    return x


def _scan_body(q_ref, k_ref, lf_ref, lb_ref, v_ref, gt_ref, gain_ref, *rest, seq_len, group, context):
    if context:
        m_ref, sfin_ref, st_ref, o_acc = rest
    else:
        s0_ref, m_ref, st_ref, o_acc = rest
    C = CHUNK
    blk = group * C
    nblk = seq_len // blk
    pair_w = 2 * DK
    head0 = lax.broadcasted_iota(jnp.int32, (blk, pair_w), 1) < DK
    t_in = lax.broadcasted_iota(jnp.int32, (C, pair_w), 0)
    j_in = lax.broadcasted_iota(jnp.int32, (C, pair_w), 1) % DK
    keep_fwd = t_in >= j_in
    keep_bwd = t_in <= j_in
    on_diag = ((lax.broadcasted_iota(jnp.int32, (2 * DV, pair_w), 0) < DV)
               == (lax.broadcasted_iota(jnp.int32, (2 * DV, pair_w), 1) < DK))
    zeros_v = jnp.zeros((C, DV), BF16)

    if context:
        st_ref[...] = jnp.zeros(st_ref.shape, F32)
    else:
        st_ref[...] = s0_ref[...]

    def one_direction(r0, d, log_ref, keep, reverse):
        g = log_ref[pl.ds(r0, blk), :]
        b = _chunk_cumsum(g)
        tots = [b[C * j + C - 1:C * j + C, :] for j in range(group)]
        totb = jnp.concatenate([jnp.broadcast_to(t, (C, pair_w)) for t in tots], axis=0)
        if reverse:
            b = totb - b + g
        q = q_ref[pl.ds(r0, blk), :]
        k = k_ref[pl.ds(r0, blk), :]
        vblk = v_ref[pl.ds(r0, blk), :]
        q_dec = (q * jnp.exp(b)).astype(BF16)
        k_inv = k * jnp.exp(-b)
        k_up = (k * jnp.exp(totb - b)).astype(BF16)
        k_inv0 = jnp.where(head0, k_inv, 0.0).astype(BF16)
        k_inv1 = jnp.where(head0, 0.0, k_inv).astype(BF16)
        st = st_ref[d]
        outs = [None] * group
        for j in (reversed(range(group)) if reverse else range(group)):
            sl = slice(C * j, C * (j + 1))
            k_bd = jnp.concatenate([k_inv0[sl], k_inv1[sl]], axis=0)
            a = jnp.where(keep, _dot_nt(q_dec[sl], k_bd), 0.0).astype(BF16)
            vc = vblk[sl]
            v_bd = jnp.concatenate([jnp.concatenate([vc[:, :DV], zeros_v], axis=1),
                                    jnp.concatenate([zeros_v, vc[:, DV:]], axis=1)], axis=0)
            outs[j] = _dot_nt(q_dec[sl], st.astype(BF16)) + _dot(a, v_bd)
            st = st * jnp.exp(tots[j]) + jnp.where(on_diag, _dot_tn(vc, k_up[sl]), 0.0)
        st_ref[d] = st
        return jnp.concatenate(outs, axis=0)

    def emit(r0, o):
        gate = _silu(gt_ref[pl.ds(r0, blk), :])
        gain = gain_ref[...]
        for hh in range(2):
            sl = slice(hh * DV, (hh + 1) * DV)
            m_ref[pl.ds(r0, blk), sl] = (_rms(o[:, sl]) * gain[:, sl] * gate[:, sl]).astype(BF16)

    def body(c, carry, second_half):
        rf = pl.multiple_of(c * blk, blk)
        rb = pl.multiple_of((nblk - 1 - c) * blk, blk)
        o_f = one_direction(rf, 0, lf_ref, keep_fwd, False)
        o_b = one_direction(rb, 1, lb_ref, keep_bwd, True)
        if second_half:
            emit(rf, o_acc[pl.ds(rf, blk), :] + o_f)
            emit(rb, o_acc[pl.ds(rb, blk), :] + o_b)
        else:
            o_acc[pl.ds(rf, blk), :] = o_f
            o_acc[pl.ds(rb, blk), :] = o_b
        return carry

    lax.fori_loop(0, nblk // 2, functools.partial(body, second_half=False), 0)
    lax.fori_loop(nblk // 2, nblk, functools.partial(body, second_half=True), 0)

    if context:
        sfin_ref[...] = st_ref[...]


def _scan_call(tok, q, k, lf, lb, v, gt, gain, s0=None):
    context = s0 is None
    if context:
        nb, seq_len, blk0 = tok.nb_ctx, tok.l_ctx, 0
    else:
        nb, seq_len, blk0 = tok.nb_lat, tok.l_lat, tok.n_ctx // tok.l_lat
    pairs = q.shape[1] // (2 * DK)
    group = min(SCAN_GROUP, seq_len // CHUNK // 2)
    seq = lambda b, p: (blk0 + b, p)
    st_spec = pl.BlockSpec((None, 2, None, 2 * DV, 2 * DK), lambda b, p: (b, 0, p, 0, 0))
    in_specs = [
        pl.BlockSpec((seq_len, 2 * DK), seq),
        pl.BlockSpec((seq_len, 2 * DK), seq),
        pl.BlockSpec((seq_len, 2 * DK), seq),
        pl.BlockSpec((seq_len, 2 * DK), seq),
        pl.BlockSpec((seq_len, 2 * DV), seq),
        pl.BlockSpec((seq_len, 2 * DV), seq),
        pl.BlockSpec((1, 2 * DV), lambda b, p: (0, p)),
    ]
    args = [q, k, lf, lb, v, gt, gain]
    out_shape = [jax.ShapeDtypeStruct(v.shape, v.dtype)]
    out_specs = [pl.BlockSpec((seq_len, 2 * DV), seq)]
    if context:
        out_shape.append(jax.ShapeDtypeStruct((nb, 2, pairs, 2 * DV, 2 * DK), F32))
        out_specs.append(st_spec)
    else:
        in_specs.append(st_spec)
        args.append(s0)
    return pl.pallas_call(
        functools.partial(_scan_body, seq_len=seq_len, group=group, context=context),
        grid=(nb, pairs),
        in_specs=in_specs,
        out_specs=out_specs,
        out_shape=out_shape,
        input_output_aliases={4: 0},
        scratch_shapes=[pltpu.VMEM((2, 2 * DV, 2 * DK), F32), pltpu.VMEM((seq_len, 2 * DV), F32)],
        compiler_params=_cparams(("arbitrary", "arbitrary")),
        name="scan_ctx" if context else "scan_lat",
    )(*args)


def _mix_mlp_body(*refs, n_x, n_out, n_ctx_tiles):
    x = _read_x(refs[:n_x], n_ctx_tiles)
    (m_ref, wo_ref, gmix_ref, gate1_ref, gpre_ref, sh_ref, sc_ref, w1_ref, w2_ref,
     gpost_ref, gate2_ref) = refs[n_x:len(refs) - n_out]
    out_refs = refs[len(refs) - n_out:]
    y = _dot(m_ref[...], wo_ref[...])
    x1 = x + gate1_ref[...] * (_rms(y) * gmix_ref[...])
    h = (_rms(x1) * gpre_ref[...] * (1.0 + sc_ref[...]) + sh_ref[...]).astype(BF16)
    u = jnp.maximum(_dot(h, w1_ref[...]), 0.0)
    z = _dot((u * u).astype(BF16), w2_ref[...])
    res = x1 + gate2_ref[...] * (_rms(z) * gpost_ref[...])
    if n_out == 1:
        out_refs[0][...] = res
    else:
        is_ctx = pl.program_id(0) < n_ctx_tiles

        @pl.when(is_ctx)
        def _():
            out_refs[0][...] = res

        @pl.when(jnp.logical_not(is_ctx))
        def _():
            out_refs[1][...] = res


def _mix_mlp_call(tok, m, x, mods, layer, w_out, w_out_idx, g_mix, g_pre, w1, w2, g_post, split_out=False):
    tm = tok.tile(512)
    d = tok.d
    row = lambda i: (i, 0)
    resident = lambda w, idx: pl.BlockSpec((None,) + w.shape[1:], lambda i: (idx, 0, 0),
                                           pipeline_mode=pl.Buffered(1))
    x_specs, x_args = tok.x_specs(x, tm)
    nct = tok.n_ctx // tm
    if split_out:
        out_specs = [pl.BlockSpec((tm, d), lambda i: (jnp.minimum(i, nct - 1), 0)),
                     pl.BlockSpec((tm, d), lambda i: (jnp.maximum(i - nct, 0), 0))]
        out_shape = [jax.ShapeDtypeStruct((tok.n_ctx, d), F32), jax.ShapeDtypeStruct((tok.n_lat, d), F32)]
    else:
        out_specs = [pl.BlockSpec((tm, d), row)]
        out_shape = [jax.ShapeDtypeStruct((tok.n, d), F32)]
    return pl.pallas_call(
        functools.partial(_mix_mlp_body, n_x=len(x_args), n_out=len(out_shape), n_ctx_tiles=nct),
        grid=(tok.n // tm,),
        in_specs=x_specs + [
            pl.BlockSpec((tm, m.shape[1]), row),
            resident(w_out, w_out_idx),
            _full((1, d)),
            tok.mod_spec(layer, 2, tm),
            _full((1, d)),
            tok.mod_spec(layer, 3, tm),
            tok.mod_spec(layer, 4, tm),
            resident(w1, layer),
            resident(w2, layer),
            _full((1, d)),
            tok.mod_spec(layer, 5, tm),
        ],
        out_specs=out_specs,
        out_shape=out_shape,
        compiler_params=_cparams(("arbitrary",)),
        name="mix_mlp",
    )(*x_args, m, w_out, g_mix, mods, g_pre, mods, mods, w1, w2, g_post, mods)


HEAD_W = 2 * LANE
O_QLAT, O_CKV, O_KPE, O_KPES, O_COLS = 0, Q_LORA, Q_LORA + KV_LORA, Q_LORA + KV_LORA + LANE, Q_LORA + KV_LORA + 2 * LANE
QB_NOPE, QB_ROPE, QB_SWAP = 0, H_C * LANE, 2 * H_C * LANE


def _expand_kv(cb, kper, wkvb_ref, k_ref, v_ref):
    for hp in range(H_C // 2):
        nope2 = _dot(cb, wkvb_ref[:, hp * 2 * LANE:(hp + 1) * 2 * LANE])
        for j in range(2):
            hh = 2 * hp + j
            k_ref[:, hh * HEAD_W:hh * HEAD_W + LANE] = nope2[:, j * LANE:(j + 1) * LANE].astype(BF16)
            k_ref[:, hh * HEAD_W + LANE:(hh + 1) * HEAD_W] = kper
    v_ref[...] = _dot(cb, wkvb_ref[:, H_C * LANE:2 * H_C * LANE]).astype(BF16)


def _mla_proj_body(x_ref, g_ref, sh_ref, sc_ref, win_ref, qn_ref, wqb_ref, kvn_ref, wkvb_ref, cc_ref, ss_ref,
                   q_ref, k_ref, v_ref, ckv_ref, kpe_ref):
    h = (_rms(x_ref[...]) * g_ref[...] * (1.0 + sc_ref[...]) + sh_ref[...]).astype(BF16)
    cc = cc_ref[...]
    ss = ss_ref[...]
    qn = (_rms(_dot(h, win_ref[:, O_QLAT:O_QLAT + Q_LORA])) * qn_ref[...]).astype(BF16)
    ckvn = _rms(_dot(h, win_ref[:, O_CKV:O_CKV + KV_LORA])) * kvn_ref[...]
    kpe2 = _dot(h, win_ref[:, O_KPE:O_KPE + 2 * LANE])
    kpe = kpe2[:, 0:LANE]
    kper = (kpe * cc + kpe2[:, LANE:2 * LANE] * ss).astype(BF16)
    ckv_ref[...] = ckvn
    kpe_ref[...] = kpe
    for hp in range(H_C // 2):
        o = hp * 2 * LANE
        nope2 = _dot(qn, wqb_ref[:, QB_NOPE + o:QB_NOPE + o + 2 * LANE])
        rope2 = _dot(qn, wqb_ref[:, QB_ROPE + o:QB_ROPE + o + 2 * LANE])
        swap2 = _dot(qn, wqb_ref[:, QB_SWAP + o:QB_SWAP + o + 2 * LANE])
        for j in range(2):
            hh = 2 * hp + j
            sl = slice(j * LANE, (j + 1) * LANE)
            rot = rope2[:, sl] * cc + swap2[:, sl] * ss
            q_ref[:, hh * HEAD_W:hh * HEAD_W + LANE] = (nope2[:, sl] * ATTN_Q_SCALE).astype(BF16)
            q_ref[:, hh * HEAD_W + LANE:(hh + 1) * HEAD_W] = (rot * ATTN_Q_SCALE).astype(BF16)
    _expand_kv(ckvn.astype(BF16), kper, wkvb_ref, k_ref, v_ref)


def _mla_proj_call(tok, x, mods, layer, g_pre, win, qn, wqb, kvn, wkvb, cc, ss):
    tm = tok.tile(MLA_TILE)
    d = tok.d
    row = lambda i: (i, 0)
    outs = [
        jax.ShapeDtypeStruct((tok.n, H_C * HEAD_W), BF16),
        jax.ShapeDtypeStruct((tok.n, H_C * HEAD_W), BF16),
        jax.ShapeDtypeStruct((tok.n, H_C * V_HEAD_C), BF16),
        jax.ShapeDtypeStruct((tok.n, KV_LORA), F32),
        jax.ShapeDtypeStruct((tok.n, LANE), F32),
    ]
    return pl.pallas_call(
        _mla_proj_body,
        grid=(tok.n // tm,),
        in_specs=[
            pl.BlockSpec((tm, d), row),
            _full((1, d)),
            tok.mod_spec(layer, 0, tm),
            tok.mod_spec(layer, 1, tm),
            _full(win.shape), _full(qn.shape), _full(wqb.shape), _full(kvn.shape), _full(wkvb.shape),
            tok.rope_spec(tm),
            tok.rope_spec(tm),
        ],
        out_specs=[pl.BlockSpec((tm, o.shape[1]), row) for o in outs],
        out_shape=outs,
        compiler_params=_cparams(("arbitrary",)),
        name="mla_proj",
    )(x, g_pre, mods, mods, win, qn, wqb, kvn, wkvb, cc, ss)


def _cache_expand_body(ckv_ref, kpe_ref, wkvb_ref, k_ref, v_ref):
    _expand_kv(ckv_ref[...].astype(BF16), kpe_ref[...].astype(BF16), wkvb_ref, k_ref, v_ref)


def _cache_expand_call(ckv, kpe_pad, wkvb):
    n = ckv.shape[0]
    tm = 512
    while n % tm:
        tm //= 2
    row = lambda i: (i, 0)
    outs = [jax.ShapeDtypeStruct((n, H_C * HEAD_W), BF16), jax.ShapeDtypeStruct((n, H_C * V_HEAD_C), BF16)]
    return pl.pallas_call(
        _cache_expand_body,
        grid=(n // tm,),
        in_specs=[pl.BlockSpec((tm, KV_LORA), row), pl.BlockSpec((tm, LANE), row), _full(wkvb.shape)],
        out_specs=[pl.BlockSpec((tm, o.shape[1]), row) for o in outs],
        out_shape=outs,
        compiler_params=_cparams(("arbitrary",)),
        name="cache_expand",
    )(ckv, kpe_pad, wkvb)


ATTN_TQ = 512
ATTN_TK = 512
ATTN_HEADS_PER_STEP = 2
ATTN_Q_SCALE = (QK_NOPE + QK_ROPE) ** -0.5 * float(np.log2(np.e))


def _softmax_pv(s, values):
    p = jnp.exp2(s - jnp.max(s, axis=-1, keepdims=True))
    den = jnp.sum(p, axis=-1, keepdims=True)
    p = p.astype(BF16)
    acc = functools.reduce(jnp.add, [_dot(p[:, k0:k0 + vb.shape[0]], vb) for vb, k0 in values])
    return (acc / den).astype(BF16)


def _attn_ctx_body(q_ref, k_ref, v_ref, o_ref):
    for hh in range(H_C):
        s = _dot_nt(q_ref[:, hh * HEAD_W:(hh + 1) * HEAD_W], k_ref[:, hh * HEAD_W:(hh + 1) * HEAD_W])
        o_ref[:, hh * V_HEAD_C:(hh + 1) * V_HEAD_C] = _softmax_pv(s, [(v_ref[:, hh * V_HEAD_C:(hh + 1) * V_HEAD_C], 0)])


def _lane_groups(x):
    return [x[:, g:g + LANE] for g in range(0, x.shape[1], LANE)]


def _attn_lat_body(q_ref, kc_ref, vc_ref, k_ref, v_ref, o_ref, sa_ref, sb_ref, ma_ref, mb_ref, vx_ref):
    past, n_self = kc_ref.shape[0], k_ref.shape[0]
    heads = vx_ref.shape[0]
    tq = sa_ref.shape[0]
    n = q_ref.shape[0] // tq
    tk = min(ATTN_TK, n_self)
    blocks = [(kc_ref, 0, past, 0)] + [(k_ref, k0, tk, past + k0) for k0 in range(0, n_self, tk)]

    for h in range(heads):
        vx_ref[h, 0:past, 0:V_HEAD_C] = vc_ref[:, h * V_HEAD_C:(h + 1) * V_HEAD_C]
        vx_ref[h, past:past + n_self, 0:V_HEAD_C] = v_ref[:, h * V_HEAD_C:(h + 1) * V_HEAD_C]
        vx_ref[h, :, V_HEAD_C:2 * V_HEAD_C] = jnp.ones((past + n_self, V_HEAD_C), BF16)

    def scores(h, i, s_ref, m_ref):
        q = q_ref[pl.ds(pl.multiple_of(i * tq, tq), tq), h * HEAD_W:(h + 1) * HEAD_W]
        mx = None
        for kk_ref, k0, size, col in blocks:
            s = _dot_nt(q, kk_ref[k0:k0 + size, h * HEAD_W:(h + 1) * HEAD_W])
            s_ref[:, col:col + size] = s
            mx = functools.reduce(jnp.maximum, _lane_groups(s) + ([] if mx is None else [mx]))
        m_ref[...] = mx

    def finish(h, i, s_ref, m_ref):
        m = jnp.max(m_ref[...], axis=-1, keepdims=True)
        acc = None
        for _, _, size, col in blocks:
            p = jnp.exp2(s_ref[:, col:col + size] - m).astype(BF16)
            pv = _dot(p, vx_ref[h, col:col + size, :])
            acc = pv if acc is None else acc + pv
        out = acc[:, 0:V_HEAD_C] / acc[:, V_HEAD_C:2 * V_HEAD_C]
        o_ref[pl.ds(pl.multiple_of(i * tq, tq), tq), h * V_HEAD_C:(h + 1) * V_HEAD_C] = out.astype(BF16)

    scores(0, 0, sa_ref, ma_ref)
    for h in range(heads):

        def two_tiles(j, carry, h=h):
            i = 2 * j
            scores(h, i + 1, sb_ref, mb_ref)
            finish(h, i, sa_ref, ma_ref)
            scores(h, i + 2, sa_ref, ma_ref)
            finish(h, i + 1, sb_ref, mb_ref)
            return carry

        lax.fori_loop(0, n // 2 - 1, two_tiles, 0)
        scores(h, n - 1, sb_ref, mb_ref)
        finish(h, n - 2, sa_ref, ma_ref)
        if h + 1 < heads:
            scores(h + 1, 0, sa_ref, ma_ref)
        finish(h, n - 1, sb_ref, mb_ref)


def _attn_ctx_call(tok, q, k, v):
    seq = lambda b: (b, 0)
    return pl.pallas_call(
        _attn_ctx_body,
        grid=(tok.nb_ctx,),
        in_specs=[pl.BlockSpec((tok.l_ctx, H_C * HEAD_W), seq),
                  pl.BlockSpec((tok.l_ctx, H_C * HEAD_W), seq),
                  pl.BlockSpec((tok.l_ctx, H_C * V_HEAD_C), seq)],
        out_specs=pl.BlockSpec((tok.l_ctx, H_C * V_HEAD_C), seq),
        out_shape=jax.ShapeDtypeStruct(v.shape, v.dtype),
        input_output_aliases={2: 0},
        compiler_params=_cparams(("arbitrary",)),
        name="attn_ctx",
    )(q, k, v)


def _attn_lat_call(tok, q, k, v, k_cache, v_cache):
    blk0 = tok.n_ctx // tok.l_lat
    past = k_cache.shape[0] // tok.nb_lat
    tq = min(ATTN_TQ, tok.l_lat // 2)
    hg = ATTN_HEADS_PER_STEP
    seq = lambda b, hh: (blk0 + b, hh)
    cache = lambda b, hh: (b, hh)
    return pl.pallas_call(
        _attn_lat_body,
        grid=(tok.nb_lat, H_C // hg),
        in_specs=[pl.BlockSpec((tok.l_lat, hg * HEAD_W), seq),
                  pl.BlockSpec((past, hg * HEAD_W), cache),
                  pl.BlockSpec((past, hg * V_HEAD_C), cache),
                  pl.BlockSpec((tok.l_lat, hg * HEAD_W), seq),
                  pl.BlockSpec((tok.l_lat, hg * V_HEAD_C), seq)],
        out_specs=pl.BlockSpec((tok.l_lat, hg * V_HEAD_C), seq),
        out_shape=jax.ShapeDtypeStruct(v.shape, v.dtype),
        input_output_aliases={4: 0},
        scratch_shapes=([pltpu.VMEM((tq, past + tok.l_lat), F32)] * 2 + [pltpu.VMEM((tq, LANE), F32)] * 2
                        + [pltpu.VMEM((hg, past + tok.l_lat, 2 * V_HEAD_C), BF16)]),
        compiler_params=_cparams(("arbitrary", "arbitrary")),
        name="attn_lat",
    )(q, k_cache, v_cache, k, v)


def _swap_halves(w, head_dim):
    n = w.shape[-1]
    idx = np.arange(n)
    idx = (idx // head_dim) * head_dim + (idx % head_dim + head_dim // 2) % head_dim
    return w[..., idx]


def _pad_cols(w, width):
    return jnp.pad(w, ((0, 0), (0, width - w.shape[1])))


def _even_weights(w_in, w_gk2, b_gk2):
    sizes = (A_QK, A_QK, A_V, A_V, 2 * GATE_RANK, A_QK, A_QK, A_V, A_V)
    qa, ka, va, ga, gk, qb, kb, vb, gb = jnp.split(w_in, np.cumsum(sizes)[:-1].tolist(), axis=1)
    w = jnp.concatenate([qa, ka, va, ga, qb, kb, vb, gb, _pad_cols(gk, LANE),
                         _swap_halves(qb, DK), _swap_halves(kb, DK)], axis=1).astype(BF16)
    wg = jnp.zeros((LANE, 2 * A_QK), F32)
    wg = wg.at[0:GATE_RANK, 0:A_QK].set(w_gk2[0]).at[GATE_RANK:2 * GATE_RANK, A_QK:2 * A_QK].set(w_gk2[1])
    bg = b_gk2.reshape(1, 2 * A_QK)
    return w, wg.astype(BF16), bg


def _odd_weights(w_in, w_q_b, w_kv_b):
    q_lat, ckv, kpe = w_in[:, :Q_LORA], w_in[:, Q_LORA:Q_LORA + KV_LORA], w_in[:, Q_LORA + KV_LORA:]
    win = jnp.concatenate([q_lat, ckv, _pad_cols(kpe, LANE), _pad_cols(_swap_halves(kpe, QK_ROPE), LANE)],
                          axis=1).astype(BF16)
    wq = w_q_b.reshape(Q_LORA, H_C, QK_NOPE + QK_ROPE)
    nope = wq[:, :, :QK_NOPE].reshape(Q_LORA, H_C * QK_NOPE)
    rope = wq[:, :, QK_NOPE:]
    pad = lambda r: jnp.pad(r, ((0, 0), (0, 0), (0, LANE - QK_ROPE))).reshape(Q_LORA, H_C * LANE)
    wqb = jnp.concatenate([nope, pad(rope), pad(_swap_halves(rope, QK_ROPE))], axis=1).astype(BF16)
    wkv = w_kv_b.reshape(KV_LORA, H_C, QK_NOPE + V_HEAD_C)
    wkvb = jnp.concatenate([wkv[:, :, :QK_NOPE].reshape(KV_LORA, H_C * QK_NOPE),
                            wkv[:, :, QK_NOPE:].reshape(KV_LORA, H_C * V_HEAD_C)], axis=1).astype(BF16)
    return win, wqb, wkvb


def _rope_tables(tok, tm):
    rows = tok.l_lat // GRID_W
    row = jnp.repeat(jnp.arange(rows), GRID_W).astype(F32)
    col = jnp.tile(jnp.arange(GRID_W), rows).astype(F32)
    n_freq = QK_ROPE // 4
    inv = ROPE_BASE ** (-jnp.arange(n_freq, dtype=F32) / n_freq)
    ang = jnp.concatenate([row[:, None] * inv, col[:, None] * inv], axis=-1)
    cos, sin = jnp.cos(ang), jnp.sin(ang)
    cc = jnp.tile(jnp.concatenate([cos, cos], axis=-1), (1, LANE // QK_ROPE))
    ss = jnp.tile(jnp.concatenate([-sin, sin], axis=-1), (1, LANE // QK_ROPE))
    cc = jnp.concatenate([jnp.ones((tm, LANE), F32), cc], axis=0)
    ss = jnp.concatenate([jnp.zeros((tm, LANE), F32), ss], axis=0)
    return cc, ss


def _states_to_kernel(s):
    nb, _, heads = s.shape[:3]
    st = jnp.swapaxes(s.reshape(nb, 2, heads // 2, 2, DK, DV), -1, -2)
    z = jnp.zeros_like(st[:, :, :, 0])
    rows = [jnp.concatenate([st[:, :, :, 0], z], axis=-1), jnp.concatenate([z, st[:, :, :, 1]], axis=-1)]
    return jnp.concatenate(rows, axis=-2)


def _states_from_kernel(st):
    nb, _, pairs = st.shape[:3]
    heads = jnp.stack([st[:, :, :, :DV, :DK], st[:, :, :, DV:, DK:]], axis=3)
    return jnp.swapaxes(heads, -1, -2).reshape(nb, 2, 2 * pairs, DK, DV)


def kernel(x_prompt, x_sample, cache_ckv, cache_kpe, state_gla, state_ret, c, c_ctx, w_ada, b_ada, norm_mix_pre, norm_mix_post, norm_mlp_pre, norm_mlp_post, w_in_even, w_gk2, b_gk2, gla_norm, ret_decay, w_out_even, w_in_odd, q_a_norm, w_q_b, kv_a_norm, w_kv_b, w_out_odd, w_mlp1, w_mlp2):
    nb_ctx, l_ctx, d = x_prompt.shape
    nb_lat, l_lat, _ = x_sample.shape
    depth = w_ada.shape[0]
    tok = _Tokens(nb_ctx, l_ctx, nb_lat, l_lat, d)
    assert nb_lat < MOD_ROWS and tok.n_ctx % l_lat == 0 and l_ctx % CHUNK == 0 and l_lat % (2 * CHUNK) == 0

    cond = jnp.concatenate([c, c_ctx[None, :], jnp.zeros((MOD_ROWS - nb_lat - 1, d), F32)], axis=0)
    mods = _ada_call(cond, w_ada, b_ada).reshape(depth, MOD_ROWS, 6, 1, d)
    rope_even = _rope_tables(tok, tok.tile(EVEN_TILE))
    rope_odd = _rope_tables(tok, tok.tile(MLA_TILE))
    x = (x_prompt.reshape(tok.n_ctx, d), x_sample.reshape(tok.n_lat, d))
    vec = lambda a: a.reshape(1, -1)
    w_out_even16, w_out_odd16 = w_out_even.astype(BF16), w_out_odd.astype(BF16)
    w_mlp1_16, w_mlp2_16 = w_mlp1.astype(BF16), w_mlp2.astype(BF16)

    new_ckv, new_kpe, new_gla, new_ret = [], [], [], []
    for l in range(depth):
        i = l // 2
        if l % 2 == 0:
            w, wg, bg = _even_weights(w_in_even[i], w_gk2[i], b_gk2[i])
            dec = jnp.repeat(ret_decay[i], DK, axis=-1)
            q, k, lf, lb, gt, v = _even_proj_call(tok, x, mods, l, vec(norm_mix_pre[l]), w, wg, bg, dec, *rope_even)
            gain = jnp.concatenate([jnp.tile(gla_norm[i], N_HEAD_SCAN), jnp.ones((A_V,), F32)]).reshape(1, 2 * A_V)
            s0_lat = _states_to_kernel(jnp.concatenate([state_gla[:, i], state_ret[:, i]], axis=2))
            m_ctx, s_fin = _scan_call(tok, q, k, lf, lb, v, gt, gain)
            (m,) = _scan_call(tok, q, k, lf, lb, m_ctx, gt, gain, s0=s0_lat)
            s_fin = _states_from_kernel(s_fin)
            new_gla.append(s_fin[:, :, :N_HEAD_SCAN])
            new_ret.append(s_fin[:, :, N_HEAD_SCAN:])
            w_out = w_out_even16
        else:
            win, wqb, wkvb = _odd_weights(w_in_odd[i], w_q_b[i], w_kv_b[i])
            q, k, v, ckv, kpe = _mla_proj_call(tok, x, mods, l, vec(norm_mix_pre[l]), win, vec(q_a_norm[i]), wqb,
                                               vec(kv_a_norm[i]), wkvb, *rope_odd)
            past = cache_ckv.shape[2]
            kpe_pad = jnp.pad(cache_kpe[:, i].reshape(nb_lat * past, QK_ROPE), ((0, 0), (0, LANE - QK_ROPE)))
            k_c, v_c = _cache_expand_call(cache_ckv[:, i].reshape(nb_lat * past, KV_LORA), kpe_pad, wkvb)
            m = _attn_lat_call(tok, q, k, _attn_ctx_call(tok, q, k, v), k_c, v_c)
            new_ckv.append(ckv[:tok.n_ctx].reshape(nb_ctx, l_ctx, KV_LORA))
            new_kpe.append(kpe[:tok.n_ctx, :QK_ROPE].reshape(nb_ctx, l_ctx, QK_ROPE))
            w_out = w_out_odd16
        x = _mix_mlp_call(tok, m, x, mods, l, w_out, i, vec(norm_mix_post[l]), vec(norm_mlp_pre[l]), w_mlp1_16,
                          w_mlp2_16, vec(norm_mlp_post[l]), split_out=(l == depth - 1))
        x = x[0] if len(x) == 1 else tuple(x)

    return (x[0].reshape(nb_ctx, l_ctx, d), x[1].reshape(nb_lat, l_lat, d),
            jnp.stack(new_ckv, axis=1), jnp.stack(new_kpe, axis=1),
            jnp.stack(new_gla, axis=1), jnp.stack(new_ret, axis=1))
```

```python
import functools

import numpy as np
import jax
import jax.numpy as jnp
from jax import lax
from jax.experimental import pallas as pl
from jax.experimental.pallas import tpu as pltpu

F32 = jnp.float32
BF16 = jnp.bfloat16

EPS = 1e-6
ROPE_BASE = 10000.0
GRID_W = 64
CHUNK = 64
GATE_RANK = 16
GATE_NORM = 16.0
N_HEAD_SCAN = 4
DK = 64
DV = 128
H_C = 8
Q_LORA = 256
KV_LORA = 256
QK_NOPE = 128
QK_ROPE = 64
V_HEAD_C = 128
LANE = 128
MOD_ROWS = 16
EVEN_TILE = 512
MLA_TILE = 1024

VMEM_LIMIT = 56 * 1024 * 1024


def _cparams(sem):
    return pltpu.CompilerParams(dimension_semantics=sem, vmem_limit_bytes=VMEM_LIMIT)


def _dot(a, b):
    return jnp.dot(a, b, preferred_element_type=F32)


def _dot_nt(a, b):
    return lax.dot_general(a, b, (((1,), (1,)), ((), ())), preferred_element_type=F32)


def _dot_tn(a, b):
    return lax.dot_general(a, b, (((0,), (0,)), ((), ())), preferred_element_type=F32)


def _rms(x):
    return x * lax.rsqrt(jnp.mean(x * x, axis=-1, keepdims=True) + EPS)


def _silu(x):
    return x * jax.nn.sigmoid(x)


def _full(shape):
    n = len(shape)
    return pl.BlockSpec(shape, lambda *_: (0,) * n, pipeline_mode=pl.Buffered(1))


def _layer_block(w, idx):
    tail = (0,) * (w.ndim - 1)
    return pl.BlockSpec((None,) + w.shape[1:], lambda *_: (idx,) + tail, pipeline_mode=pl.Buffered(1))


def _ada_body(cond_ref, w_ref, b_ref, o_ref):
    s = _silu(cond_ref[...]).astype(BF16)
    o_ref[...] = _dot(s, w_ref[...].astype(BF16)) + b_ref[...]


def _ada_call(cond, w_ada, b_ada):
    depth, d, n = w_ada.shape
    tn = 1536
    return pl.pallas_call(
        _ada_body,
        grid=(depth, n // tn),
        in_specs=[
            pl.BlockSpec((MOD_ROWS, d), lambda l, j: (0, 0)),
            pl.BlockSpec((None, d, tn), lambda l, j: (l, 0, j)),
            pl.BlockSpec((None, 1, tn), lambda l, j: (l, 0, j)),
        ],
        out_specs=pl.BlockSpec((None, MOD_ROWS, tn), lambda l, j: (l, 0, j)),
        out_shape=jax.ShapeDtypeStruct((depth, MOD_ROWS, n), F32),
        compiler_params=_cparams(("arbitrary", "arbitrary")),
        name="ada_mod",
    )(cond, w_ada, b_ada.reshape(depth, 1, n))


class _Tokens:
    def __init__(self, nb_ctx, l_ctx, nb_lat, l_lat, d):
        self.nb_ctx, self.l_ctx, self.nb_lat, self.l_lat, self.d = nb_ctx, l_ctx, nb_lat, l_lat, d
        self.n_ctx = nb_ctx * l_ctx
        self.n_lat = nb_lat * l_lat
        self.n = self.n_ctx + self.n_lat
        self.ctx_row = nb_lat

    def tile(self, want):
        t = want
        while self.n_ctx % t or self.l_lat % t:
            t //= 2
        return t

    def mod_spec(self, layer, chunk, tm):
        n_ctx, l_lat, ctx_row = self.n_ctx, self.l_lat, self.ctx_row

        def idx(i, *_):
            start = i * tm
            row = jnp.where(start < n_ctx, ctx_row, (start - n_ctx) // l_lat)
            return (layer, row, chunk, 0, 0)

        return pl.BlockSpec((None, None, None, 1, self.d), idx)

    def x_specs(self, x, tm):
        if not isinstance(x, tuple):
            return [pl.BlockSpec((tm, self.d), lambda i: (i, 0))], [x]
        nct = self.n_ctx // tm
        return [pl.BlockSpec((tm, self.d), lambda i: (jnp.minimum(i, nct - 1), 0)),
                pl.BlockSpec((tm, self.d), lambda i: (jnp.maximum(i - nct, 0), 0))], list(x)

    def rope_spec(self, tm):
        n_ctx, l_lat = self.n_ctx, self.l_lat

        def idx(i):
            start = i * tm
            return (jnp.where(start < n_ctx, 0, 1 + ((start - n_ctx) % l_lat) // tm), 0)

        return pl.BlockSpec((tm, LANE), idx)


def _read_x(x_refs, n_ctx_tiles):
    if len(x_refs) == 1:
        return x_refs[0][...]
    return jnp.where(pl.program_id(0) < n_ctx_tiles, x_refs[0][...], x_refs[1][...])


A_QK = N_HEAD_SCAN * DK
A_V = N_HEAD_SCAN * DV
E_QA, E_KA, E_VA, E_GA = 0, A_QK, 2 * A_QK, 2 * A_QK + A_V
E_QB = E_GA + A_V
E_KB = E_QB + A_QK
E_VB = E_KB + A_QK
E_GB = E_VB + A_V
E_GK = E_GB + A_V
E_QBS = E_GK + LANE
E_KBS = E_QBS + A_QK
E_COLS = E_KBS + A_QK


def _log_sigmoid(x):
    return jnp.minimum(x, 0.0) - jnp.log1p(jnp.exp(-jnp.abs(x)))


def _even_proj_body(*refs, n_x, n_ctx_tiles):
    x = _read_x(refs[:n_x], n_ctx_tiles)
    (g_ref, sh_ref, sc_ref, w_ref, wg_ref, bg_ref, dec_ref, cc_ref, ss_ref,
     q_ref, k_ref, lf_ref, lb_ref, gt_ref, v_ref) = refs[n_x:]
    tm = x.shape[0]
    h = (_rms(x) * g_ref[...] * (1.0 + sc_ref[...]) + sh_ref[...]).astype(BF16)

    def proj(start, width):
        return _dot(h, w_ref[:, start:start + width])

    cc = cc_ref[...]
    ss = ss_ref[...]
    scale = DK ** -0.5
    q_ref[:, 0:A_QK] = proj(E_QA, A_QK) * scale
    k_ref[:, 0:A_QK] = proj(E_KA, A_QK)
    qb, qbs, kb, kbs = proj(E_QB, A_QK), proj(E_QBS, A_QK), proj(E_KB, A_QK), proj(E_KBS, A_QK)
    for j in range(A_QK // LANE):
        sl = slice(j * LANE, (j + 1) * LANE)
        o = A_QK + j * LANE
        q_ref[:, o:o + LANE] = qb[:, sl] * cc + qbs[:, sl] * ss
        k_ref[:, o:o + LANE] = (kb[:, sl] * cc + kbs[:, sl] * ss) * scale
    v_ref[:, 0:A_V] = proj(E_VA, A_V).astype(BF16)
    v_ref[:, A_V:2 * A_V] = proj(E_VB, A_V).astype(BF16)
    gt_ref[:, 0:A_V] = proj(E_GA, A_V)
    gt_ref[:, A_V:2 * A_V] = proj(E_GB, A_V)
    gk = proj(E_GK, LANE).astype(BF16)
    la = _log_sigmoid(_dot(gk, wg_ref[...]) + bg_ref[...]) * (1.0 / GATE_NORM)
    lf_ref[:, 0:A_QK] = la[:, 0:A_QK]
    lb_ref[:, 0:A_QK] = la[:, A_QK:2 * A_QK]
    log_g = -jnp.exp(dec_ref[...])
    lf_ref[:, A_QK:2 * A_QK] = jnp.broadcast_to(log_g[0:1, :], (tm, A_QK))
    lb_ref[:, A_QK:2 * A_QK] = jnp.broadcast_to(log_g[1:2, :], (tm, A_QK))


def _even_proj_call(tok, x, mods, layer, idx, g_pre, w, wg, bg, dec, cc, ss):
    tm = tok.tile(EVEN_TILE)
    d = tok.d
    row = lambda i: (i, 0)
    outs = [
        jax.ShapeDtypeStruct((tok.n, 2 * A_QK), F32),
        jax.ShapeDtypeStruct((tok.n, 2 * A_QK), F32),
        jax.ShapeDtypeStruct((tok.n, 2 * A_QK), F32),
        jax.ShapeDtypeStruct((tok.n, 2 * A_QK), F32),
        jax.ShapeDtypeStruct((tok.n, 2 * A_V), F32),
        jax.ShapeDtypeStruct((tok.n, 2 * A_V), BF16),
    ]
    x_specs, x_args = tok.x_specs(x, tm)
    return pl.pallas_call(
        functools.partial(_even_proj_body, n_x=len(x_args), n_ctx_tiles=tok.n_ctx // tm),
        grid=(tok.n // tm,),
        in_specs=x_specs + [
            _full((1, d)),
            tok.mod_spec(layer, 0, tm),
            tok.mod_spec(layer, 1, tm),
            _layer_block(w, idx), _layer_block(wg, idx), _layer_block(bg, idx), _layer_block(dec, idx),
            tok.rope_spec(tm),
            tok.rope_spec(tm),
        ],
        out_specs=[pl.BlockSpec((tm, o.shape[1]), row) for o in outs],
        out_shape=outs,
        compiler_params=_cparams(("arbitrary",)),
        name="even_proj",
    )(*x_args, g_pre, mods, mods, w, wg, bg, dec, cc, ss)


SCAN_GROUP = 8


def _chunk_cumsum(x):
    row = lax.broadcasted_iota(jnp.int32, x.shape, 0) % CHUNK
    s = 1
    while s < CHUNK:
        x = x + jnp.where(row >= s, pltpu.roll(x, s, axis=0), 0.0)
        s *= 2
    return x


def _scan_body(q_ref, k_ref, lf_ref, lb_ref, v_ref, gt_ref, gain_ref, *rest, seq_len, group, context):
    if context:
        m_ref, sfin_ref, st_ref, o_acc = rest
    else:
        s0_ref, m_ref, st_ref, o_acc = rest
    C = CHUNK
    blk = group * C
    nblk = seq_len // blk
    pair_w = 2 * DK
    head0 = lax.broadcasted_iota(jnp.int32, (blk, pair_w), 1) < DK
    t_in = lax.broadcasted_iota(jnp.int32, (C, pair_w), 0)
    j_in = lax.broadcasted_iota(jnp.int32, (C, pair_w), 1) % DK
    keep_fwd = t_in >= j_in
    keep_bwd = t_in <= j_in
    on_diag = ((lax.broadcasted_iota(jnp.int32, (2 * DV, pair_w), 0) < DV)
               == (lax.broadcasted_iota(jnp.int32, (2 * DV, pair_w), 1) < DK))
    zeros_v = jnp.zeros((C, DV), BF16)

    if context:
        st_ref[...] = jnp.zeros(st_ref.shape, F32)
    else:
        st_ref[...] = s0_ref[...]

    def one_direction(r0, d, log_ref, keep, reverse):
        g = log_ref[pl.ds(r0, blk), :]
        b = _chunk_cumsum(g)
        tots = [b[C * j + C - 1:C * j + C, :] for j in range(group)]
        totb = jnp.concatenate([jnp.broadcast_to(t, (C, pair_w)) for t in tots], axis=0)
        if reverse:
            b = totb - b + g
        q = q_ref[pl.ds(r0, blk), :]
        k = k_ref[pl.ds(r0, blk), :]
        vblk = v_ref[pl.ds(r0, blk), :]
        q_dec = (q * jnp.exp(b)).astype(BF16)
        k_inv = k * jnp.exp(-b)
        k_up = (k * jnp.exp(totb - b)).astype(BF16)
        k_inv0 = jnp.where(head0, k_inv, 0.0).astype(BF16)
        k_inv1 = jnp.where(head0, 0.0, k_inv).astype(BF16)
        st = st_ref[d]
        outs = [None] * group
        for j in (reversed(range(group)) if reverse else range(group)):
            sl = slice(C * j, C * (j + 1))
            k_bd = jnp.concatenate([k_inv0[sl], k_inv1[sl]], axis=0)
            a = jnp.where(keep, _dot_nt(q_dec[sl], k_bd), 0.0).astype(BF16)
            vc = vblk[sl]
            v_bd = jnp.concatenate([jnp.concatenate([vc[:, :DV], zeros_v], axis=1),
                                    jnp.concatenate([zeros_v, vc[:, DV:]], axis=1)], axis=0)
            outs[j] = _dot_nt(q_dec[sl], st.astype(BF16)) + _dot(a, v_bd)
            st = st * jnp.exp(tots[j]) + jnp.where(on_diag, _dot_tn(vc, k_up[sl]), 0.0)
        st_ref[d] = st
        return jnp.concatenate(outs, axis=0)

    def emit(r0, o):
        gate = _silu(gt_ref[pl.ds(r0, blk), :])
        gain = gain_ref[...]
        for hh in range(2):
            sl = slice(hh * DV, (hh + 1) * DV)
            m_ref[pl.ds(r0, blk), sl] = (_rms(o[:, sl]) * gain[:, sl] * gate[:, sl]).astype(BF16)

    def body(c, carry, second_half):
        rf = pl.multiple_of(c * blk, blk)
        rb = pl.multiple_of((nblk - 1 - c) * blk, blk)
        o_f = one_direction(rf, 0, lf_ref, keep_fwd, False)
        o_b = one_direction(rb, 1, lb_ref, keep_bwd, True)
        if second_half:
            emit(rf, o_acc[pl.ds(rf, blk), :] + o_f)
            emit(rb, o_acc[pl.ds(rb, blk), :] + o_b)
        else:
            o_acc[pl.ds(rf, blk), :] = o_f
            o_acc[pl.ds(rb, blk), :] = o_b
        return carry

    if nblk == 1:
        emit(0, one_direction(0, 0, lf_ref, keep_fwd, False) + one_direction(0, 1, lb_ref, keep_bwd, True))
    else:
        lax.fori_loop(0, nblk // 2, functools.partial(body, second_half=False), 0)
        lax.fori_loop(nblk // 2, nblk, functools.partial(body, second_half=True), 0)

    if context:
        sfin_ref[...] = st_ref[...]


def _scan_call(tok, q, k, lf, lb, v, gt, gain, s0=None):
    context = s0 is None
    if context:
        nb, seq_len, blk0 = tok.nb_ctx, tok.l_ctx, 0
    else:
        nb, seq_len, blk0 = tok.nb_lat, tok.l_lat, tok.n_ctx // tok.l_lat
    pairs = q.shape[1] // (2 * DK)
    group = min(SCAN_GROUP, seq_len // CHUNK)
    seq = lambda b, p: (blk0 + b, p)
    st_spec = pl.BlockSpec((None, 2, None, 2 * DV, 2 * DK), lambda b, p: (b, 0, p, 0, 0))
    in_specs = [
        pl.BlockSpec((seq_len, 2 * DK), seq),
        pl.BlockSpec((seq_len, 2 * DK), seq),
        pl.BlockSpec((seq_len, 2 * DK), seq),
        pl.BlockSpec((seq_len, 2 * DK), seq),
        pl.BlockSpec((seq_len, 2 * DV), seq),
        pl.BlockSpec((seq_len, 2 * DV), seq),
        pl.BlockSpec((1, 2 * DV), lambda b, p: (0, p)),
    ]
    args = [q, k, lf, lb, v, gt, gain]
    out_shape = [jax.ShapeDtypeStruct(v.shape, v.dtype)]
    out_specs = [pl.BlockSpec((seq_len, 2 * DV), seq)]
    if context:
        out_shape.append(jax.ShapeDtypeStruct((nb, 2, pairs, 2 * DV, 2 * DK), F32))
        out_specs.append(st_spec)
    else:
        in_specs.append(st_spec)
        args.append(s0)
    return pl.pallas_call(
        functools.partial(_scan_body, seq_len=seq_len, group=group, context=context),
        grid=(nb, pairs),
        in_specs=in_specs,
        out_specs=out_specs,
        out_shape=out_shape,
        input_output_aliases={4: 0},
        scratch_shapes=[pltpu.VMEM((2, 2 * DV, 2 * DK), F32), pltpu.VMEM((seq_len, 2 * DV), F32)],
        compiler_params=_cparams(("arbitrary", "arbitrary")),
        name="scan_ctx" if context else "scan_lat",
    )(*args)


MIX_ROW_GROUPS = 4


def _mix_mlp_body(*refs, n_x, n_out, n_ctx_tiles):
    x = _read_x(refs[:n_x], n_ctx_tiles)
    (m_ref, wo_ref, gmix_ref, gate1_ref, gpre_ref, sh_ref, sc_ref, w1_ref, w2_ref,
     gpost_ref, gate2_ref) = refs[n_x:len(refs) - n_out]
    out_refs = refs[len(refs) - n_out:]
    rows = x.shape[0] // MIX_ROW_GROUPS
    x1_parts, h_parts = [], []
    for r0 in range(0, x.shape[0], rows):
        y = _dot(m_ref[r0:r0 + rows, :], wo_ref[...])
        x1_g = x[r0:r0 + rows] + gate1_ref[...] * (_rms(y) * gmix_ref[...])
        x1_parts.append(x1_g)
        h_parts.append((_rms(x1_g) * gpre_ref[...] * (1.0 + sc_ref[...]) + sh_ref[...]).astype(BF16))
    x1 = jnp.concatenate(x1_parts, axis=0)
    h = jnp.concatenate(h_parts, axis=0)
    u = jnp.maximum(_dot(h, w1_ref[...]), 0.0)
    z = _dot((u * u).astype(BF16), w2_ref[...])
    res = x1 + gate2_ref[...] * (_rms(z) * gpost_ref[...])
    if n_out == 1:
        out_refs[0][...] = res
    else:
        is_ctx = pl.program_id(0) < n_ctx_tiles

        @pl.when(is_ctx)
        def _():
            out_refs[0][...] = res

        @pl.when(jnp.logical_not(is_ctx))
        def _():
            out_refs[1][...] = res


def _mix_mlp_call(tok, m, x, mods, layer, w_out, w_out_idx, g_mix, g_pre, w1, w2, g_post, split_out=False):
    tm = tok.tile(512)
    d = tok.d
    row = lambda i: (i, 0)
    x_specs, x_args = tok.x_specs(x, tm)
    nct = tok.n_ctx // tm
    if split_out:
        out_specs = [pl.BlockSpec((tm, d), lambda i: (jnp.minimum(i, nct - 1), 0)),
                     pl.BlockSpec((tm, d), lambda i: (jnp.maximum(i - nct, 0), 0))]
        out_shape = [jax.ShapeDtypeStruct((tok.n_ctx, d), F32), jax.ShapeDtypeStruct((tok.n_lat, d), F32)]
    else:
        out_specs = [pl.BlockSpec((tm, d), row)]
        out_shape = [jax.ShapeDtypeStruct((tok.n, d), F32)]
    return pl.pallas_call(
        functools.partial(_mix_mlp_body, n_x=len(x_args), n_out=len(out_shape), n_ctx_tiles=nct),
        grid=(tok.n // tm,),
        in_specs=x_specs + [
            pl.BlockSpec((tm, m.shape[1]), row),
            _layer_block(w_out, w_out_idx),
            _full((1, d)),
            tok.mod_spec(layer, 2, tm),
            _full((1, d)),
            tok.mod_spec(layer, 3, tm),
            tok.mod_spec(layer, 4, tm),
            _layer_block(w1, layer),
            _layer_block(w2, layer),
            _full((1, d)),
            tok.mod_spec(layer, 5, tm),
        ],
        out_specs=out_specs,
        out_shape=out_shape,
        compiler_params=_cparams(("arbitrary",)),
        name="mix_mlp",
    )(*x_args, m, w_out, g_mix, mods, g_pre, mods, mods, w1, w2, g_post, mods)


HEAD_W = 2 * LANE
O_QLAT, O_CKV, O_KPE, O_KPES, O_COLS = 0, Q_LORA, Q_LORA + KV_LORA, Q_LORA + KV_LORA + LANE, Q_LORA + KV_LORA + 2 * LANE
QB_NOPE, QB_ROPE, QB_SWAP = 0, H_C * LANE, 2 * H_C * LANE


def _expand_kv(cb, kper, wkvb_ref, k_ref, v_ref):
    for hp in range(H_C // 2):
        nope2 = _dot(cb, wkvb_ref[:, hp * 2 * LANE:(hp + 1) * 2 * LANE])
        for j in range(2):
            hh = 2 * hp + j
            k_ref[:, hh * HEAD_W:hh * HEAD_W + LANE] = nope2[:, j * LANE:(j + 1) * LANE].astype(BF16)
            k_ref[:, hh * HEAD_W + LANE:(hh + 1) * HEAD_W] = kper
    v_ref[...] = _dot(cb, wkvb_ref[:, H_C * LANE:2 * H_C * LANE]).astype(BF16)


def _mla_proj_body(x_ref, g_ref, sh_ref, sc_ref, win_ref, qn_ref, wqb_ref, kvn_ref, wkvb_ref, cc_ref, ss_ref,
                   q_ref, k_ref, v_ref, ckv_ref, kpe_ref):
    h = (_rms(x_ref[...]) * g_ref[...] * (1.0 + sc_ref[...]) + sh_ref[...]).astype(BF16)
    cc = cc_ref[...]
    ss = ss_ref[...]
    qn = (_rms(_dot(h, win_ref[:, O_QLAT:O_QLAT + Q_LORA])) * qn_ref[...]).astype(BF16)
    ckvn = _rms(_dot(h, win_ref[:, O_CKV:O_CKV + KV_LORA])) * kvn_ref[...]
    kpe2 = _dot(h, win_ref[:, O_KPE:O_KPE + 2 * LANE])
    kpe = kpe2[:, 0:LANE]
    kper = (kpe * cc + kpe2[:, LANE:2 * LANE] * ss).astype(BF16)
    ckv_ref[...] = ckvn
    kpe_ref[...] = kpe
    for hp in range(H_C // 2):
        o = hp * 2 * LANE
        nope2 = _dot(qn, wqb_ref[:, QB_NOPE + o:QB_NOPE + o + 2 * LANE])
        rope2 = _dot(qn, wqb_ref[:, QB_ROPE + o:QB_ROPE + o + 2 * LANE])
        swap2 = _dot(qn, wqb_ref[:, QB_SWAP + o:QB_SWAP + o + 2 * LANE])
        for j in range(2):
            hh = 2 * hp + j
            sl = slice(j * LANE, (j + 1) * LANE)
            rot = rope2[:, sl] * cc + swap2[:, sl] * ss
            q_ref[:, hh * HEAD_W:hh * HEAD_W + LANE] = (nope2[:, sl] * ATTN_Q_SCALE).astype(BF16)
            q_ref[:, hh * HEAD_W + LANE:(hh + 1) * HEAD_W] = (rot * ATTN_Q_SCALE).astype(BF16)
    _expand_kv(ckvn.astype(BF16), kper, wkvb_ref, k_ref, v_ref)


def _mla_proj_call(tok, x, mods, layer, idx, g_pre, win, qn, wqb, kvn, wkvb, cc, ss):
    tm = tok.tile(MLA_TILE)
    d = tok.d
    row = lambda i: (i, 0)
    outs = [
        jax.ShapeDtypeStruct((tok.n, H_C * HEAD_W), BF16),
        jax.ShapeDtypeStruct((tok.n, H_C * HEAD_W), BF16),
        jax.ShapeDtypeStruct((tok.n, H_C * V_HEAD_C), BF16),
        jax.ShapeDtypeStruct((tok.n, KV_LORA), F32),
        jax.ShapeDtypeStruct((tok.n, LANE), F32),
    ]
    return pl.pallas_call(
        _mla_proj_body,
        grid=(tok.n // tm,),
        in_specs=[
            pl.BlockSpec((tm, d), row),
            _full((1, d)),
            tok.mod_spec(layer, 0, tm),
            tok.mod_spec(layer, 1, tm),
            _layer_block(win, idx), _full(qn.shape), _layer_block(wqb, idx), _full(kvn.shape),
            _layer_block(wkvb, idx),
            tok.rope_spec(tm),
            tok.rope_spec(tm),
        ],
        out_specs=[pl.BlockSpec((tm, o.shape[1]), row) for o in outs],
        out_shape=outs,
        compiler_params=_cparams(("arbitrary",)),
        name="mla_proj",
    )(x, g_pre, mods, mods, win, qn, wqb, kvn, wkvb, cc, ss)


def _cache_expand_body(ckv_ref, kpe_ref, wkvb_ref, k_ref, v_ref):
    _expand_kv(ckv_ref[...].astype(BF16), kpe_ref[...].astype(BF16), wkvb_ref, k_ref, v_ref)


def _cache_expand_call(ckv, kpe_pad, wkvb, idx):
    n = ckv.shape[0]
    tm = 512
    while n % tm:
        tm //= 2
    row = lambda i: (i, 0)
    outs = [jax.ShapeDtypeStruct((n, H_C * HEAD_W), BF16), jax.ShapeDtypeStruct((n, H_C * V_HEAD_C), BF16)]
    return pl.pallas_call(
        _cache_expand_body,
        grid=(n // tm,),
        in_specs=[pl.BlockSpec((tm, KV_LORA), row), pl.BlockSpec((tm, LANE), row), _layer_block(wkvb, idx)],
        out_specs=[pl.BlockSpec((tm, o.shape[1]), row) for o in outs],
        out_shape=outs,
        compiler_params=_cparams(("arbitrary",)),
        name="cache_expand",
    )(ckv, kpe_pad, wkvb)


ATTN_TQ = 512
ATTN_TK = 512
ATTN_HEADS_PER_STEP = 2
ATTN_Q_SCALE = (QK_NOPE + QK_ROPE) ** -0.5 * float(np.log2(np.e))


def _softmax_pv(s, values):
    p = jnp.exp2(s - jnp.max(s, axis=-1, keepdims=True))
    den = jnp.sum(p, axis=-1, keepdims=True)
    p = p.astype(BF16)
    acc = functools.reduce(jnp.add, [_dot(p[:, k0:k0 + vb.shape[0]], vb) for vb, k0 in values])
    return (acc / den).astype(BF16)


def _attn_ctx_body(q_ref, k_ref, v_ref, o_ref):
    for hh in range(H_C):
        s = _dot_nt(q_ref[:, hh * HEAD_W:(hh + 1) * HEAD_W], k_ref[:, hh * HEAD_W:(hh + 1) * HEAD_W])
        o_ref[:, hh * V_HEAD_C:(hh + 1) * V_HEAD_C] = _softmax_pv(s, [(v_ref[:, hh * V_HEAD_C:(hh + 1) * V_HEAD_C], 0)])


def _lane_groups(x):
    return [x[:, g:g + LANE] for g in range(0, x.shape[1], LANE)]


def _attn_lat_body(q_ref, kc_ref, vc_ref, k_ref, v_ref, o_ref, sa_ref, sb_ref, ma_ref, mb_ref, vx_ref):
    past, n_self = kc_ref.shape[0], k_ref.shape[0]
    heads = vx_ref.shape[0]
    tq = sa_ref.shape[0]
    n = q_ref.shape[0] // tq
    tk = min(ATTN_TK, n_self)
    blocks = [(kc_ref, 0, past, 0)] + [(k_ref, k0, tk, past + k0) for k0 in range(0, n_self, tk)]

    for h in range(heads):
        vx_ref[h, 0:past, 0:V_HEAD_C] = vc_ref[:, h * V_HEAD_C:(h + 1) * V_HEAD_C]
        vx_ref[h, past:past + n_self, 0:V_HEAD_C] = v_ref[:, h * V_HEAD_C:(h + 1) * V_HEAD_C]
        vx_ref[h, :, V_HEAD_C:2 * V_HEAD_C] = jnp.ones((past + n_self, V_HEAD_C), BF16)

    def scores(h, i, s_ref, m_ref):
        q = q_ref[pl.ds(pl.multiple_of(i * tq, tq), tq), h * HEAD_W:(h + 1) * HEAD_W]
        mx = None
        for kk_ref, k0, size, col in blocks:
            s = _dot_nt(q, kk_ref[k0:k0 + size, h * HEAD_W:(h + 1) * HEAD_W])
            s_ref[:, col:col + size] = s
            mx = functools.reduce(jnp.maximum, _lane_groups(s) + ([] if mx is None else [mx]))
        m_ref[...] = mx

    def finish(h, i, s_ref, m_ref):
        m = jnp.max(m_ref[...], axis=-1, keepdims=True)
        acc = None
        for _, _, size, col in blocks:
            p = jnp.exp2(s_ref[:, col:col + size] - m).astype(BF16)
            pv = _dot(p, vx_ref[h, col:col + size, :])
            acc = pv if acc is None else acc + pv
        out = acc[:, 0:V_HEAD_C] / acc[:, V_HEAD_C:2 * V_HEAD_C]
        o_ref[pl.ds(pl.multiple_of(i * tq, tq), tq), h * V_HEAD_C:(h + 1) * V_HEAD_C] = out.astype(BF16)

    scores(0, 0, sa_ref, ma_ref)
    for h in range(heads):

        def two_tiles(j, carry, h=h):
            i = 2 * j
            scores(h, i + 1, sb_ref, mb_ref)
            finish(h, i, sa_ref, ma_ref)
            scores(h, i + 2, sa_ref, ma_ref)
            finish(h, i + 1, sb_ref, mb_ref)
            return carry

        lax.fori_loop(0, n // 2 - 1, two_tiles, 0)
        scores(h, n - 1, sb_ref, mb_ref)
        finish(h, n - 2, sa_ref, ma_ref)
        if h + 1 < heads:
            scores(h + 1, 0, sa_ref, ma_ref)
        finish(h, n - 1, sb_ref, mb_ref)


def _attn_ctx_call(tok, q, k, v):
    seq = lambda b: (b, 0)
    return pl.pallas_call(
        _attn_ctx_body,
        grid=(tok.nb_ctx,),
        in_specs=[pl.BlockSpec((tok.l_ctx, H_C * HEAD_W), seq),
                  pl.BlockSpec((tok.l_ctx, H_C * HEAD_W), seq),
                  pl.BlockSpec((tok.l_ctx, H_C * V_HEAD_C), seq)],
        out_specs=pl.BlockSpec((tok.l_ctx, H_C * V_HEAD_C), seq),
        out_shape=jax.ShapeDtypeStruct(v.shape, v.dtype),
        input_output_aliases={2: 0},
        compiler_params=_cparams(("arbitrary",)),
        name="attn_ctx",
    )(q, k, v)


def _attn_lat_call(tok, q, k, v, k_cache, v_cache):
    blk0 = tok.n_ctx // tok.l_lat
    past = k_cache.shape[0] // tok.nb_lat
    tq = min(ATTN_TQ, tok.l_lat // 2)
    hg = ATTN_HEADS_PER_STEP
    seq = lambda b, hh: (blk0 + b, hh)
    cache = lambda b, hh: (b, hh)
    return pl.pallas_call(
        _attn_lat_body,
        grid=(tok.nb_lat, H_C // hg),
        in_specs=[pl.BlockSpec((tok.l_lat, hg * HEAD_W), seq),
                  pl.BlockSpec((past, hg * HEAD_W), cache),
                  pl.BlockSpec((past, hg * V_HEAD_C), cache),
                  pl.BlockSpec((tok.l_lat, hg * HEAD_W), seq),
                  pl.BlockSpec((tok.l_lat, hg * V_HEAD_C), seq)],
        out_specs=pl.BlockSpec((tok.l_lat, hg * V_HEAD_C), seq),
        out_shape=jax.ShapeDtypeStruct(v.shape, v.dtype),
        input_output_aliases={4: 0},
        scratch_shapes=([pltpu.VMEM((tq, past + tok.l_lat), F32)] * 2 + [pltpu.VMEM((tq, LANE), F32)] * 2
                        + [pltpu.VMEM((hg, past + tok.l_lat, 2 * V_HEAD_C), BF16)]),
        compiler_params=_cparams(("arbitrary", "arbitrary")),
        name="attn_lat",
    )(q, k_cache, v_cache, k, v)


def _swap_halves(w, head_dim):
    lead = w.shape[:-1]
    halves = w.reshape(lead + (w.shape[-1] // head_dim, 2, head_dim // 2))
    return halves[..., ::-1, :].reshape(w.shape)


def _pad_cols(w, width):
    return jnp.pad(w, ((0, 0),) * (w.ndim - 1) + ((0, width - w.shape[-1]),))


def _even_weights(w_in, w_gk2, b_gk2):
    n = w_in.shape[0]
    sizes = (A_QK, A_QK, A_V, A_V, 2 * GATE_RANK, A_QK, A_QK, A_V, A_V)
    qa, ka, va, ga, gk, qb, kb, vb, gb = jnp.split(w_in, np.cumsum(sizes)[:-1].tolist(), axis=-1)
    w = jnp.concatenate([qa, ka, va, ga, qb, kb, vb, gb, _pad_cols(gk, LANE),
                         _swap_halves(qb, DK), _swap_halves(kb, DK)], axis=-1).astype(BF16)
    wg = jnp.zeros((n, LANE, 2 * A_QK), F32)
    wg = wg.at[:, 0:GATE_RANK, 0:A_QK].set(w_gk2[:, 0]).at[:, GATE_RANK:2 * GATE_RANK, A_QK:2 * A_QK].set(w_gk2[:, 1])
    bg = b_gk2.reshape(n, 1, 2 * A_QK)
    return w, wg.astype(BF16), bg


def _odd_weights(w_in, w_q_b, w_kv_b):
    n = w_in.shape[0]
    q_lat, ckv, kpe = w_in[..., :Q_LORA], w_in[..., Q_LORA:Q_LORA + KV_LORA], w_in[..., Q_LORA + KV_LORA:]
    win = jnp.concatenate([q_lat, ckv, _pad_cols(kpe, LANE), _pad_cols(_swap_halves(kpe, QK_ROPE), LANE)],
                          axis=-1).astype(BF16)
    wq = w_q_b.reshape(n, Q_LORA, H_C, QK_NOPE + QK_ROPE)
    nope = wq[..., :QK_NOPE].reshape(n, Q_LORA, H_C * QK_NOPE)
    rope = wq[..., QK_NOPE:]
    pad = lambda r: _pad_cols(r, LANE).reshape(n, Q_LORA, H_C * LANE)
    wqb = jnp.concatenate([nope, pad(rope), pad(_swap_halves(rope, QK_ROPE))], axis=-1).astype(BF16)
    wkv = w_kv_b.reshape(n, KV_LORA, H_C, QK_NOPE + V_HEAD_C)
    wkvb = jnp.concatenate([wkv[..., :QK_NOPE].reshape(n, KV_LORA, H_C * QK_NOPE),
                            wkv[..., QK_NOPE:].reshape(n, KV_LORA, H_C * V_HEAD_C)], axis=-1).astype(BF16)
    return win, wqb, wkvb


def _rope_tables(tok, tm):
    rows = tok.l_lat // GRID_W
    row = jnp.repeat(jnp.arange(rows), GRID_W).astype(F32)
    col = jnp.tile(jnp.arange(GRID_W), rows).astype(F32)
    n_freq = QK_ROPE // 4
    inv = ROPE_BASE ** (-jnp.arange(n_freq, dtype=F32) / n_freq)
    ang = jnp.concatenate([row[:, None] * inv, col[:, None] * inv], axis=-1)
    cos, sin = jnp.cos(ang), jnp.sin(ang)
    cc = jnp.tile(jnp.concatenate([cos, cos], axis=-1), (1, LANE // QK_ROPE))
    ss = jnp.tile(jnp.concatenate([-sin, sin], axis=-1), (1, LANE // QK_ROPE))
    cc = jnp.concatenate([jnp.ones((tm, LANE), F32), cc], axis=0)
    ss = jnp.concatenate([jnp.zeros((tm, LANE), F32), ss], axis=0)
    return cc, ss


def _states_to_kernel(s):
    nb, _, heads = s.shape[:3]
    st = jnp.swapaxes(s.reshape(nb, 2, heads // 2, 2, DK, DV), -1, -2)
    z = jnp.zeros_like(st[:, :, :, 0])
    rows = [jnp.concatenate([st[:, :, :, 0], z], axis=-1), jnp.concatenate([z, st[:, :, :, 1]], axis=-1)]
    return jnp.concatenate(rows, axis=-2)


def _states_from_kernel(st):
    nb, _, pairs = st.shape[:3]
    heads = jnp.stack([st[:, :, :, :DV, :DK], st[:, :, :, DV:, DK:]], axis=3)
    return jnp.swapaxes(heads, -1, -2).reshape(nb, 2, 2 * pairs, DK, DV)


def kernel(x_prompt, x_sample, cache_ckv, cache_kpe, state_gla, state_ret, c, c_ctx, w_ada, b_ada, norm_mix_pre, norm_mix_post, norm_mlp_pre, norm_mlp_post, w_in_even, w_gk2, b_gk2, gla_norm, ret_decay, w_out_even, w_in_odd, q_a_norm, w_q_b, kv_a_norm, w_kv_b, w_out_odd, w_mlp1, w_mlp2):
    nb_ctx, l_ctx, d = x_prompt.shape
    nb_lat, l_lat, _ = x_sample.shape
    depth = w_ada.shape[0]
    tok = _Tokens(nb_ctx, l_ctx, nb_lat, l_lat, d)
    assert nb_lat < MOD_ROWS and tok.n_ctx % l_lat == 0 and l_ctx % CHUNK == 0 and l_lat % (2 * CHUNK) == 0

    cond = jnp.concatenate([c, c_ctx[None, :], jnp.zeros((MOD_ROWS - nb_lat - 1, d), F32)], axis=0)
    mods = _ada_call(cond, w_ada, b_ada).reshape(depth, MOD_ROWS, 6, 1, d)
    rope_even = _rope_tables(tok, tok.tile(EVEN_TILE))
    rope_odd = _rope_tables(tok, tok.tile(MLA_TILE))
    x = (x_prompt.reshape(tok.n_ctx, d), x_sample.reshape(tok.n_lat, d))
    vec = lambda a: a.reshape(1, -1)
    w_out_even16, w_out_odd16 = w_out_even.astype(BF16), w_out_odd.astype(BF16)
    w_mlp1_16, w_mlp2_16 = w_mlp1.astype(BF16), w_mlp2.astype(BF16)
    even_w = _even_weights(w_in_even, w_gk2, b_gk2) + (jnp.repeat(ret_decay, DK, axis=-1),)
    odd_w = _odd_weights(w_in_odd, w_q_b, w_kv_b)

    new_ckv, new_kpe, new_gla, new_ret = [], [], [], []
    for l in range(depth):
        i = l // 2
        if l % 2 == 0:
            q, k, lf, lb, gt, v = _even_proj_call(tok, x, mods, l, i, vec(norm_mix_pre[l]), *even_w, *rope_even)
            gain = jnp.concatenate([jnp.tile(gla_norm[i], N_HEAD_SCAN), jnp.ones((A_V,), F32)]).reshape(1, 2 * A_V)
            s0_lat = _states_to_kernel(jnp.concatenate([state_gla[:, i], state_ret[:, i]], axis=2))
            m_ctx, s_fin = _scan_call(tok, q, k, lf, lb, v, gt, gain)
            (m,) = _scan_call(tok, q, k, lf, lb, m_ctx, gt, gain, s0=s0_lat)
            s_fin = _states_from_kernel(s_fin)
            new_gla.append(s_fin[:, :, :N_HEAD_SCAN])
            new_ret.append(s_fin[:, :, N_HEAD_SCAN:])
            w_out = w_out_even16
        else:
            win, wqb, wkvb = odd_w
            q, k, v, ckv, kpe = _mla_proj_call(tok, x, mods, l, i, vec(norm_mix_pre[l]), win, vec(q_a_norm[i]), wqb,
                                               vec(kv_a_norm[i]), wkvb, *rope_odd)
            past = cache_ckv.shape[2]
            kpe_pad = jnp.pad(cache_kpe[:, i].reshape(nb_lat * past, QK_ROPE), ((0, 0), (0, LANE - QK_ROPE)))
            k_c, v_c = _cache_expand_call(cache_ckv[:, i].reshape(nb_lat * past, KV_LORA), kpe_pad, wkvb, i)
            m = _attn_lat_call(tok, q, k, _attn_ctx_call(tok, q, k, v), k_c, v_c)
            new_ckv.append(ckv[:tok.n_ctx].reshape(nb_ctx, l_ctx, KV_LORA))
            new_kpe.append(kpe[:tok.n_ctx, :QK_ROPE].reshape(nb_ctx, l_ctx, QK_ROPE))
            w_out = w_out_odd16
        x = _mix_mlp_call(tok, m, x, mods, l, w_out, i, vec(norm_mix_post[l]), vec(norm_mlp_pre[l]), w_mlp1_16,
                          w_mlp2_16, vec(norm_mlp_post[l]), split_out=(l == depth - 1))
        x = x[0] if len(x) == 1 else tuple(x)

    return (x[0].reshape(nb_ctx, l_ctx, d), x[1].reshape(nb_lat, l_lat, d),
            jnp.stack(new_ckv, axis=1), jnp.stack(new_kpe, axis=1),
            jnp.stack(new_gla, axis=1), jnp.stack(new_ret, axis=1))
```

```python
import functools

import numpy as np
import jax
import jax.numpy as jnp
from jax import lax
from jax.experimental import pallas as pl
from jax.experimental.pallas import tpu as pltpu

F32 = jnp.float32
BF16 = jnp.bfloat16

EPS = 1e-6
ROPE_BASE = 10000.0
GRID_W = 64
CHUNK = 64
GATE_RANK = 16
GATE_NORM = 16.0
N_HEAD_SCAN = 4
DK = 64
DV = 128
H_C = 8
Q_LORA = 256
KV_LORA = 256
QK_NOPE = 128
QK_ROPE = 64
V_HEAD_C = 128
LANE = 128
MOD_ROWS = 16
EVEN_TILE = 512
MLA_TILE = 1024

VMEM_LIMIT = 56 * 1024 * 1024


def _cparams(sem):
    return pltpu.CompilerParams(dimension_semantics=sem, vmem_limit_bytes=VMEM_LIMIT)


def _dot(a, b):
    return jnp.dot(a, b, preferred_element_type=F32)


def _dot_nt(a, b):
    return lax.dot_general(a, b, (((1,), (1,)), ((), ())), preferred_element_type=F32)


def _dot_tn(a, b):
    return lax.dot_general(a, b, (((0,), (0,)), ((), ())), preferred_element_type=F32)


def _rms(x):
    return x * lax.rsqrt(jnp.mean(x * x, axis=-1, keepdims=True) + EPS)


def _silu(x):
    return x * jax.nn.sigmoid(x)


def _full(shape):
    n = len(shape)
    return pl.BlockSpec(shape, lambda *_: (0,) * n, pipeline_mode=pl.Buffered(1))


def _layer_block(w, idx):
    tail = (0,) * (w.ndim - 1)
    return pl.BlockSpec((None,) + w.shape[1:], lambda *_: (idx,) + tail, pipeline_mode=pl.Buffered(1))


def _side_cast_specs(stacks, layer, steps):
    nb = 1 << (steps.bit_length() - 1)
    block = lambda i: jnp.minimum(i, nb - 1)
    ins = [pl.BlockSpec((None, w.shape[1] // nb, w.shape[2]), lambda i: (layer, block(i), 0)) for w in stacks]
    outs = [pl.BlockSpec((w.shape[1] // nb, w.shape[2]), lambda i: (block(i), 0)) for w in stacks]
    shapes = [jax.ShapeDtypeStruct(w.shape[1:], BF16) for w in stacks]
    return ins, outs, shapes


def _side_cast(in_refs, out_refs):
    for src, dst in zip(in_refs, out_refs):
        dst[...] = src[...].astype(BF16)


def _ada_body(cond_ref, w_ref, b_ref, o_ref):
    s = _silu(cond_ref[...]).astype(BF16)
    o_ref[...] = _dot(s, w_ref[...].astype(BF16)) + b_ref[...]


def _ada_call(cond, w_ada, b_ada):
    depth, d, n = w_ada.shape
    tn = 1536
    return pl.pallas_call(
        _ada_body,
        grid=(depth, n // tn),
        in_specs=[
            pl.BlockSpec((MOD_ROWS, d), lambda l, j: (0, 0)),
            pl.BlockSpec((None, d, tn), lambda l, j: (l, 0, j)),
            pl.BlockSpec((None, 1, tn), lambda l, j: (l, 0, j)),
        ],
        out_specs=pl.BlockSpec((None, MOD_ROWS, tn), lambda l, j: (l, 0, j)),
        out_shape=jax.ShapeDtypeStruct((depth, MOD_ROWS, n), F32),
        compiler_params=_cparams(("arbitrary", "arbitrary")),
        name="ada_mod",
    )(cond, w_ada, b_ada.reshape(depth, 1, n))


class _Tokens:
    def __init__(self, nb_ctx, l_ctx, nb_lat, l_lat, d):
        self.nb_ctx, self.l_ctx, self.nb_lat, self.l_lat, self.d = nb_ctx, l_ctx, nb_lat, l_lat, d
        self.n_ctx = nb_ctx * l_ctx
        self.n_lat = nb_lat * l_lat
        self.n = self.n_ctx + self.n_lat
        self.ctx_row = nb_lat

    def tile(self, want):
        t = want
        while self.n_ctx % t or self.l_lat % t:
            t //= 2
        return t

    def mod_spec(self, layer, chunk, tm):
        n_ctx, l_lat, ctx_row = self.n_ctx, self.l_lat, self.ctx_row

        def idx(i, *_):
            start = i * tm
            row = jnp.where(start < n_ctx, ctx_row, (start - n_ctx) // l_lat)
            return (layer, row, chunk, 0, 0)

        return pl.BlockSpec((None, None, None, 1, self.d), idx)

    def x_specs(self, x, tm):
        if not isinstance(x, tuple):
            return [pl.BlockSpec((tm, self.d), lambda i: (i, 0))], [x]
        nct = self.n_ctx // tm
        return [pl.BlockSpec((tm, self.d), lambda i: (jnp.minimum(i, nct - 1), 0)),
                pl.BlockSpec((tm, self.d), lambda i: (jnp.maximum(i - nct, 0), 0))], list(x)

    def rope_spec(self, tm):
        n_ctx, l_lat = self.n_ctx, self.l_lat

        def idx(i):
            start = i * tm
            return (jnp.where(start < n_ctx, 0, 1 + ((start - n_ctx) % l_lat) // tm), 0)

        return pl.BlockSpec((tm, LANE), idx)


def _read_x(x_refs, n_ctx_tiles):
    if len(x_refs) == 1:
        return x_refs[0][...]
    return jnp.where(pl.program_id(0) < n_ctx_tiles, x_refs[0][...], x_refs[1][...])


A_QK = N_HEAD_SCAN * DK
A_V = N_HEAD_SCAN * DV
E_QA, E_KA, E_VA, E_GA = 0, A_QK, 2 * A_QK, 2 * A_QK + A_V
E_QB = E_GA + A_V
E_KB = E_QB + A_QK
E_VB = E_KB + A_QK
E_GB = E_VB + A_V
E_GK = E_GB + A_V
E_QBS = E_GK + LANE
E_KBS = E_QBS + A_QK
E_COLS = E_KBS + A_QK


def _log_sigmoid(x):
    return jnp.minimum(x, 0.0) - jnp.log1p(jnp.exp(-jnp.abs(x)))


def _even_proj_body(*refs, n_x, n_ctx_tiles, n_cast):
    x = _read_x(refs[:n_x], n_ctx_tiles)
    g_ref, sh_ref, sc_ref, w_ref, wg_ref, bg_ref, dec_ref, cc_ref, ss_ref = refs[n_x:n_x + 9]
    q_ref, k_ref, lf_ref, lb_ref, gt_ref, v_ref = refs[n_x + 9 + n_cast:n_x + 15 + n_cast]
    _side_cast(refs[n_x + 9:n_x + 9 + n_cast], refs[n_x + 15 + n_cast:])
    tm = x.shape[0]
    h = (_rms(x) * g_ref[...] * (1.0 + sc_ref[...]) + sh_ref[...]).astype(BF16)

    def proj(start, width):
        return _dot(h, w_ref[:, start:start + width])

    cc = cc_ref[...]
    ss = ss_ref[...]
    scale = DK ** -0.5
    q_ref[:, 0:A_QK] = proj(E_QA, A_QK) * scale
    k_ref[:, 0:A_QK] = proj(E_KA, A_QK)
    qb, qbs, kb, kbs = proj(E_QB, A_QK), proj(E_QBS, A_QK), proj(E_KB, A_QK), proj(E_KBS, A_QK)
    for j in range(A_QK // LANE):
        sl = slice(j * LANE, (j + 1) * LANE)
        o = A_QK + j * LANE
        q_ref[:, o:o + LANE] = qb[:, sl] * cc + qbs[:, sl] * ss
        k_ref[:, o:o + LANE] = (kb[:, sl] * cc + kbs[:, sl] * ss) * scale
    v_ref[:, 0:A_V] = proj(E_VA, A_V).astype(BF16)
    v_ref[:, A_V:2 * A_V] = proj(E_VB, A_V).astype(BF16)
    gt_ref[:, 0:A_V] = proj(E_GA, A_V)
    gt_ref[:, A_V:2 * A_V] = proj(E_GB, A_V)
    gk = proj(E_GK, LANE).astype(BF16)
    la = _log_sigmoid(_dot(gk, wg_ref[...]) + bg_ref[...]) * (1.0 / GATE_NORM)
    lf_ref[:, 0:A_QK] = la[:, 0:A_QK]
    lb_ref[:, 0:A_QK] = la[:, A_QK:2 * A_QK]
    log_g = -jnp.exp(dec_ref[...])
    lf_ref[:, A_QK:2 * A_QK] = jnp.broadcast_to(log_g[0:1, :], (tm, A_QK))
    lb_ref[:, A_QK:2 * A_QK] = jnp.broadcast_to(log_g[1:2, :], (tm, A_QK))


def _even_proj_call(tok, x, mods, layer, idx, g_pre, w, wg, bg, dec, cc, ss, cast_stacks):
    tm = tok.tile(EVEN_TILE)
    d = tok.d
    row = lambda i: (i, 0)
    outs = [
        jax.ShapeDtypeStruct((tok.n, 2 * A_QK), F32),
        jax.ShapeDtypeStruct((tok.n, 2 * A_QK), F32),
        jax.ShapeDtypeStruct((tok.n, 2 * A_QK), F32),
        jax.ShapeDtypeStruct((tok.n, 2 * A_QK), F32),
        jax.ShapeDtypeStruct((tok.n, 2 * A_V), F32),
        jax.ShapeDtypeStruct((tok.n, 2 * A_V), BF16),
    ]
    x_specs, x_args = tok.x_specs(x, tm)
    cast_in, cast_out, cast_shape = _side_cast_specs(cast_stacks, layer, tok.n // tm)
    return pl.pallas_call(
        functools.partial(_even_proj_body, n_x=len(x_args), n_ctx_tiles=tok.n_ctx // tm, n_cast=len(cast_stacks)),
        grid=(tok.n // tm,),
        in_specs=x_specs + [
            _full((1, d)),
            tok.mod_spec(layer, 0, tm),
            tok.mod_spec(layer, 1, tm),
            _layer_block(w, idx), _layer_block(wg, idx), _layer_block(bg, idx), _layer_block(dec, idx),
            tok.rope_spec(tm),
            tok.rope_spec(tm),
        ] + cast_in,
        out_specs=[pl.BlockSpec((tm, o.shape[1]), row) for o in outs] + cast_out,
        out_shape=outs + cast_shape,
        compiler_params=_cparams(("arbitrary",)),
        name="even_proj",
    )(*x_args, g_pre, mods, mods, w, wg, bg, dec, cc, ss, *cast_stacks)


SCAN_GROUP = 8


def _chunk_cumsum(x):
    row = lax.broadcasted_iota(jnp.int32, x.shape, 0) % CHUNK
    s = 1
    while s < CHUNK:
        x = x + jnp.where(row >= s, pltpu.roll(x, s, axis=0), 0.0)
        s *= 2
    return x


def _scan_body(q_ref, k_ref, lf_ref, lb_ref, v_ref, gt_ref, gain_ref, *rest, seq_len, group, context):
    if context:
        m_ref, sfin_ref, st_ref, o_acc = rest
    else:
        s0_ref, m_ref, st_ref, o_acc = rest
    C = CHUNK
    blk = group * C
    nblk = seq_len // blk
    pair_w = 2 * DK
    head0 = lax.broadcasted_iota(jnp.int32, (blk, pair_w), 1) < DK
    t_in = lax.broadcasted_iota(jnp.int32, (C, pair_w), 0)
    j_in = lax.broadcasted_iota(jnp.int32, (C, pair_w), 1) % DK
    keep_fwd = t_in >= j_in
    keep_bwd = t_in <= j_in
    on_diag = ((lax.broadcasted_iota(jnp.int32, (2 * DV, pair_w), 0) < DV)
               == (lax.broadcasted_iota(jnp.int32, (2 * DV, pair_w), 1) < DK))
    zeros_v = jnp.zeros((C, DV), BF16)

    if context:
        st_ref[...] = jnp.zeros(st_ref.shape, F32)
    else:
        st_ref[...] = s0_ref[...]

    def one_direction(r0, d, log_ref, keep, reverse):
        g = log_ref[pl.ds(r0, blk), :]
        b = _chunk_cumsum(g)
        tots = [b[C * j + C - 1:C * j + C, :] for j in range(group)]
        totb = jnp.concatenate([jnp.broadcast_to(t, (C, pair_w)) for t in tots], axis=0)
        if reverse:
            b = totb - b + g
        q = q_ref[pl.ds(r0, blk), :]
        k = k_ref[pl.ds(r0, blk), :]
        vblk = v_ref[pl.ds(r0, blk), :]
        q_dec = (q * jnp.exp(b)).astype(BF16)
        k_inv = k * jnp.exp(-b)
        k_up = (k * jnp.exp(totb - b)).astype(BF16)
        k_inv0 = jnp.where(head0, k_inv, 0.0).astype(BF16)
        k_inv1 = jnp.where(head0, 0.0, k_inv).astype(BF16)
        st = st_ref[d]
        outs = [None] * group
        for j in (reversed(range(group)) if reverse else range(group)):
            sl = slice(C * j, C * (j + 1))
            k_bd = jnp.concatenate([k_inv0[sl], k_inv1[sl]], axis=0)
            a = jnp.where(keep, _dot_nt(q_dec[sl], k_bd), 0.0).astype(BF16)
            vc = vblk[sl]
            v_bd = jnp.concatenate([jnp.concatenate([vc[:, :DV], zeros_v], axis=1),
                                    jnp.concatenate([zeros_v, vc[:, DV:]], axis=1)], axis=0)
            outs[j] = _dot_nt(q_dec[sl], st.astype(BF16)) + _dot(a, v_bd)
            st = st * jnp.exp(tots[j]) + jnp.where(on_diag, _dot_tn(vc, k_up[sl]), 0.0)
        st_ref[d] = st
        return jnp.concatenate(outs, axis=0)

    def emit(r0, o):
        gate = _silu(gt_ref[pl.ds(r0, blk), :])
        gain = gain_ref[...]
        for hh in range(2):
            sl = slice(hh * DV, (hh + 1) * DV)
            m_ref[pl.ds(r0, blk), sl] = (_rms(o[:, sl]) * gain[:, sl] * gate[:, sl]).astype(BF16)

    def body(c, carry, second_half):
        rf = pl.multiple_of(c * blk, blk)
        rb = pl.multiple_of((nblk - 1 - c) * blk, blk)
        o_f = one_direction(rf, 0, lf_ref, keep_fwd, False)
        o_b = one_direction(rb, 1, lb_ref, keep_bwd, True)
        if second_half:
            emit(rf, o_acc[pl.ds(rf, blk), :] + o_f)
            emit(rb, o_acc[pl.ds(rb, blk), :] + o_b)
        else:
            o_acc[pl.ds(rf, blk), :] = o_f
            o_acc[pl.ds(rb, blk), :] = o_b
        return carry

    if nblk == 1:
        emit(0, one_direction(0, 0, lf_ref, keep_fwd, False) + one_direction(0, 1, lb_ref, keep_bwd, True))
    else:
        lax.fori_loop(0, nblk // 2, functools.partial(body, second_half=False), 0)
        lax.fori_loop(nblk // 2, nblk, functools.partial(body, second_half=True), 0)

    if context:
        sfin_ref[...] = st_ref[...]


def _scan_call(tok, q, k, lf, lb, v, gt, gain, s0=None):
    context = s0 is None
    if context:
        nb, seq_len, blk0 = tok.nb_ctx, tok.l_ctx, 0
    else:
        nb, seq_len, blk0 = tok.nb_lat, tok.l_lat, tok.n_ctx // tok.l_lat
    pairs = q.shape[1] // (2 * DK)
    group = min(SCAN_GROUP, seq_len // CHUNK)
    seq = lambda b, p: (blk0 + b, p)
    st_spec = pl.BlockSpec((None, 2, None, 2 * DV, 2 * DK), lambda b, p: (b, 0, p, 0, 0))
    in_specs = [
        pl.BlockSpec((seq_len, 2 * DK), seq),
        pl.BlockSpec((seq_len, 2 * DK), seq),
        pl.BlockSpec((seq_len, 2 * DK), seq),
        pl.BlockSpec((seq_len, 2 * DK), seq),
        pl.BlockSpec((seq_len, 2 * DV), seq),
        pl.BlockSpec((seq_len, 2 * DV), seq),
        pl.BlockSpec((1, 2 * DV), lambda b, p: (0, p)),
    ]
    args = [q, k, lf, lb, v, gt, gain]
    out_shape = [jax.ShapeDtypeStruct(v.shape, v.dtype)]
    out_specs = [pl.BlockSpec((seq_len, 2 * DV), seq)]
    if context:
        out_shape.append(jax.ShapeDtypeStruct((nb, 2, pairs, 2 * DV, 2 * DK), F32))
        out_specs.append(st_spec)
    else:
        in_specs.append(st_spec)
        args.append(s0)
    return pl.pallas_call(
        functools.partial(_scan_body, seq_len=seq_len, group=group, context=context),
        grid=(nb, pairs),
        in_specs=in_specs,
        out_specs=out_specs,
        out_shape=out_shape,
        input_output_aliases={4: 0},
        scratch_shapes=[pltpu.VMEM((2, 2 * DV, 2 * DK), F32), pltpu.VMEM((seq_len, 2 * DV), F32)],
        compiler_params=_cparams(("arbitrary", "arbitrary")),
        name="scan_ctx" if context else "scan_lat",
    )(*args)


MIX_ROW_GROUPS = 4


def _mix_mlp_body(*refs, n_x, n_out, n_ctx_tiles):
    x = _read_x(refs[:n_x], n_ctx_tiles)
    (m_ref, wo_ref, gmix_ref, gate1_ref, gpre_ref, sh_ref, sc_ref, w1_ref, w2_ref,
     gpost_ref, gate2_ref) = refs[n_x:len(refs) - n_out]
    out_refs = refs[len(refs) - n_out:]
    rows = x.shape[0] // MIX_ROW_GROUPS
    x1_parts, h_parts = [], []
    for r0 in range(0, x.shape[0], rows):
        y = _dot(m_ref[r0:r0 + rows, :], wo_ref[...])
        x1_g = x[r0:r0 + rows] + gate1_ref[...] * (_rms(y) * gmix_ref[...])
        x1_parts.append(x1_g)
        h_parts.append((_rms(x1_g) * gpre_ref[...] * (1.0 + sc_ref[...]) + sh_ref[...]).astype(BF16))
    x1 = jnp.concatenate(x1_parts, axis=0)
    h = jnp.concatenate(h_parts, axis=0)
    u = jnp.maximum(_dot(h, w1_ref[...]), 0.0)
    z = _dot((u * u).astype(BF16), w2_ref[...])
    res = x1 + gate2_ref[...] * (_rms(z) * gpost_ref[...])
    if n_out == 1:
        out_refs[0][...] = res
    else:
        is_ctx = pl.program_id(0) < n_ctx_tiles

        @pl.when(is_ctx)
        def _():
            out_refs[0][...] = res

        @pl.when(jnp.logical_not(is_ctx))
        def _():
            out_refs[1][...] = res


def _mix_mlp_call(tok, m, x, mods, layer, w_out, w_out_idx, g_mix, g_pre, w1, w2, g_post, split_out=False):
    tm = tok.tile(512)
    d = tok.d
    row = lambda i: (i, 0)
    x_specs, x_args = tok.x_specs(x, tm)
    nct = tok.n_ctx // tm
    if split_out:
        out_specs = [pl.BlockSpec((tm, d), lambda i: (jnp.minimum(i, nct - 1), 0)),
                     pl.BlockSpec((tm, d), lambda i: (jnp.maximum(i - nct, 0), 0))]
        out_shape = [jax.ShapeDtypeStruct((tok.n_ctx, d), F32), jax.ShapeDtypeStruct((tok.n_lat, d), F32)]
    else:
        out_specs = [pl.BlockSpec((tm, d), row)]
        out_shape = [jax.ShapeDtypeStruct((tok.n, d), F32)]
    return pl.pallas_call(
        functools.partial(_mix_mlp_body, n_x=len(x_args), n_out=len(out_shape), n_ctx_tiles=nct),
        grid=(tok.n // tm,),
        in_specs=x_specs + [
            pl.BlockSpec((tm, m.shape[1]), row),
            _layer_block(w_out, w_out_idx),
            _full((1, d)),
            tok.mod_spec(layer, 2, tm),
            _full((1, d)),
            tok.mod_spec(layer, 3, tm),
            tok.mod_spec(layer, 4, tm),
            _full(w1.shape),
            _full(w2.shape),
            _full((1, d)),
            tok.mod_spec(layer, 5, tm),
        ],
        out_specs=out_specs,
        out_shape=out_shape,
        compiler_params=_cparams(("arbitrary",)),
        name="mix_mlp",
    )(*x_args, m, w_out, g_mix, mods, g_pre, mods, mods, w1, w2, g_post, mods)


HEAD_W = 2 * LANE
O_QLAT, O_CKV, O_KPE, O_KPES, O_COLS = 0, Q_LORA, Q_LORA + KV_LORA, Q_LORA + KV_LORA + LANE, Q_LORA + KV_LORA + 2 * LANE
QB_NOPE, QB_ROPE, QB_SWAP = 0, H_C * LANE, 2 * H_C * LANE


def _expand_kv(cb, kper, wkvb_ref, k_ref, v_ref):
    for hp in range(H_C // 2):
        nope2 = _dot(cb, wkvb_ref[:, hp * 2 * LANE:(hp + 1) * 2 * LANE])
        for j in range(2):
            hh = 2 * hp + j
            k_ref[:, hh * HEAD_W:hh * HEAD_W + LANE] = nope2[:, j * LANE:(j + 1) * LANE].astype(BF16)
            k_ref[:, hh * HEAD_W + LANE:(hh + 1) * HEAD_W] = kper
    v_ref[...] = _dot(cb, wkvb_ref[:, H_C * LANE:2 * H_C * LANE]).astype(BF16)


def _mla_proj_body(*refs, n_cast):
    x_ref, g_ref, sh_ref, sc_ref, win_ref, qn_ref, wqb_ref, kvn_ref, wkvb_ref, cc_ref, ss_ref = refs[:11]
    q_ref, k_ref, v_ref, ckv_ref, kpe_ref = refs[11 + n_cast:16 + n_cast]
    _side_cast(refs[11:11 + n_cast], refs[16 + n_cast:])
    h = (_rms(x_ref[...]) * g_ref[...] * (1.0 + sc_ref[...]) + sh_ref[...]).astype(BF16)
    cc = cc_ref[...]
    ss = ss_ref[...]
    qn = (_rms(_dot(h, win_ref[:, O_QLAT:O_QLAT + Q_LORA])) * qn_ref[...]).astype(BF16)
    ckvn = _rms(_dot(h, win_ref[:, O_CKV:O_CKV + KV_LORA])) * kvn_ref[...]
    kpe2 = _dot(h, win_ref[:, O_KPE:O_KPE + 2 * LANE])
    kpe = kpe2[:, 0:LANE]
    kper = (kpe * cc + kpe2[:, LANE:2 * LANE] * ss).astype(BF16)
    ckv_ref[...] = ckvn
    kpe_ref[...] = kpe
    for hp in range(H_C // 2):
        o = hp * 2 * LANE
        nope2 = _dot(qn, wqb_ref[:, QB_NOPE + o:QB_NOPE + o + 2 * LANE])
        rope2 = _dot(qn, wqb_ref[:, QB_ROPE + o:QB_ROPE + o + 2 * LANE])
        swap2 = _dot(qn, wqb_ref[:, QB_SWAP + o:QB_SWAP + o + 2 * LANE])
        for j in range(2):
            hh = 2 * hp + j
            sl = slice(j * LANE, (j + 1) * LANE)
            rot = rope2[:, sl] * cc + swap2[:, sl] * ss
            q_ref[:, hh * HEAD_W:hh * HEAD_W + LANE] = (nope2[:, sl] * ATTN_Q_SCALE).astype(BF16)
            q_ref[:, hh * HEAD_W + LANE:(hh + 1) * HEAD_W] = (rot * ATTN_Q_SCALE).astype(BF16)
    _expand_kv(ckvn.astype(BF16), kper, wkvb_ref, k_ref, v_ref)


def _mla_proj_call(tok, x, mods, layer, idx, g_pre, win, qn, wqb, kvn, wkvb, cc, ss, cast_stacks):
    tm = tok.tile(MLA_TILE)
    d = tok.d
    row = lambda i: (i, 0)
    outs = [
        jax.ShapeDtypeStruct((tok.n, H_C * HEAD_W), BF16),
        jax.ShapeDtypeStruct((tok.n, H_C * HEAD_W), BF16),
        jax.ShapeDtypeStruct((tok.n, H_C * V_HEAD_C), BF16),
        jax.ShapeDtypeStruct((tok.n, KV_LORA), F32),
        jax.ShapeDtypeStruct((tok.n, LANE), F32),
    ]
    cast_in, cast_out, cast_shape = _side_cast_specs(cast_stacks, layer, tok.n // tm)
    return pl.pallas_call(
        functools.partial(_mla_proj_body, n_cast=len(cast_stacks)),
        grid=(tok.n // tm,),
        in_specs=[
            pl.BlockSpec((tm, d), row),
            _full((1, d)),
            tok.mod_spec(layer, 0, tm),
            tok.mod_spec(layer, 1, tm),
            _layer_block(win, idx), _full(qn.shape), _layer_block(wqb, idx), _full(kvn.shape),
            _layer_block(wkvb, idx),
            tok.rope_spec(tm),
            tok.rope_spec(tm),
        ] + cast_in,
        out_specs=[pl.BlockSpec((tm, o.shape[1]), row) for o in outs] + cast_out,
        out_shape=outs + cast_shape,
        compiler_params=_cparams(("arbitrary",)),
        name="mla_proj",
    )(x, g_pre, mods, mods, win, qn, wqb, kvn, wkvb, cc, ss, *cast_stacks)


def _cache_expand_body(ckv_ref, kpe_ref, wkvb_ref, k_ref, v_ref):
    _expand_kv(ckv_ref[...].astype(BF16), kpe_ref[...].astype(BF16), wkvb_ref, k_ref, v_ref)


def _cache_expand_call(ckv, kpe_pad, wkvb, idx):
    n = ckv.shape[0]
    tm = 512
    while n % tm:
        tm //= 2
    row = lambda i: (i, 0)
    outs = [jax.ShapeDtypeStruct((n, H_C * HEAD_W), BF16), jax.ShapeDtypeStruct((n, H_C * V_HEAD_C), BF16)]
    return pl.pallas_call(
        _cache_expand_body,
        grid=(n // tm,),
        in_specs=[pl.BlockSpec((tm, KV_LORA), row), pl.BlockSpec((tm, LANE), row), _layer_block(wkvb, idx)],
        out_specs=[pl.BlockSpec((tm, o.shape[1]), row) for o in outs],
        out_shape=outs,
        compiler_params=_cparams(("arbitrary",)),
        name="cache_expand",
    )(ckv, kpe_pad, wkvb)


ATTN_TQ = 512
ATTN_TK = 512
ATTN_HEADS_PER_STEP = 2
ATTN_Q_SCALE = (QK_NOPE + QK_ROPE) ** -0.5 * float(np.log2(np.e))


def _softmax_pv(s, values):
    p = jnp.exp2(s - jnp.max(s, axis=-1, keepdims=True))
    den = jnp.sum(p, axis=-1, keepdims=True)
    p = p.astype(BF16)
    acc = functools.reduce(jnp.add, [_dot(p[:, k0:k0 + vb.shape[0]], vb) for vb, k0 in values])
    return (acc / den).astype(BF16)


def _attn_ctx_body(q_ref, k_ref, v_ref, o_ref):
    for hh in range(H_C):
        s = _dot_nt(q_ref[:, hh * HEAD_W:(hh + 1) * HEAD_W], k_ref[:, hh * HEAD_W:(hh + 1) * HEAD_W])
        o_ref[:, hh * V_HEAD_C:(hh + 1) * V_HEAD_C] = _softmax_pv(s, [(v_ref[:, hh * V_HEAD_C:(hh + 1) * V_HEAD_C], 0)])


def _lane_groups(x):
    return [x[:, g:g + LANE] for g in range(0, x.shape[1], LANE)]


def _attn_lat_body(q_ref, kc_ref, vc_ref, k_ref, v_ref, o_ref, sa_ref, sb_ref, ma_ref, mb_ref, vx_ref):
    past, n_self = kc_ref.shape[0], k_ref.shape[0]
    heads = vx_ref.shape[0]
    tq = sa_ref.shape[0]
    n = q_ref.shape[0] // tq
    tk = min(ATTN_TK, n_self)
    blocks = [(kc_ref, 0, past, 0)] + [(k_ref, k0, tk, past + k0) for k0 in range(0, n_self, tk)]

    for h in range(heads):
        vx_ref[h, 0:past, 0:V_HEAD_C] = vc_ref[:, h * V_HEAD_C:(h + 1) * V_HEAD_C]
        vx_ref[h, past:past + n_self, 0:V_HEAD_C] = v_ref[:, h * V_HEAD_C:(h + 1) * V_HEAD_C]
        vx_ref[h, :, V_HEAD_C:2 * V_HEAD_C] = jnp.ones((past + n_self, V_HEAD_C), BF16)

    def scores(h, i, s_ref, m_ref):
        q = q_ref[pl.ds(pl.multiple_of(i * tq, tq), tq), h * HEAD_W:(h + 1) * HEAD_W]
        mx = None
        for kk_ref, k0, size, col in blocks:
            s = _dot_nt(q, kk_ref[k0:k0 + size, h * HEAD_W:(h + 1) * HEAD_W])
            s_ref[:, col:col + size] = s
            mx = functools.reduce(jnp.maximum, _lane_groups(s) + ([] if mx is None else [mx]))
        m_ref[...] = mx

    def finish(h, i, s_ref, m_ref):
        m = jnp.max(m_ref[...], axis=-1, keepdims=True)
        acc = None
        for _, _, size, col in blocks:
            p = jnp.exp2(s_ref[:, col:col + size] - m).astype(BF16)
            pv = _dot(p, vx_ref[h, col:col + size, :])
            acc = pv if acc is None else acc + pv
        out = acc[:, 0:V_HEAD_C] / acc[:, V_HEAD_C:2 * V_HEAD_C]
        o_ref[pl.ds(pl.multiple_of(i * tq, tq), tq), h * V_HEAD_C:(h + 1) * V_HEAD_C] = out.astype(BF16)

    scores(0, 0, sa_ref, ma_ref)
    for h in range(heads):

        def two_tiles(j, carry, h=h):
            i = 2 * j
            scores(h, i + 1, sb_ref, mb_ref)
            finish(h, i, sa_ref, ma_ref)
            scores(h, i + 2, sa_ref, ma_ref)
            finish(h, i + 1, sb_ref, mb_ref)
            return carry

        lax.fori_loop(0, n // 2 - 1, two_tiles, 0)
        scores(h, n - 1, sb_ref, mb_ref)
        finish(h, n - 2, sa_ref, ma_ref)
        if h + 1 < heads:
            scores(h + 1, 0, sa_ref, ma_ref)
        finish(h, n - 1, sb_ref, mb_ref)


def _attn_ctx_call(tok, q, k, v):
    seq = lambda b: (b, 0)
    return pl.pallas_call(
        _attn_ctx_body,
        grid=(tok.nb_ctx,),
        in_specs=[pl.BlockSpec((tok.l_ctx, H_C * HEAD_W), seq),
                  pl.BlockSpec((tok.l_ctx, H_C * HEAD_W), seq),
                  pl.BlockSpec((tok.l_ctx, H_C * V_HEAD_C), seq)],
        out_specs=pl.BlockSpec((tok.l_ctx, H_C * V_HEAD_C), seq),
        out_shape=jax.ShapeDtypeStruct(v.shape, v.dtype),
        input_output_aliases={2: 0},
        compiler_params=_cparams(("arbitrary",)),
        name="attn_ctx",
    )(q, k, v)


def _attn_lat_call(tok, q, k, v, k_cache, v_cache):
    blk0 = tok.n_ctx // tok.l_lat
    past = k_cache.shape[0] // tok.nb_lat
    tq = min(ATTN_TQ, tok.l_lat // 2)
    hg = ATTN_HEADS_PER_STEP
    seq = lambda b, hh: (blk0 + b, hh)
    cache = lambda b, hh: (b, hh)
    return pl.pallas_call(
        _attn_lat_body,
        grid=(tok.nb_lat, H_C // hg),
        in_specs=[pl.BlockSpec((tok.l_lat, hg * HEAD_W), seq),
                  pl.BlockSpec((past, hg * HEAD_W), cache),
                  pl.BlockSpec((past, hg * V_HEAD_C), cache),
                  pl.BlockSpec((tok.l_lat, hg * HEAD_W), seq),
                  pl.BlockSpec((tok.l_lat, hg * V_HEAD_C), seq)],
        out_specs=pl.BlockSpec((tok.l_lat, hg * V_HEAD_C), seq),
        out_shape=jax.ShapeDtypeStruct(v.shape, v.dtype),
        input_output_aliases={4: 0},
        scratch_shapes=([pltpu.VMEM((tq, past + tok.l_lat), F32)] * 2 + [pltpu.VMEM((tq, LANE), F32)] * 2
                        + [pltpu.VMEM((hg, past + tok.l_lat, 2 * V_HEAD_C), BF16)]),
        compiler_params=_cparams(("arbitrary", "arbitrary")),
        name="attn_lat",
    )(q, k_cache, v_cache, k, v)


def _swap_halves(w, head_dim):
    lead = w.shape[:-1]
    halves = w.reshape(lead + (w.shape[-1] // head_dim, 2, head_dim // 2))
    return halves[..., ::-1, :].reshape(w.shape)


def _pad_cols(w, width):
    return jnp.pad(w, ((0, 0),) * (w.ndim - 1) + ((0, width - w.shape[-1]),))


def _even_weights(w_in, w_gk2, b_gk2):
    n = w_in.shape[0]
    sizes = (A_QK, A_QK, A_V, A_V, 2 * GATE_RANK, A_QK, A_QK, A_V, A_V)
    qa, ka, va, ga, gk, qb, kb, vb, gb = jnp.split(w_in, np.cumsum(sizes)[:-1].tolist(), axis=-1)
    w = jnp.concatenate([qa, ka, va, ga, qb, kb, vb, gb, _pad_cols(gk, LANE),
                         _swap_halves(qb, DK), _swap_halves(kb, DK)], axis=-1).astype(BF16)
    wg = jnp.zeros((n, LANE, 2 * A_QK), F32)
    wg = wg.at[:, 0:GATE_RANK, 0:A_QK].set(w_gk2[:, 0]).at[:, GATE_RANK:2 * GATE_RANK, A_QK:2 * A_QK].set(w_gk2[:, 1])
    bg = b_gk2.reshape(n, 1, 2 * A_QK)
    return w, wg.astype(BF16), bg


def _odd_weights(w_in, w_q_b, w_kv_b):
    n = w_in.shape[0]
    q_lat, ckv, kpe = w_in[..., :Q_LORA], w_in[..., Q_LORA:Q_LORA + KV_LORA], w_in[..., Q_LORA + KV_LORA:]
    win = jnp.concatenate([q_lat, ckv, _pad_cols(kpe, LANE), _pad_cols(_swap_halves(kpe, QK_ROPE), LANE)],
                          axis=-1).astype(BF16)
    wq = w_q_b.reshape(n, Q_LORA, H_C, QK_NOPE + QK_ROPE)
    nope = wq[..., :QK_NOPE].reshape(n, Q_LORA, H_C * QK_NOPE)
    rope = wq[..., QK_NOPE:]
    pad = lambda r: _pad_cols(r, LANE).reshape(n, Q_LORA, H_C * LANE)
    wqb = jnp.concatenate([nope, pad(rope), pad(_swap_halves(rope, QK_ROPE))], axis=-1).astype(BF16)
    wkv = w_kv_b.reshape(n, KV_LORA, H_C, QK_NOPE + V_HEAD_C)
    wkvb = jnp.concatenate([wkv[..., :QK_NOPE].reshape(n, KV_LORA, H_C * QK_NOPE),
                            wkv[..., QK_NOPE:].reshape(n, KV_LORA, H_C * V_HEAD_C)], axis=-1).astype(BF16)
    return win, wqb, wkvb


def _rope_tables(tok, tm):
    rows = tok.l_lat // GRID_W
    row = jnp.repeat(jnp.arange(rows), GRID_W).astype(F32)
    col = jnp.tile(jnp.arange(GRID_W), rows).astype(F32)
    n_freq = QK_ROPE // 4
    inv = ROPE_BASE ** (-jnp.arange(n_freq, dtype=F32) / n_freq)
    ang = jnp.concatenate([row[:, None] * inv, col[:, None] * inv], axis=-1)
    cos, sin = jnp.cos(ang), jnp.sin(ang)
    cc = jnp.tile(jnp.concatenate([cos, cos], axis=-1), (1, LANE // QK_ROPE))
    ss = jnp.tile(jnp.concatenate([-sin, sin], axis=-1), (1, LANE // QK_ROPE))
    cc = jnp.concatenate([jnp.ones((tm, LANE), F32), cc], axis=0)
    ss = jnp.concatenate([jnp.zeros((tm, LANE), F32), ss], axis=0)
    return cc, ss


def _states_to_kernel(s):
    nb, _, heads = s.shape[:3]
    st = jnp.swapaxes(s.reshape(nb, 2, heads // 2, 2, DK, DV), -1, -2)
    z = jnp.zeros_like(st[:, :, :, 0])
    rows = [jnp.concatenate([st[:, :, :, 0], z], axis=-1), jnp.concatenate([z, st[:, :, :, 1]], axis=-1)]
    return jnp.concatenate(rows, axis=-2)


def _states_from_kernel(st):
    nb, _, pairs = st.shape[:3]
    heads = jnp.stack([st[:, :, :, :DV, :DK], st[:, :, :, DV:, DK:]], axis=3)
    return jnp.swapaxes(heads, -1, -2).reshape(nb, 2, 2 * pairs, DK, DV)


def kernel(x_prompt, x_sample, cache_ckv, cache_kpe, state_gla, state_ret, c, c_ctx, w_ada, b_ada, norm_mix_pre, norm_mix_post, norm_mlp_pre, norm_mlp_post, w_in_even, w_gk2, b_gk2, gla_norm, ret_decay, w_out_even, w_in_odd, q_a_norm, w_q_b, kv_a_norm, w_kv_b, w_out_odd, w_mlp1, w_mlp2):
    nb_ctx, l_ctx, d = x_prompt.shape
    nb_lat, l_lat, _ = x_sample.shape
    depth = w_ada.shape[0]
    tok = _Tokens(nb_ctx, l_ctx, nb_lat, l_lat, d)
    assert nb_lat < MOD_ROWS and tok.n_ctx % l_lat == 0 and l_ctx % CHUNK == 0 and l_lat % (2 * CHUNK) == 0

    cond = jnp.concatenate([c, c_ctx[None, :], jnp.zeros((MOD_ROWS - nb_lat - 1, d), F32)], axis=0)
    mods = _ada_call(cond, w_ada, b_ada).reshape(depth, MOD_ROWS, 6, 1, d)
    rope_even = _rope_tables(tok, tok.tile(EVEN_TILE))
    rope_odd = _rope_tables(tok, tok.tile(MLA_TILE))
    x = (x_prompt.reshape(tok.n_ctx, d), x_sample.reshape(tok.n_lat, d))
    vec = lambda a: a.reshape(1, -1)
    w_out_even16, w_out_odd16 = w_out_even.astype(BF16), w_out_odd.astype(BF16)
    mlp_w = (w_mlp1, w_mlp2)
    even_w = _even_weights(w_in_even, w_gk2, b_gk2) + (jnp.repeat(ret_decay, DK, axis=-1),)
    odd_w = _odd_weights(w_in_odd, w_q_b, w_kv_b)

    new_ckv, new_kpe, new_gla, new_ret = [], [], [], []
    for l in range(depth):
        i = l // 2
        if l % 2 == 0:
            q, k, lf, lb, gt, v, w1, w2 = _even_proj_call(tok, x, mods, l, i, vec(norm_mix_pre[l]), *even_w,
                                                          *rope_even, mlp_w)
            gain = jnp.concatenate([jnp.tile(gla_norm[i], N_HEAD_SCAN), jnp.ones((A_V,), F32)]).reshape(1, 2 * A_V)
            s0_lat = _states_to_kernel(jnp.concatenate([state_gla[:, i], state_ret[:, i]], axis=2))
            m_ctx, s_fin = _scan_call(tok, q, k, lf, lb, v, gt, gain)
            (m,) = _scan_call(tok, q, k, lf, lb, m_ctx, gt, gain, s0=s0_lat)
            s_fin = _states_from_kernel(s_fin)
            new_gla.append(s_fin[:, :, :N_HEAD_SCAN])
            new_ret.append(s_fin[:, :, N_HEAD_SCAN:])
            w_out = w_out_even16
        else:
            win, wqb, wkvb = odd_w
            q, k, v, ckv, kpe, w1, w2 = _mla_proj_call(tok, x, mods, l, i, vec(norm_mix_pre[l]), win, vec(q_a_norm[i]),
                                                       wqb, vec(kv_a_norm[i]), wkvb, *rope_odd, mlp_w)
            past = cache_ckv.shape[2]
            kpe_pad = jnp.pad(cache_kpe[:, i].reshape(nb_lat * past, QK_ROPE), ((0, 0), (0, LANE - QK_ROPE)))
            k_c, v_c = _cache_expand_call(cache_ckv[:, i].reshape(nb_lat * past, KV_LORA), kpe_pad, wkvb, i)
            m = _attn_lat_call(tok, q, k, _attn_ctx_call(tok, q, k, v), k_c, v_c)
            new_ckv.append(ckv[:tok.n_ctx].reshape(nb_ctx, l_ctx, KV_LORA))
            new_kpe.append(kpe[:tok.n_ctx, :QK_ROPE].reshape(nb_ctx, l_ctx, QK_ROPE))
            w_out = w_out_odd16
        x = _mix_mlp_call(tok, m, x, mods, l, w_out, i, vec(norm_mix_post[l]), vec(norm_mlp_pre[l]), w1, w2,
                          vec(norm_mlp_post[l]), split_out=(l == depth - 1))
        x = x[0] if len(x) == 1 else tuple(x)

    return (x[0].reshape(nb_ctx, l_ctx, d), x[1].reshape(nb_lat, l_lat, d),
            jnp.stack(new_ckv, axis=1), jnp.stack(new_kpe, axis=1),
            jnp.stack(new_gla, axis=1), jnp.stack(new_ret, axis=1))
```

```python
import functools

import numpy as np
import jax
import jax.numpy as jnp
from jax import lax
from jax.experimental import pallas as pl
from jax.experimental.pallas import tpu as pltpu

F32 = jnp.float32
BF16 = jnp.bfloat16

EPS = 1e-6
ROPE_BASE = 10000.0
GRID_W = 64
CHUNK = 64
GATE_RANK = 16
GATE_NORM = 16.0
N_HEAD_SCAN = 4
DK = 64
DV = 128
H_C = 8
Q_LORA = 256
KV_LORA = 256
QK_NOPE = 128
QK_ROPE = 64
V_HEAD_C = 128
LANE = 128
MOD_ROWS = 16
EVEN_TILE = 512
MLA_TILE = 1024

VMEM_LIMIT = 56 * 1024 * 1024


def _cparams(sem):
    return pltpu.CompilerParams(dimension_semantics=sem, vmem_limit_bytes=VMEM_LIMIT)


def _dot(a, b):
    return jnp.dot(a, b, preferred_element_type=F32)


def _dot_nt(a, b):
    return lax.dot_general(a, b, (((1,), (1,)), ((), ())), preferred_element_type=F32)


def _dot_tn(a, b):
    return lax.dot_general(a, b, (((0,), (0,)), ((), ())), preferred_element_type=F32)


def _rms(x):
    return x * lax.rsqrt(jnp.mean(x * x, axis=-1, keepdims=True) + EPS)


def _silu(x):
    return x * jax.nn.sigmoid(x)


def _full(shape):
    n = len(shape)
    return pl.BlockSpec(shape, lambda *_: (0,) * n, pipeline_mode=pl.Buffered(1))


def _layer_block(w, idx):
    tail = (0,) * (w.ndim - 1)
    return pl.BlockSpec((None,) + w.shape[1:], lambda *_: (idx,) + tail, pipeline_mode=pl.Buffered(1))


def _side_cast_specs(items, steps):
    nb = 1 << (steps.bit_length() - 1)
    block = lambda i: jnp.minimum(i, nb - 1)
    ins = [pl.BlockSpec((None, w.shape[1] // nb, w.shape[2]), lambda i, layer=layer: (layer, block(i), 0))
           for w, layer in items]
    outs = [pl.BlockSpec((w.shape[1] // nb, w.shape[2]), lambda i: (block(i), 0)) for w, _ in items]
    shapes = [jax.ShapeDtypeStruct(w.shape[1:], BF16) for w, _ in items]
    return ins, outs, shapes


def _side_cast(in_refs, out_refs):
    for src, dst in zip(in_refs, out_refs):
        dst[...] = src[...].astype(BF16)


def _ada_body(cond_ref, w_ref, b_ref, o_ref):
    s = _silu(cond_ref[...]).astype(BF16)
    o_ref[...] = _dot(s, w_ref[...].astype(BF16)) + b_ref[...]


def _ada_call(cond, w_ada, b_ada):
    depth, d, n = w_ada.shape
    tn = 1536
    return pl.pallas_call(
        _ada_body,
        grid=(depth, n // tn),
        in_specs=[
            pl.BlockSpec((MOD_ROWS, d), lambda l, j: (0, 0)),
            pl.BlockSpec((None, d, tn), lambda l, j: (l, 0, j)),
            pl.BlockSpec((None, 1, tn), lambda l, j: (l, 0, j)),
        ],
        out_specs=pl.BlockSpec((None, MOD_ROWS, tn), lambda l, j: (l, 0, j)),
        out_shape=jax.ShapeDtypeStruct((depth, MOD_ROWS, n), F32),
        compiler_params=_cparams(("arbitrary", "arbitrary")),
        name="ada_mod",
    )(cond, w_ada, b_ada.reshape(depth, 1, n))


class _Tokens:
    def __init__(self, nb_ctx, l_ctx, nb_lat, l_lat, d):
        self.nb_ctx, self.l_ctx, self.nb_lat, self.l_lat, self.d = nb_ctx, l_ctx, nb_lat, l_lat, d
        self.n_ctx = nb_ctx * l_ctx
        self.n_lat = nb_lat * l_lat
        self.n = self.n_ctx + self.n_lat
        self.ctx_row = nb_lat

    def tile(self, want):
        t = want
        while self.n_ctx % t or self.l_lat % t:
            t //= 2
        return t

    def mod_spec(self, layer, chunk, tm):
        n_ctx, l_lat, ctx_row = self.n_ctx, self.l_lat, self.ctx_row

        def idx(i, *_):
            start = i * tm
            row = jnp.where(start < n_ctx, ctx_row, (start - n_ctx) // l_lat)
            return (layer, row, chunk, 0, 0)

        return pl.BlockSpec((None, None, None, 1, self.d), idx)

    def x_specs(self, x, tm):
        if not isinstance(x, tuple):
            return [pl.BlockSpec((tm, self.d), lambda i: (i, 0))], [x]
        nct = self.n_ctx // tm
        return [pl.BlockSpec((tm, self.d), lambda i: (jnp.minimum(i, nct - 1), 0)),
                pl.BlockSpec((tm, self.d), lambda i: (jnp.maximum(i - nct, 0), 0))], list(x)

    def rope_spec(self, tm):
        n_ctx, l_lat = self.n_ctx, self.l_lat

        def idx(i):
            start = i * tm
            return (jnp.where(start < n_ctx, 0, 1 + ((start - n_ctx) % l_lat) // tm), 0)

        return pl.BlockSpec((tm, LANE), idx)


def _read_x(x_refs, n_ctx_tiles):
    if len(x_refs) == 1:
        return x_refs[0][...]
    return jnp.where(pl.program_id(0) < n_ctx_tiles, x_refs[0][...], x_refs[1][...])


A_QK = N_HEAD_SCAN * DK
A_V = N_HEAD_SCAN * DV
E_QA, E_KA, E_VA, E_GA = 0, A_QK, 2 * A_QK, 2 * A_QK + A_V
E_QB = E_GA + A_V
E_KB = E_QB + A_QK
E_VB = E_KB + A_QK
E_GB = E_VB + A_V
E_GK = E_GB + A_V
E_QBS = E_GK + LANE
E_KBS = E_QBS + A_QK
E_COLS = E_KBS + A_QK


def _log_sigmoid(x):
    return jnp.minimum(x, 0.0) - jnp.log1p(jnp.exp(-jnp.abs(x)))


def _even_proj_body(*refs, n_x, n_ctx_tiles, n_cast):
    x = _read_x(refs[:n_x], n_ctx_tiles)
    g_ref, sh_ref, sc_ref, w_ref, wg_ref, bg_ref, dec_ref, cc_ref, ss_ref = refs[n_x:n_x + 9]
    q_ref, k_ref, lf_ref, lb_ref, gt_ref, v_ref = refs[n_x + 9 + n_cast:n_x + 15 + n_cast]
    _side_cast(refs[n_x + 9:n_x + 9 + n_cast], refs[n_x + 15 + n_cast:])
    tm = x.shape[0]
    h = (_rms(x) * g_ref[...] * (1.0 + sc_ref[...]) + sh_ref[...]).astype(BF16)

    def proj(start, width):
        return _dot(h, w_ref[:, start:start + width])

    cc = cc_ref[...]
    ss = ss_ref[...]
    scale = DK ** -0.5
    q_ref[:, 0:A_QK] = proj(E_QA, A_QK) * scale
    k_ref[:, 0:A_QK] = proj(E_KA, A_QK)
    qb, qbs, kb, kbs = proj(E_QB, A_QK), proj(E_QBS, A_QK), proj(E_KB, A_QK), proj(E_KBS, A_QK)
    for j in range(A_QK // LANE):
        sl = slice(j * LANE, (j + 1) * LANE)
        o = A_QK + j * LANE
        q_ref[:, o:o + LANE] = qb[:, sl] * cc + qbs[:, sl] * ss
        k_ref[:, o:o + LANE] = (kb[:, sl] * cc + kbs[:, sl] * ss) * scale
    v_ref[:, 0:A_V] = proj(E_VA, A_V).astype(BF16)
    v_ref[:, A_V:2 * A_V] = proj(E_VB, A_V).astype(BF16)
    gt_ref[:, 0:A_V] = proj(E_GA, A_V)
    gt_ref[:, A_V:2 * A_V] = proj(E_GB, A_V)
    gk = proj(E_GK, LANE).astype(BF16)
    la = _log_sigmoid(_dot(gk, wg_ref[...]) + bg_ref[...]) * (1.0 / GATE_NORM)
    lf_ref[:, 0:A_QK] = la[:, 0:A_QK]
    lb_ref[:, 0:A_QK] = la[:, A_QK:2 * A_QK]
    log_g = -jnp.exp(dec_ref[...])
    lf_ref[:, A_QK:2 * A_QK] = jnp.broadcast_to(log_g[0:1, :], (tm, A_QK))
    lb_ref[:, A_QK:2 * A_QK] = jnp.broadcast_to(log_g[1:2, :], (tm, A_QK))


def _even_proj_call(tok, x, mods, layer, idx, g_pre, w, wg, bg, dec, cc, ss, cast_items):
    tm = tok.tile(EVEN_TILE)
    d = tok.d
    row = lambda i: (i, 0)
    outs = [
        jax.ShapeDtypeStruct((tok.n, 2 * A_QK), F32),
        jax.ShapeDtypeStruct((tok.n, 2 * A_QK), F32),
        jax.ShapeDtypeStruct((tok.n, 2 * A_QK), F32),
        jax.ShapeDtypeStruct((tok.n, 2 * A_QK), F32),
        jax.ShapeDtypeStruct((tok.n, 2 * A_V), F32),
        jax.ShapeDtypeStruct((tok.n, 2 * A_V), BF16),
    ]
    x_specs, x_args = tok.x_specs(x, tm)
    cast_in, cast_out, cast_shape = _side_cast_specs(cast_items, tok.n // tm)
    return pl.pallas_call(
        functools.partial(_even_proj_body, n_x=len(x_args), n_ctx_tiles=tok.n_ctx // tm, n_cast=len(cast_items)),
        grid=(tok.n // tm,),
        in_specs=x_specs + [
            _full((1, d)),
            tok.mod_spec(layer, 0, tm),
            tok.mod_spec(layer, 1, tm),
            _layer_block(w, idx), _layer_block(wg, idx), _layer_block(bg, idx), _layer_block(dec, idx),
            tok.rope_spec(tm),
            tok.rope_spec(tm),
        ] + cast_in,
        out_specs=[pl.BlockSpec((tm, o.shape[1]), row) for o in outs] + cast_out,
        out_shape=outs + cast_shape,
        compiler_params=_cparams(("arbitrary",)),
        name="even_proj",
    )(*x_args, g_pre, mods, mods, w, wg, bg, dec, cc, ss, *[stack for stack, _ in cast_items])


SCAN_GROUP = 8


def _chunk_cumsum(x):
    row = lax.broadcasted_iota(jnp.int32, x.shape, 0) % CHUNK
    s = 1
    while s < CHUNK:
        x = x + jnp.where(row >= s, pltpu.roll(x, s, axis=0), 0.0)
        s *= 2
    return x


def _scan_body(q_ref, k_ref, lf_ref, lb_ref, v_ref, gt_ref, gain_ref, *rest, seq_len, group, context):
    if context:
        m_ref, sfin_ref, st_ref, o_acc = rest
    else:
        s0_ref, m_ref, st_ref, o_acc = rest
    C = CHUNK
    blk = group * C
    nblk = seq_len // blk
    pair_w = 2 * DK
    head0 = lax.broadcasted_iota(jnp.int32, (blk, pair_w), 1) < DK
    t_in = lax.broadcasted_iota(jnp.int32, (C, pair_w), 0)
    j_in = lax.broadcasted_iota(jnp.int32, (C, pair_w), 1) % DK
    keep_fwd = t_in >= j_in
    keep_bwd = t_in <= j_in
    on_diag = ((lax.broadcasted_iota(jnp.int32, (2 * DV, pair_w), 0) < DV)
               == (lax.broadcasted_iota(jnp.int32, (2 * DV, pair_w), 1) < DK))
    zeros_v = jnp.zeros((C, DV), BF16)

    if context:
        st_ref[...] = jnp.zeros(st_ref.shape, F32)
    else:
        st_ref[...] = s0_ref[...]

    def one_direction(r0, d, log_ref, keep, reverse):
        g = log_ref[pl.ds(r0, blk), :]
        b = _chunk_cumsum(g)
        tots = [b[C * j + C - 1:C * j + C, :] for j in range(group)]
        totb = jnp.concatenate([jnp.broadcast_to(t, (C, pair_w)) for t in tots], axis=0)
        if reverse:
            b = totb - b + g
        q = q_ref[pl.ds(r0, blk), :]
        k = k_ref[pl.ds(r0, blk), :]
        vblk = v_ref[pl.ds(r0, blk), :]
        q_dec = (q * jnp.exp(b)).astype(BF16)
        k_inv = k * jnp.exp(-b)
        k_up = (k * jnp.exp(totb - b)).astype(BF16)
        k_inv0 = jnp.where(head0, k_inv, 0.0).astype(BF16)
        k_inv1 = jnp.where(head0, 0.0, k_inv).astype(BF16)
        st = st_ref[d]
        outs = [None] * group
        for j in (reversed(range(group)) if reverse else range(group)):
            sl = slice(C * j, C * (j + 1))
            k_bd = jnp.concatenate([k_inv0[sl], k_inv1[sl]], axis=0)
            a = jnp.where(keep, _dot_nt(q_dec[sl], k_bd), 0.0).astype(BF16)
            vc = vblk[sl]
            v_bd = jnp.concatenate([jnp.concatenate([vc[:, :DV], zeros_v], axis=1),
                                    jnp.concatenate([zeros_v, vc[:, DV:]], axis=1)], axis=0)
            outs[j] = _dot_nt(q_dec[sl], st.astype(BF16)) + _dot(a, v_bd)
            st = st * jnp.exp(tots[j]) + jnp.where(on_diag, _dot_tn(vc, k_up[sl]), 0.0)
        st_ref[d] = st
        return jnp.concatenate(outs, axis=0)

    def emit(r0, o):
        gate = _silu(gt_ref[pl.ds(r0, blk), :])
        gain = gain_ref[...]
        for hh in range(2):
            sl = slice(hh * DV, (hh + 1) * DV)
            m_ref[pl.ds(r0, blk), sl] = (_rms(o[:, sl]) * gain[:, sl] * gate[:, sl]).astype(BF16)

    def body(c, carry, second_half):
        rf = pl.multiple_of(c * blk, blk)
        rb = pl.multiple_of((nblk - 1 - c) * blk, blk)
        o_f = one_direction(rf, 0, lf_ref, keep_fwd, False)
        o_b = one_direction(rb, 1, lb_ref, keep_bwd, True)
        if second_half:
            emit(rf, o_acc[pl.ds(rf, blk), :] + o_f)
            emit(rb, o_acc[pl.ds(rb, blk), :] + o_b)
        else:
            o_acc[pl.ds(rf, blk), :] = o_f
            o_acc[pl.ds(rb, blk), :] = o_b
        return carry

    if nblk == 1:
        emit(0, one_direction(0, 0, lf_ref, keep_fwd, False) + one_direction(0, 1, lb_ref, keep_bwd, True))
    else:
        lax.fori_loop(0, nblk // 2, functools.partial(body, second_half=False), 0)
        lax.fori_loop(nblk // 2, nblk, functools.partial(body, second_half=True), 0)

    if context:
        sfin_ref[...] = st_ref[...]


def _scan_call(tok, q, k, lf, lb, v, gt, gain, s0=None):
    context = s0 is None
    if context:
        nb, seq_len, blk0 = tok.nb_ctx, tok.l_ctx, 0
    else:
        nb, seq_len, blk0 = tok.nb_lat, tok.l_lat, tok.n_ctx // tok.l_lat
    pairs = q.shape[1] // (2 * DK)
    group = min(SCAN_GROUP, seq_len // CHUNK)
    seq = lambda b, p: (blk0 + b, p)
    st_spec = pl.BlockSpec((None, 2, None, 2 * DV, 2 * DK), lambda b, p: (b, 0, p, 0, 0))
    in_specs = [
        pl.BlockSpec((seq_len, 2 * DK), seq),
        pl.BlockSpec((seq_len, 2 * DK), seq),
        pl.BlockSpec((seq_len, 2 * DK), seq),
        pl.BlockSpec((seq_len, 2 * DK), seq),
        pl.BlockSpec((seq_len, 2 * DV), seq),
        pl.BlockSpec((seq_len, 2 * DV), seq),
        pl.BlockSpec((1, 2 * DV), lambda b, p: (0, p)),
    ]
    args = [q, k, lf, lb, v, gt, gain]
    out_shape = [jax.ShapeDtypeStruct(v.shape, v.dtype)]
    out_specs = [pl.BlockSpec((seq_len, 2 * DV), seq)]
    if context:
        out_shape.append(jax.ShapeDtypeStruct((nb, 2, pairs, 2 * DV, 2 * DK), F32))
        out_specs.append(st_spec)
    else:
        in_specs.append(st_spec)
        args.append(s0)
    return pl.pallas_call(
        functools.partial(_scan_body, seq_len=seq_len, group=group, context=context),
        grid=(nb, pairs),
        in_specs=in_specs,
        out_specs=out_specs,
        out_shape=out_shape,
        input_output_aliases={4: 0},
        scratch_shapes=[pltpu.VMEM((2, 2 * DV, 2 * DK), F32), pltpu.VMEM((seq_len, 2 * DV), F32)],
        compiler_params=_cparams(("arbitrary", "arbitrary")),
        name="scan_ctx" if context else "scan_lat",
    )(*args)


MIX_ROW_GROUPS = 4


def _mix_mlp_body(*refs, n_x, n_out, n_ctx_tiles):
    x = _read_x(refs[:n_x], n_ctx_tiles)
    (m_ref, wo_ref, gmix_ref, gate1_ref, gpre_ref, sh_ref, sc_ref, w1_ref, w2_ref,
     gpost_ref, gate2_ref) = refs[n_x:len(refs) - n_out]
    out_refs = refs[len(refs) - n_out:]
    rows = x.shape[0] // MIX_ROW_GROUPS
    x1_parts, h_parts = [], []
    for r0 in range(0, x.shape[0], rows):
        y = _dot(m_ref[r0:r0 + rows, :], wo_ref[...])
        x1_g = x[r0:r0 + rows] + gate1_ref[...] * (_rms(y) * gmix_ref[...])
        x1_parts.append(x1_g)
        h_parts.append((_rms(x1_g) * gpre_ref[...] * (1.0 + sc_ref[...]) + sh_ref[...]).astype(BF16))
    x1 = jnp.concatenate(x1_parts, axis=0)
    h = jnp.concatenate(h_parts, axis=0)
    u = jnp.maximum(_dot(h, w1_ref[...]), 0.0)
    z = _dot((u * u).astype(BF16), w2_ref[...])
    res = x1 + gate2_ref[...] * (_rms(z) * gpost_ref[...])
    if n_out == 1:
        out_refs[0][...] = res
    else:
        is_ctx = pl.program_id(0) < n_ctx_tiles

        @pl.when(is_ctx)
        def _():
            out_refs[0][...] = res

        @pl.when(jnp.logical_not(is_ctx))
        def _():
            out_refs[1][...] = res


def _mix_mlp_call(tok, m, x, mods, layer, w_out, g_mix, g_pre, w1, w2, g_post, split_out=False):
    tm = tok.tile(512)
    d = tok.d
    row = lambda i: (i, 0)
    x_specs, x_args = tok.x_specs(x, tm)
    nct = tok.n_ctx // tm
    if split_out:
        out_specs = [pl.BlockSpec((tm, d), lambda i: (jnp.minimum(i, nct - 1), 0)),
                     pl.BlockSpec((tm, d), lambda i: (jnp.maximum(i - nct, 0), 0))]
        out_shape = [jax.ShapeDtypeStruct((tok.n_ctx, d), F32), jax.ShapeDtypeStruct((tok.n_lat, d), F32)]
    else:
        out_specs = [pl.BlockSpec((tm, d), row)]
        out_shape = [jax.ShapeDtypeStruct((tok.n, d), F32)]
    return pl.pallas_call(
        functools.partial(_mix_mlp_body, n_x=len(x_args), n_out=len(out_shape), n_ctx_tiles=nct),
        grid=(tok.n // tm,),
        in_specs=x_specs + [
            pl.BlockSpec((tm, m.shape[1]), row),
            _full(w_out.shape),
            _full((1, d)),
            tok.mod_spec(layer, 2, tm),
            _full((1, d)),
            tok.mod_spec(layer, 3, tm),
            tok.mod_spec(layer, 4, tm),
            _full(w1.shape),
            _full(w2.shape),
            _full((1, d)),
            tok.mod_spec(layer, 5, tm),
        ],
        out_specs=out_specs,
        out_shape=out_shape,
        compiler_params=_cparams(("arbitrary",)),
        name="mix_mlp",
    )(*x_args, m, w_out, g_mix, mods, g_pre, mods, mods, w1, w2, g_post, mods)


HEAD_W = 2 * LANE
O_QLAT, O_CKV, O_KPE, O_KPES, O_COLS = 0, Q_LORA, Q_LORA + KV_LORA, Q_LORA + KV_LORA + LANE, Q_LORA + KV_LORA + 2 * LANE
QB_NOPE, QB_ROPE, QB_SWAP = 0, H_C * LANE, 2 * H_C * LANE


def _expand_kv(cb, kper, wkvb_ref, k_ref, v_ref):
    for hp in range(H_C // 2):
        nope2 = _dot(cb, wkvb_ref[:, hp * 2 * LANE:(hp + 1) * 2 * LANE])
        for j in range(2):
            hh = 2 * hp + j
            k_ref[:, hh * HEAD_W:hh * HEAD_W + LANE] = nope2[:, j * LANE:(j + 1) * LANE].astype(BF16)
            k_ref[:, hh * HEAD_W + LANE:(hh + 1) * HEAD_W] = kper
    v_ref[...] = _dot(cb, wkvb_ref[:, H_C * LANE:2 * H_C * LANE]).astype(BF16)


def _mla_proj_body(x_ref, g_ref, sh_ref, sc_ref, win_ref, qn_ref, wqb_ref, kvn_ref, wkvb_ref, cc_ref, ss_ref,
                   q_ref, k_ref, v_ref, ckv_ref, kpe_ref):
    h =(_rms(x_ref[...]) * g_ref[...] * (1.0 + sc_ref[...]) + sh_ref[...]).astype(BF16)
    cc = cc_ref[...]
    ss = ss_ref[...]
    qn = (_rms(_dot(h, win_ref[:, O_QLAT:O_QLAT + Q_LORA])) * qn_ref[...]).astype(BF16)
    ckvn = _rms(_dot(h, win_ref[:, O_CKV:O_CKV + KV_LORA])) * kvn_ref[...]
    kpe2 = _dot(h, win_ref[:, O_KPE:O_KPE + 2 * LANE])
    kpe = kpe2[:, 0:LANE]
    kper = (kpe * cc + kpe2[:, LANE:2 * LANE] * ss).astype(BF16)
    ckv_ref[...] = ckvn
    kpe_ref[...] = kpe
    for hp in range(H_C // 2):
        o = hp * 2 * LANE
        nope2 = _dot(qn, wqb_ref[:, QB_NOPE + o:QB_NOPE + o + 2 * LANE])
        rope2 = _dot(qn, wqb_ref[:, QB_ROPE + o:QB_ROPE + o + 2 * LANE])
        swap2 = _dot(qn, wqb_ref[:, QB_SWAP + o:QB_SWAP + o + 2 * LANE])
        for j in range(2):
            hh = 2 * hp + j
            sl = slice(j * LANE, (j + 1) * LANE)
            rot = rope2[:, sl] * cc + swap2[:, sl] * ss
            q_ref[:, hh * HEAD_W:hh * HEAD_W + LANE] = (nope2[:, sl] * ATTN_Q_SCALE).astype(BF16)
            q_ref[:, hh * HEAD_W + LANE:(hh + 1) * HEAD_W] = (rot * ATTN_Q_SCALE).astype(BF16)
    _expand_kv(ckvn.astype(BF16), kper, wkvb_ref, k_ref, v_ref)


def _mla_proj_call(tok, x, mods, layer, idx, g_pre, win, qn, wqb, kvn, wkvb, cc, ss):
    tm = tok.tile(MLA_TILE)
    d = tok.d
    row = lambda i: (i, 0)
    outs = [
        jax.ShapeDtypeStruct((tok.n, H_C * HEAD_W), BF16),
        jax.ShapeDtypeStruct((tok.n, H_C * HEAD_W), BF16),
        jax.ShapeDtypeStruct((tok.n, H_C * V_HEAD_C), BF16),
        jax.ShapeDtypeStruct((tok.n, KV_LORA), F32),
        jax.ShapeDtypeStruct((tok.n, LANE), F32),
    ]
    return pl.pallas_call(
        _mla_proj_body,
        grid=(tok.n // tm,),
        in_specs=[
            pl.BlockSpec((tm, d), row),
            _full((1, d)),
            tok.mod_spec(layer, 0, tm),
            tok.mod_spec(layer, 1, tm),
            _layer_block(win, idx), _full(qn.shape), _layer_block(wqb, idx), _full(kvn.shape),
            _layer_block(wkvb, idx),
            tok.rope_spec(tm),
            tok.rope_spec(tm),
        ],
        out_specs=[pl.BlockSpec((tm, o.shape[1]), row) for o in outs],
        out_shape=outs,
        compiler_params=_cparams(("arbitrary",)),
        name="mla_proj",
    )(x, g_pre, mods, mods, win, qn, wqb, kvn, wkvb, cc, ss)


def _cache_expand_body(ckv_ref, kpe_ref, wkvb_ref, k_ref, v_ref):
    _expand_kv(ckv_ref[...].astype(BF16), kpe_ref[...].astype(BF16), wkvb_ref, k_ref, v_ref)


def _cache_expand_call(ckv, kpe_pad, wkvb, idx):
    n = ckv.shape[0]
    tm = 512
    while n % tm:
        tm //= 2
    row = lambda i: (i, 0)
    outs = [jax.ShapeDtypeStruct((n, H_C * HEAD_W), BF16), jax.ShapeDtypeStruct((n, H_C * V_HEAD_C), BF16)]
    return pl.pallas_call(
        _cache_expand_body,
        grid=(n // tm,),
        in_specs=[pl.BlockSpec((tm, KV_LORA), row), pl.BlockSpec((tm, LANE), row), _layer_block(wkvb, idx)],
        out_specs=[pl.BlockSpec((tm, o.shape[1]), row) for o in outs],
        out_shape=outs,
        compiler_params=_cparams(("arbitrary",)),
        name="cache_expand",
    )(ckv, kpe_pad, wkvb)


ATTN_TQ = 512
ATTN_TK = 512
ATTN_HEADS_PER_STEP = 2
ATTN_Q_SCALE = (QK_NOPE + QK_ROPE) ** -0.5 * float(np.log2(np.e))


def _softmax_pv(s, values):
    p = jnp.exp2(s - jnp.max(s, axis=-1, keepdims=True))
    den = jnp.sum(p, axis=-1, keepdims=True)
    p = p.astype(BF16)
    acc = functools.reduce(jnp.add, [_dot(p[:, k0:k0 + vb.shape[0]], vb) for vb, k0 in values])
    return (acc / den).astype(BF16)


def _attn_ctx_body(q_ref, k_ref, v_ref, o_ref):
    for hh in range(H_C):
        s = _dot_nt(q_ref[:, hh * HEAD_W:(hh + 1) * HEAD_W], k_ref[:, hh * HEAD_W:(hh + 1) * HEAD_W])
        o_ref[:, hh * V_HEAD_C:(hh + 1) * V_HEAD_C] = _softmax_pv(s, [(v_ref[:, hh * V_HEAD_C:(hh + 1) * V_HEAD_C], 0)])


def _lane_groups(x):
    return [x[:, g:g + LANE] for g in range(0, x.shape[1], LANE)]


def _attn_lat_body(q_ref, kc_ref, vc_ref, k_ref, v_ref, o_ref, sa_ref, sb_ref, ma_ref, mb_ref, vx_ref):
    past, n_self = kc_ref.shape[0], k_ref.shape[0]
    heads = vx_ref.shape[0]
    tq = sa_ref.shape[0]
    n = q_ref.shape[0] // tq
    tk = min(ATTN_TK, n_self)
    blocks = [(kc_ref, 0, past, 0)] + [(k_ref, k0, tk, past + k0) for k0 in range(0, n_self, tk)]

    for h in range(heads):
        vx_ref[h, 0:past, 0:V_HEAD_C] = vc_ref[:, h * V_HEAD_C:(h + 1) * V_HEAD_C]
        vx_ref[h, past:past + n_self, 0:V_HEAD_C] = v_ref[:, h * V_HEAD_C:(h + 1) * V_HEAD_C]
        vx_ref[h, :, V_HEAD_C:2 * V_HEAD_C] = jnp.ones((past + n_self, V_HEAD_C), BF16)

    def scores(h, i, s_ref, m_ref):
        q = q_ref[pl.ds(pl.multiple_of(i * tq, tq), tq), h * HEAD_W:(h + 1) * HEAD_W]
        mx = None
        for kk_ref, k0, size, col in blocks:
            s = _dot_nt(q, kk_ref[k0:k0 + size, h * HEAD_W:(h + 1) * HEAD_W])
            s_ref[:, col:col + size] = s
            mx = functools.reduce(jnp.maximum, _lane_groups(s) + ([] if mx is None else [mx]))
        m_ref[...] = mx

    def finish(h, i, s_ref, m_ref):
        m = jnp.max(m_ref[...], axis=-1, keepdims=True)
        acc = None
        for _, _, size, col in blocks:
            p = jnp.exp2(s_ref[:, col:col + size] - m).astype(BF16)
            pv = _dot(p, vx_ref[h, col:col + size, :])
            acc = pv if acc is None else acc + pv
        out = acc[:, 0:V_HEAD_C] / acc[:, V_HEAD_C:2 * V_HEAD_C]
        o_ref[pl.ds(pl.multiple_of(i * tq, tq), tq), h * V_HEAD_C:(h + 1) * V_HEAD_C] = out.astype(BF16)

    scores(0, 0, sa_ref, ma_ref)
    for h in range(heads):

        def two_tiles(j, carry, h=h):
            i = 2 * j
            scores(h, i + 1, sb_ref, mb_ref)
            finish(h, i, sa_ref, ma_ref)
            scores(h, i + 2, sa_ref, ma_ref)
            finish(h, i + 1, sb_ref, mb_ref)
            return carry

        lax.fori_loop(0, n // 2 - 1, two_tiles, 0)
        scores(h, n - 1, sb_ref, mb_ref)
        finish(h, n - 2, sa_ref, ma_ref)
        if h + 1 < heads:
            scores(h + 1, 0, sa_ref, ma_ref)
        finish(h, n - 1, sb_ref, mb_ref)


def _attn_ctx_call(tok, q, k, v):
    seq = lambda b: (b, 0)
    return pl.pallas_call(
        _attn_ctx_body,
        grid=(tok.nb_ctx,),
        in_specs=[pl.BlockSpec((tok.l_ctx, H_C * HEAD_W), seq),
                  pl.BlockSpec((tok.l_ctx, H_C * HEAD_W), seq),
                  pl.BlockSpec((tok.l_ctx, H_C * V_HEAD_C), seq)],
        out_specs=pl.BlockSpec((tok.l_ctx, H_C * V_HEAD_C), seq),
        out_shape=jax.ShapeDtypeStruct(v.shape, v.dtype),
        input_output_aliases={2: 0},
        compiler_params=_cparams(("arbitrary",)),
        name="attn_ctx",
    )(q, k, v)


def _attn_lat_call(tok, q, k, v, k_cache, v_cache):
    blk0 = tok.n_ctx // tok.l_lat
    past = k_cache.shape[0] // tok.nb_lat
    tq = min(ATTN_TQ, tok.l_lat // 2)
    hg = ATTN_HEADS_PER_STEP
    seq = lambda b, hh: (blk0 + b, hh)
    cache = lambda b, hh: (b, hh)
    return pl.pallas_call(
        _attn_lat_body,
        grid=(tok.nb_lat, H_C // hg),
        in_specs=[pl.BlockSpec((tok.l_lat, hg * HEAD_W), seq),
                  pl.BlockSpec((past, hg * HEAD_W), cache),
                  pl.BlockSpec((past, hg * V_HEAD_C), cache),
                  pl.BlockSpec((tok.l_lat, hg * HEAD_W), seq),
                  pl.BlockSpec((tok.l_lat, hg * V_HEAD_C), seq)],
        out_specs=pl.BlockSpec((tok.l_lat, hg * V_HEAD_C), seq),
        out_shape=jax.ShapeDtypeStruct(v.shape, v.dtype),
        input_output_aliases={4: 0},
        scratch_shapes=([pltpu.VMEM((tq, past + tok.l_lat), F32)] * 2 + [pltpu.VMEM((tq, LANE), F32)] * 2
                        + [pltpu.VMEM((hg, past + tok.l_lat, 2 * V_HEAD_C), BF16)]),
        compiler_params=_cparams(("arbitrary", "arbitrary")),
        name="attn_lat",
    )(q, k_cache, v_cache, k, v)


def _swap_halves(w, head_dim):
    lead = w.shape[:-1]
    halves = w.reshape(lead + (w.shape[-1] // head_dim, 2, head_dim // 2))
    return halves[..., ::-1, :].reshape(w.shape)


def _pad_cols(w, width):
    return jnp.pad(w, ((0, 0),) * (w.ndim - 1) + ((0, width - w.shape[-1]),))


def _even_weights(w_in, w_gk2, b_gk2):
    n = w_in.shape[0]
    sizes = (A_QK, A_QK, A_V, A_V, 2 * GATE_RANK, A_QK, A_QK, A_V, A_V)
    qa, ka, va, ga, gk, qb, kb, vb, gb = jnp.split(w_in, np.cumsum(sizes)[:-1].tolist(), axis=-1)
    w = jnp.concatenate([qa, ka, va, ga, qb, kb, vb, gb, _pad_cols(gk, LANE),
                         _swap_halves(qb, DK), _swap_halves(kb, DK)], axis=-1).astype(BF16)
    wg = jnp.zeros((n, LANE, 2 * A_QK), F32)
    wg = wg.at[:, 0:GATE_RANK, 0:A_QK].set(w_gk2[:, 0]).at[:, GATE_RANK:2 * GATE_RANK, A_QK:2 * A_QK].set(w_gk2[:, 1])
    bg = b_gk2.reshape(n, 1, 2 * A_QK)
    return w, wg.astype(BF16), bg


def _odd_weights(w_in, w_q_b, w_kv_b):
    n = w_in.shape[0]
    q_lat, ckv, kpe = w_in[..., :Q_LORA], w_in[..., Q_LORA:Q_LORA + KV_LORA], w_in[..., Q_LORA + KV_LORA:]
    win = jnp.concatenate([q_lat, ckv, _pad_cols(kpe, LANE), _pad_cols(_swap_halves(kpe, QK_ROPE), LANE)],
                          axis=-1).astype(BF16)
    wq = w_q_b.reshape(n, Q_LORA, H_C, QK_NOPE + QK_ROPE)
    nope = wq[..., :QK_NOPE].reshape(n, Q_LORA, H_C * QK_NOPE)
    rope = wq[..., QK_NOPE:]
    pad = lambda r: _pad_cols(r, LANE).reshape(n, Q_LORA, H_C * LANE)
    wqb = jnp.concatenate([nope, pad(rope), pad(_swap_halves(rope, QK_ROPE))], axis=-1).astype(BF16)
    wkv = w_kv_b.reshape(n, KV_LORA, H_C, QK_NOPE + V_HEAD_C)
    wkvb = jnp.concatenate([wkv[..., :QK_NOPE].reshape(n, KV_LORA, H_C * QK_NOPE),
                            wkv[..., QK_NOPE:].reshape(n, KV_LORA, H_C * V_HEAD_C)], axis=-1).astype(BF16)
    return win, wqb, wkvb


def _rope_tables(tok, tm):
    rows = tok.l_lat // GRID_W
    row = jnp.repeat(jnp.arange(rows), GRID_W).astype(F32)
    col = jnp.tile(jnp.arange(GRID_W), rows).astype(F32)
    n_freq = QK_ROPE // 4
    inv = ROPE_BASE ** (-jnp.arange(n_freq, dtype=F32) / n_freq)
    ang = jnp.concatenate([row[:, None] * inv, col[:, None] * inv], axis=-1)
    cos, sin = jnp.cos(ang), jnp.sin(ang)
    cc = jnp.tile(jnp.concatenate([cos, cos], axis=-1), (1, LANE // QK_ROPE))
    ss = jnp.tile(jnp.concatenate([-sin, sin], axis=-1), (1, LANE // QK_ROPE))
    cc = jnp.concatenate([jnp.ones((tm, LANE), F32), cc], axis=0)
    ss = jnp.concatenate([jnp.zeros((tm, LANE), F32), ss], axis=0)
    return cc, ss


def _states_to_kernel(s):
    nb, _, heads = s.shape[:3]
    st = jnp.swapaxes(s.reshape(nb, 2, heads // 2, 2, DK, DV), -1, -2)
    z = jnp.zeros_like(st[:, :, :, 0])
    rows = [jnp.concatenate([st[:, :, :, 0], z], axis=-1), jnp.concatenate([z, st[:, :, :, 1]], axis=-1)]
    return jnp.concatenate(rows, axis=-2)


def _states_from_kernel(st):
    nb, _, pairs = st.shape[:3]
    heads = jnp.stack([st[:, :, :, :DV, :DK], st[:, :, :, DV:, DK:]], axis=3)
    return jnp.swapaxes(heads, -1, -2).reshape(nb, 2, 2 * pairs, DK, DV)


def kernel(x_prompt, x_sample, cache_ckv, cache_kpe, state_gla, state_ret, c, c_ctx, w_ada, b_ada, norm_mix_pre, norm_mix_post, norm_mlp_pre, norm_mlp_post, w_in_even, w_gk2, b_gk2, gla_norm, ret_decay, w_out_even, w_in_odd, q_a_norm, w_q_b, kv_a_norm, w_kv_b, w_out_odd, w_mlp1, w_mlp2):
    nb_ctx, l_ctx, d = x_prompt.shape
    nb_lat, l_lat, _ = x_sample.shape
    depth = w_ada.shape[0]
    tok = _Tokens(nb_ctx, l_ctx, nb_lat, l_lat, d)
    assert nb_lat < MOD_ROWS and tok.n_ctx % l_lat == 0 and l_ctx % CHUNK == 0 and l_lat % (2 * CHUNK) == 0

    cond = jnp.concatenate([c, c_ctx[None, :], jnp.zeros((MOD_ROWS - nb_lat - 1, d), F32)], axis=0)
    mods = _ada_call(cond, w_ada, b_ada).reshape(depth, MOD_ROWS, 6, 1, d)
    rope_even = _rope_tables(tok, tok.tile(EVEN_TILE))
    rope_odd = _rope_tables(tok, tok.tile(MLA_TILE))
    x = (x_prompt.reshape(tok.n_ctx, d), x_sample.reshape(tok.n_lat, d))
    vec = lambda a: a.reshape(1, -1)
    mlp_w = (w_mlp1, w_mlp2)
    even_w = _even_weights(w_in_even, w_gk2, b_gk2) + (jnp.repeat(ret_decay, DK, axis=-1),)
    odd_w = _odd_weights(w_in_odd, w_q_b, w_kv_b)

    new_ckv, new_kpe, new_gla, new_ret = [], [], [], []
    for l in range(depth):
        i = l // 2
        if l % 2 == 0:
            cast_items = [(w, ll) for ll in range(l, min(l + 2, depth)) for w in mlp_w]
            cast_items += [(w_out_even, i)] + ([(w_out_odd, i)] if l + 1 < depth else [])
            q, k, lf, lb, gt, v, *w16 = _even_proj_call(tok, x, mods, l, i, vec(norm_mix_pre[l]), *even_w,
                                                          *rope_even, cast_items)
            gain = jnp.concatenate([jnp.tile(gla_norm[i], N_HEAD_SCAN), jnp.ones((A_V,), F32)]).reshape(1, 2 * A_V)
            s0_lat = _states_to_kernel(jnp.concatenate([state_gla[:, i], state_ret[:, i]], axis=2))
            m_ctx, s_fin = _scan_call(tok, q, k, lf, lb, v, gt, gain)
            (m,) = _scan_call(tok, q, k, lf, lb, m_ctx, gt, gain, s0=s0_lat)
            s_fin = _states_from_kernel(s_fin)
            new_gla.append(s_fin[:, :, :N_HEAD_SCAN])
            new_ret.append(s_fin[:, :, N_HEAD_SCAN:])
            w_out = w16[-2] if l + 1 < depth else w16[-1]
        else:
            win, wqb, wkvb = odd_w
            q, k, v, ckv, kpe = _mla_proj_call(tok, x, mods, l, i, vec(norm_mix_pre[l]), win, vec(q_a_norm[i]), wqb,
                                               vec(kv_a_norm[i]), wkvb, *rope_odd)
            past = cache_ckv.shape[2]
            kpe_pad = jnp.pad(cache_kpe[:, i].reshape(nb_lat * past, QK_ROPE), ((0, 0), (0, LANE - QK_ROPE)))
            k_c, v_c = _cache_expand_call(cache_ckv[:, i].reshape(nb_lat * past, KV_LORA), kpe_pad, wkvb, i)
            m = _attn_lat_call(tok, q, k, _attn_ctx_call(tok, q, k, v), k_c, v_c)
            new_ckv.append(ckv[:tok.n_ctx].reshape(nb_ctx, l_ctx, KV_LORA))
            new_kpe.append(kpe[:tok.n_ctx, :QK_ROPE].reshape(nb_ctx, l_ctx, QK_ROPE))
            w_out = w16[-1]
        w1, w2 = w16[2 * (l % 2):2 * (l % 2) + 2]
        x = _mix_mlp_call(tok, m, x, mods, l, w_out, vec(norm_mix_post[l]), vec(norm_mlp_pre[l]), w1, w2,
                          vec(norm_mlp_post[l]), split_out=(l == depth - 1))
        x = x[0] if len(x) == 1 else tuple(x)

    return (x[0].reshape(nb_ctx, l_ctx, d), x[1].reshape(nb_lat, l_lat, d),
            jnp.stack(new_ckv, axis=1), jnp.stack(new_kpe, axis=1),
            jnp.stack(new_gla, axis=1), jnp.stack(new_ret, axis=1))
```

```python
import functools

import numpy as np
import jax
import jax.numpy as jnp
from jax import lax
from jax.experimental import pallas as pl
from jax.experimental.pallas import tpu as pltpu

F32 = jnp.float32
BF16 = jnp.bfloat16

EPS = 1e-6
ROPE_BASE = 10000.0
GRID_W = 64
CHUNK = 64
GATE_RANK = 16
GATE_NORM = 16.0
N_HEAD_SCAN = 4
DK = 64
DV = 128
H_C = 8
Q_LORA = 256
KV_LORA = 256
QK_NOPE = 128
QK_ROPE = 64
V_HEAD_C = 128
LANE = 128
MOD_ROWS = 16
EVEN_TILE = 512
MLA_TILE = 1024

VMEM_LIMIT = 56 * 1024 * 1024


def _cparams(sem):
    return pltpu.CompilerParams(dimension_semantics=sem, vmem_limit_bytes=VMEM_LIMIT)


def _dot(a, b):
    return jnp.dot(a, b, preferred_element_type=F32)


def _dot_nt(a, b):
    return lax.dot_general(a, b, (((1,), (1,)), ((), ())), preferred_element_type=F32)


def _dot_tn(a, b):
    return lax.dot_general(a, b, (((0,), (0,)), ((), ())), preferred_element_type=F32)


def _rms(x):
    return x * lax.rsqrt(jnp.mean(x * x, axis=-1, keepdims=True) + EPS)


def _silu(x):
    return x * jax.nn.sigmoid(x)


def _full(shape):
    n = len(shape)
    return pl.BlockSpec(shape, lambda *_: (0,) * n, pipeline_mode=pl.Buffered(1))


def _layer_block(w, idx):
    tail = (0,) * (w.ndim - 1)
    return pl.BlockSpec((None,) + w.shape[1:], lambda *_: (idx,) + tail, pipeline_mode=pl.Buffered(1))


def _side_cast_specs(items, steps):
    nb = 1 << (steps.bit_length() - 1)
    block = lambda i: jnp.minimum(i, nb - 1)
    ins = [pl.BlockSpec((None, w.shape[1] // nb, w.shape[2]), lambda i, layer=layer: (layer, block(i), 0))
           for w, layer in items]
    outs = [pl.BlockSpec((w.shape[1] // nb, w.shape[2]), lambda i: (block(i), 0)) for w, _ in items]
    shapes = [jax.ShapeDtypeStruct(w.shape[1:], BF16) for w, _ in items]
    return ins, outs, shapes


def _side_cast(in_refs, out_refs):
    for src, dst in zip(in_refs, out_refs):
        dst[...] = src[...].astype(BF16)


def _ada_body(cond_ref, w_ref, b_ref, o_ref):
    s = _silu(cond_ref[...]).astype(BF16)
    o_ref[...] = _dot(s, w_ref[...].astype(BF16)) + b_ref[...]


def _ada_call(cond, w_ada, b_ada):
    depth, d, n = w_ada.shape
    tn = 1536
    return pl.pallas_call(
        _ada_body,
        grid=(depth, n // tn),
        in_specs=[
            pl.BlockSpec((MOD_ROWS, d), lambda l, j: (0, 0)),
            pl.BlockSpec((None, d, tn), lambda l, j: (l, 0, j)),
            pl.BlockSpec((None, 1, tn), lambda l, j: (l, 0, j)),
        ],
        out_specs=pl.BlockSpec((None, MOD_ROWS, tn), lambda l, j: (l, 0, j)),
        out_shape=jax.ShapeDtypeStruct((depth, MOD_ROWS, n), F32),
        compiler_params=_cparams(("arbitrary", "arbitrary")),
        name="ada_mod",
    )(cond, w_ada, b_ada.reshape(depth, 1, n))


class _Tokens:
    def __init__(self, nb_ctx, l_ctx, nb_lat, l_lat, d):
        self.nb_ctx, self.l_ctx, self.nb_lat, self.l_lat, self.d = nb_ctx, l_ctx, nb_lat, l_lat, d
        self.n_ctx = nb_ctx * l_ctx
        self.n_lat = nb_lat * l_lat
        self.n = self.n_ctx + self.n_lat
        self.ctx_row = nb_lat

    def tile(self, want):
        t = want
        while self.n_ctx % t or self.l_lat % t:
            t //= 2
        return t

    def mod_spec(self, layer, chunk, tm):
        n_ctx, l_lat, ctx_row = self.n_ctx, self.l_lat, self.ctx_row

        def idx(i, *_):
            start = i * tm
            row = jnp.where(start < n_ctx, ctx_row, (start - n_ctx) // l_lat)
            return (layer, row, chunk, 0, 0)

        return pl.BlockSpec((None, None, None, 1, self.d), idx)

    def x_specs(self, x, tm):
        if not isinstance(x, tuple):
            return [pl.BlockSpec((tm, self.d), lambda i: (i, 0))], [x]
        nct = self.n_ctx // tm
        return [pl.BlockSpec((tm, self.d), lambda i: (jnp.minimum(i, nct - 1), 0)),
                pl.BlockSpec((tm, self.d), lambda i: (jnp.maximum(i - nct, 0), 0))], list(x)

    def rope_spec(self, tm):
        n_ctx, l_lat = self.n_ctx, self.l_lat

        def idx(i):
            start = i * tm
            return (jnp.where(start < n_ctx, 0, 1 + ((start - n_ctx) % l_lat) // tm), 0)

        return pl.BlockSpec((tm, LANE), idx)


def _read_x(x_refs, n_ctx_tiles):
    if len(x_refs) == 1:
        return x_refs[0][...]
    return jnp.where(pl.program_id(0) < n_ctx_tiles, x_refs[0][...], x_refs[1][...])


A_QK = N_HEAD_SCAN * DK
A_V = N_HEAD_SCAN * DV
E_QA, E_KA, E_VA, E_GA = 0, A_QK, 2 * A_QK, 2 * A_QK + A_V
E_QB = E_GA + A_V
E_KB = E_QB + A_QK
E_VB = E_KB + A_QK
E_GB = E_VB + A_V
E_GK = E_GB + A_V
E_QBS = E_GK + LANE
E_KBS = E_QBS + A_QK
E_COLS = E_KBS + A_QK


def _log_sigmoid(x):
    return jnp.minimum(x, 0.0) - jnp.log1p(jnp.exp(-jnp.abs(x)))


def _even_proj_body(*refs, n_x, n_ctx_tiles, n_cast):
    x = _read_x(refs[:n_x], n_ctx_tiles)
    g_ref, sh_ref, sc_ref, w_ref, wg_ref, bg_ref, dec_ref, cc_ref, ss_ref = refs[n_x:n_x + 9]
    q_ref, k_ref, lf_ref, lb_ref, gt_ref, v_ref = refs[n_x + 9 + n_cast:n_x + 15 + n_cast]
    _side_cast(refs[n_x + 9:n_x + 9 + n_cast], refs[n_x + 15 + n_cast:])
    tm = x.shape[0]
    h = (_rms(x) * g_ref[...] * (1.0 + sc_ref[...]) + sh_ref[...]).astype(BF16)

    def proj(start, width):
        return _dot(h, w_ref[:, start:start + width])

    cc = cc_ref[...]
    ss = ss_ref[...]
    scale = DK ** -0.5
    q_ref[:, 0:A_QK] = proj(E_QA, A_QK) * scale
    k_ref[:, 0:A_QK] = proj(E_KA, A_QK)
    qb, qbs, kb, kbs = proj(E_QB, A_QK), proj(E_QBS, A_QK), proj(E_KB, A_QK), proj(E_KBS, A_QK)
    for j in range(A_QK // LANE):
        sl = slice(j * LANE, (j + 1) * LANE)
        o = A_QK + j * LANE
        q_ref[:, o:o + LANE] = qb[:, sl] * cc + qbs[:, sl] * ss
        k_ref[:, o:o + LANE] = (kb[:, sl] * cc + kbs[:, sl] * ss) * scale
    v_ref[:, 0:A_V] = proj(E_VA, A_V).astype(BF16)
    v_ref[:, A_V:2 * A_V] = proj(E_VB, A_V).astype(BF16)
    gt_ref[:, 0:A_V] = proj(E_GA, A_V)
    gt_ref[:, A_V:2 * A_V] = proj(E_GB, A_V)
    gk = proj(E_GK, LANE).astype(BF16)
    la = _log_sigmoid(_dot(gk, wg_ref[...]) + bg_ref[...]) * (1.0 / GATE_NORM)
    lf_ref[:, 0:A_QK] = la[:, 0:A_QK]
    lb_ref[:, 0:A_QK] = la[:, A_QK:2 * A_QK]
    log_g = -jnp.exp(dec_ref[...])
    lf_ref[:, A_QK:2 * A_QK] = jnp.broadcast_to(log_g[0:1, :], (tm, A_QK))
    lb_ref[:, A_QK:2 * A_QK] = jnp.broadcast_to(log_g[1:2, :], (tm, A_QK))


def _even_proj_call(tok, x, mods, layer, idx, g_pre, w, wg, bg, dec, cc, ss, cast_items):
    tm = tok.tile(EVEN_TILE)
    d = tok.d
    row = lambda i: (i, 0)
    outs = [
        jax.ShapeDtypeStruct((tok.n, 2 * A_QK), F32),
        jax.ShapeDtypeStruct((tok.n, 2 * A_QK), F32),
        jax.ShapeDtypeStruct((tok.n, 2 * A_QK), F32),
        jax.ShapeDtypeStruct((tok.n, 2 * A_QK), F32),
        jax.ShapeDtypeStruct((tok.n, 2 * A_V), F32),
        jax.ShapeDtypeStruct((tok.n, 2 * A_V), BF16),
    ]
    x_specs, x_args = tok.x_specs(x, tm)
    cast_in, cast_out, cast_shape = _side_cast_specs(cast_items, tok.n // tm)
    return pl.pallas_call(
        functools.partial(_even_proj_body, n_x=len(x_args), n_ctx_tiles=tok.n_ctx // tm, n_cast=len(cast_items)),
        grid=(tok.n // tm,),
        in_specs=x_specs + [
            _full((1, d)),
            tok.mod_spec(layer, 0, tm),
            tok.mod_spec(layer, 1, tm),
            _layer_block(w, idx), _layer_block(wg, idx), _layer_block(bg, idx), _layer_block(dec, idx),
            tok.rope_spec(tm),
            tok.rope_spec(tm),
        ] + cast_in,
        out_specs=[pl.BlockSpec((tm, o.shape[1]), row) for o in outs] + cast_out,
        out_shape=outs + cast_shape,
        compiler_params=_cparams(("arbitrary",)),
        name="even_proj",
    )(*x_args, g_pre, mods, mods, w, wg, bg, dec, cc, ss, *[stack for stack, _ in cast_items])


SCAN_GROUP = 8


def _chunk_cumsum(x):
    row = lax.broadcasted_iota(jnp.int32, x.shape, 0) % CHUNK
    s = 1
    while s < CHUNK:
        x = x + jnp.where(row >= s, pltpu.roll(x, s, axis=0), 0.0)
        s *= 2
    return x


def _scan_body(q_ref, k_ref, lf_ref, lb_ref, v_ref, gt_ref, gain_ref, *rest, seq_len, group, context):
    if context:
        m_ref, sfin_ref, st_ref, o_acc = rest
    else:
        s0_ref, m_ref, st_ref, o_acc = rest
    C = CHUNK
    blk = group * C
    nblk = seq_len // blk
    pair_w = 2 * DK
    head0 = lax.broadcasted_iota(jnp.int32, (blk, pair_w), 1) < DK
    t_in = lax.broadcasted_iota(jnp.int32, (C, pair_w), 0)
    j_in = lax.broadcasted_iota(jnp.int32, (C, pair_w), 1) % DK
    keep_fwd = t_in >= j_in
    keep_bwd = t_in <= j_in
    on_diag = ((lax.broadcasted_iota(jnp.int32, (2 * DV, pair_w), 0) < DV)
               == (lax.broadcasted_iota(jnp.int32, (2 * DV, pair_w), 1) < DK))
    zeros_v = jnp.zeros((C, DV), BF16)

    if context:
        st_ref[...] = jnp.zeros(st_ref.shape, F32)
    else:
        zeros_s = jnp.zeros((DK, DV), F32)
        for d in range(2):
            both = jnp.concatenate([jnp.concatenate([s0_ref[d, 0], zeros_s], axis=1),
                                    jnp.concatenate([zeros_s, s0_ref[d, 1]], axis=1)], axis=0)
            st_ref[d] = both.T

    def one_direction(r0, d, log_ref, keep, reverse):
        g = log_ref[pl.ds(r0, blk), :]
        b = _chunk_cumsum(g)
        tots = [b[C * j + C - 1:C * j + C, :] for j in range(group)]
        totb = jnp.concatenate([jnp.broadcast_to(t, (C, pair_w)) for t in tots], axis=0)
        if reverse:
            b = totb - b + g
        q = q_ref[pl.ds(r0, blk), :]
        k = k_ref[pl.ds(r0, blk), :]
        vblk = v_ref[pl.ds(r0, blk), :]
        q_dec = (q * jnp.exp(b)).astype(BF16)
        k_inv = k * jnp.exp(-b)
        k_up = (k * jnp.exp(totb - b)).astype(BF16)
        k_inv0 = jnp.where(head0, k_inv, 0.0).astype(BF16)
        k_inv1 = jnp.where(head0, 0.0, k_inv).astype(BF16)
        st = st_ref[d]
        outs = [None] * group
        for j in (reversed(range(group)) if reverse else range(group)):
            sl = slice(C * j, C * (j + 1))
            k_bd = jnp.concatenate([k_inv0[sl], k_inv1[sl]], axis=0)
            a = jnp.where(keep, _dot_nt(q_dec[sl], k_bd), 0.0).astype(BF16)
            vc = vblk[sl]
            v_bd = jnp.concatenate([jnp.concatenate([vc[:, :DV], zeros_v], axis=1),
                                    jnp.concatenate([zeros_v, vc[:, DV:]], axis=1)], axis=0)
            outs[j] = _dot_nt(q_dec[sl], st.astype(BF16)) + _dot(a, v_bd)
            st = st * jnp.exp(tots[j]) + jnp.where(on_diag, _dot_tn(vc, k_up[sl]), 0.0)
        st_ref[d] = st
        return jnp.concatenate(outs, axis=0)

    def emit(r0, o):
        gate = _silu(gt_ref[pl.ds(r0, blk), :])
        gain = gain_ref[...]
        for hh in range(2):
            sl = slice(hh * DV, (hh + 1) * DV)
            m_ref[pl.ds(r0, blk), sl] = (_rms(o[:, sl]) * gain[:, sl] * gate[:, sl]).astype(BF16)

    def body(c, carry, second_half):
        rf = pl.multiple_of(c * blk, blk)
        rb = pl.multiple_of((nblk - 1 - c) * blk, blk)
        o_f = one_direction(rf, 0, lf_ref, keep_fwd, False)
        o_b = one_direction(rb, 1, lb_ref, keep_bwd, True)
        if second_half:
            emit(rf, o_acc[pl.ds(rf, blk), :] + o_f)
            emit(rb, o_acc[pl.ds(rb, blk), :] + o_b)
        else:
            o_acc[pl.ds(rf, blk), :] = o_f
            o_acc[pl.ds(rb, blk), :] = o_b
        return carry

    if nblk == 1:
        emit(0, one_direction(0, 0, lf_ref, keep_fwd, False) + one_direction(0, 1, lb_ref, keep_bwd, True))
    else:
        lax.fori_loop(0, nblk // 2, functools.partial(body, second_half=False), 0)
        lax.fori_loop(nblk // 2, nblk, functools.partial(body, second_half=True), 0)

    if context:
        for d in range(2):
            both = st_ref[d].T
            sfin_ref[d, 0] = both[0:DK, 0:DV]
            sfin_ref[d, 1] = both[DK:2 * DK, DV:2 * DV]


def _scan_call(tok, q, k, lf, lb, v, gt, gain, s0=None):
    context = s0 is None
    if context:
        nb, seq_len, blk0 = tok.nb_ctx, tok.l_ctx, 0
    else:
        nb, seq_len, blk0 = tok.nb_lat, tok.l_lat, tok.n_ctx // tok.l_lat
    pairs = q.shape[1] // (2 * DK)
    group = min(SCAN_GROUP, seq_len // CHUNK)
    seq = lambda b, p: (blk0 + b, p)
    st_spec = pl.BlockSpec((None, 2, 2, DK, DV), lambda b, p: (b, 0, p, 0, 0))
    in_specs = [
        pl.BlockSpec((seq_len, 2 * DK), seq),
        pl.BlockSpec((seq_len, 2 * DK), seq),
        pl.BlockSpec((seq_len, 2 * DK), seq),
        pl.BlockSpec((seq_len, 2 * DK), seq),
        pl.BlockSpec((seq_len, 2 * DV), seq),
        pl.BlockSpec((seq_len, 2 * DV), seq),
        pl.BlockSpec((1, 2 * DV), lambda b, p: (0, p)),
    ]
    args = [q, k, lf, lb, v, gt, gain]
    out_shape = [jax.ShapeDtypeStruct(v.shape, v.dtype)]
    out_specs = [pl.BlockSpec((seq_len, 2 * DV), seq)]
    if context:
        out_shape.append(jax.ShapeDtypeStruct((nb, 2, 2 * pairs, DK, DV), F32))
        out_specs.append(st_spec)
    else:
        in_specs.append(st_spec)
        args.append(s0)
    return pl.pallas_call(
        functools.partial(_scan_body, seq_len=seq_len, group=group, context=context),
        grid=(nb, pairs),
        in_specs=in_specs,
        out_specs=out_specs,
        out_shape=out_shape,
        input_output_aliases={4: 0},
        scratch_shapes=[pltpu.VMEM((2, 2 * DV, 2 * DK), F32), pltpu.VMEM((seq_len, 2 * DV), F32)],
        compiler_params=_cparams(("arbitrary", "arbitrary")),
        name="scan_ctx" if context else "scan_lat",
    )(*args)


MIX_ROW_GROUPS = 4


def _mix_mlp_body(*refs, n_x, n_out, n_ctx_tiles):
    x = _read_x(refs[:n_x], n_ctx_tiles)
    (m_ref, wo_ref, gmix_ref, gate1_ref, gpre_ref, sh_ref, sc_ref, w1_ref, w2_ref,
     gpost_ref, gate2_ref) = refs[n_x:len(refs) - n_out]
    out_refs = refs[len(refs) - n_out:]
    rows = x.shape[0] // MIX_ROW_GROUPS
    x1_parts, h_parts = [], []
    for r0 in range(0, x.shape[0], rows):
        y = _dot(m_ref[r0:r0 + rows, :], wo_ref[...])
        x1_g = x[r0:r0 + rows] + gate1_ref[...] * (_rms(y) * gmix_ref[...])
        x1_parts.append(x1_g)
        h_parts.append((_rms(x1_g) * gpre_ref[...] * (1.0 + sc_ref[...]) + sh_ref[...]).astype(BF16))
    x1 = jnp.concatenate(x1_parts, axis=0)
    h = jnp.concatenate(h_parts, axis=0)
    u = jnp.maximum(_dot(h, w1_ref[...]), 0.0)
    z = _dot((u * u).astype(BF16), w2_ref[...])
    res = x1 + gate2_ref[...] * (_rms(z) * gpost_ref[...])
    if n_out == 1:
        out_refs[0][...] = res
    else:
        is_ctx = pl.program_id(0) < n_ctx_tiles

        @pl.when(is_ctx)
        def _():
            out_refs[0][...] = res

        @pl.when(jnp.logical_not(is_ctx))
        def _():
            out_refs[1][...] = res


def _mix_mlp_call(tok, m, x, mods, layer, w_out, g_mix, g_pre, w1, w2, g_post, split_out=False):
    tm = tok.tile(512)
    d = tok.d
    row = lambda i: (i, 0)
    x_specs, x_args = tok.x_specs(x, tm)
    nct = tok.n_ctx // tm
    if split_out:
        out_specs = [pl.BlockSpec((tm, d), lambda i: (jnp.minimum(i, nct - 1), 0)),
                     pl.BlockSpec((tm, d), lambda i: (jnp.maximum(i - nct, 0), 0))]
        out_shape = [jax.ShapeDtypeStruct((tok.n_ctx, d), F32), jax.ShapeDtypeStruct((tok.n_lat, d), F32)]
    else:
        out_specs = [pl.BlockSpec((tm, d), row)]
        out_shape = [jax.ShapeDtypeStruct((tok.n, d), F32)]
    return pl.pallas_call(
        functools.partial(_mix_mlp_body, n_x=len(x_args), n_out=len(out_shape), n_ctx_tiles=nct),
        grid=(tok.n // tm,),
        in_specs=x_specs + [
            pl.BlockSpec((tm, m.shape[1]), row),
            _full(w_out.shape),
            _full((1, d)),
            tok.mod_spec(layer, 2, tm),
            _full((1, d)),
            tok.mod_spec(layer, 3, tm),
            tok.mod_spec(layer, 4, tm),
            _full(w1.shape),
            _full(w2.shape),
            _full((1, d)),
            tok.mod_spec(layer, 5, tm),
        ],
        out_specs=out_specs,
        out_shape=out_shape,
        compiler_params=_cparams(("arbitrary",)),
        name="mix_mlp",
    )(*x_args, m, w_out, g_mix, mods, g_pre, mods, mods, w1, w2, g_post, mods)


HEAD_W = 2 * LANE
O_QLAT, O_CKV, O_KPE, O_KPES, O_COLS = 0, Q_LORA, Q_LORA + KV_LORA, Q_LORA + KV_LORA + LANE, Q_LORA + KV_LORA + 2 * LANE
QB_NOPE, QB_ROPE, QB_SWAP = 0, H_C * LANE, 2 * H_C * LANE


def _expand_kv(cb, kper, wkvb_ref, k_ref, v_ref):
    for hp in range(H_C // 2):
        nope2 = _dot(cb, wkvb_ref[:, hp * 2 * LANE:(hp + 1) * 2 * LANE])
        for j in range(2):
            hh = 2 * hp + j
            k_ref[:, hh * HEAD_W:hh * HEAD_W + LANE] = nope2[:, j * LANE:(j + 1) * LANE].astype(BF16)
            k_ref[:, hh * HEAD_W + LANE:(hh + 1) * HEAD_W] = kper
    v_ref[...] = _dot(cb, wkvb_ref[:, H_C * LANE:2 * H_C * LANE]).astype(BF16)


def _mla_proj_body(x_ref, g_ref, sh_ref, sc_ref, win_ref, qn_ref, wqb_ref, kvn_ref, wkvb_ref, cc_ref, ss_ref,
                   q_ref, k_ref, v_ref, ckv_ref, kpe_ref):
    h =(_rms(x_ref[...]) * g_ref[...] * (1.0 + sc_ref[...]) + sh_ref[...]).astype(BF16)
    cc = cc_ref[...]
    ss = ss_ref[...]
    qn = (_rms(_dot(h, win_ref[:, O_QLAT:O_QLAT + Q_LORA])) * qn_ref[...]).astype(BF16)
    ckvn = _rms(_dot(h, win_ref[:, O_CKV:O_CKV + KV_LORA])) * kvn_ref[...]
    kpe2 = _dot(h, win_ref[:, O_KPE:O_KPE + 2 * LANE])
    kpe = kpe2[:, 0:LANE]
    kper = (kpe * cc + kpe2[:, LANE:2 * LANE] * ss).astype(BF16)
    ckv_ref[...] = ckvn
    kpe_ref[...] = kpe
    for hp in range(H_C // 2):
        o = hp * 2 * LANE
        nope2 = _dot(qn, wqb_ref[:, QB_NOPE + o:QB_NOPE + o + 2 * LANE])
        rope2 = _dot(qn, wqb_ref[:, QB_ROPE + o:QB_ROPE + o + 2 * LANE])
        swap2 = _dot(qn, wqb_ref[:, QB_SWAP + o:QB_SWAP + o + 2 * LANE])
        for j in range(2):
            hh = 2 * hp + j
            sl = slice(j * LANE, (j + 1) * LANE)
            rot = rope2[:, sl] * cc + swap2[:, sl] * ss
            q_ref[:, hh * HEAD_W:hh * HEAD_W + LANE] = (nope2[:, sl] * ATTN_Q_SCALE).astype(BF16)
            q_ref[:, hh * HEAD_W + LANE:(hh + 1) * HEAD_W] = (rot * ATTN_Q_SCALE).astype(BF16)
    _expand_kv(ckvn.astype(BF16), kper, wkvb_ref, k_ref, v_ref)


def _mla_proj_call(tok, x, mods, layer, idx, g_pre, win, qn, wqb, kvn, wkvb, cc, ss):
    tm = tok.tile(MLA_TILE)
    d = tok.d
    row = lambda i: (i, 0)
    outs = [
        jax.ShapeDtypeStruct((tok.n, H_C * HEAD_W), BF16),
        jax.ShapeDtypeStruct((tok.n, H_C * HEAD_W), BF16),
        jax.ShapeDtypeStruct((tok.n, H_C * V_HEAD_C), BF16),
        jax.ShapeDtypeStruct((tok.n, KV_LORA), F32),
        jax.ShapeDtypeStruct((tok.n, LANE), F32),
    ]
    return pl.pallas_call(
        _mla_proj_body,
        grid=(tok.n // tm,),
        in_specs=[
            pl.BlockSpec((tm, d), row),
            _full((1, d)),
            tok.mod_spec(layer, 0, tm),
            tok.mod_spec(layer, 1, tm),
            _layer_block(win, idx), _full(qn.shape), _layer_block(wqb, idx), _full(kvn.shape),
            _layer_block(wkvb, idx),
            tok.rope_spec(tm),
            tok.rope_spec(tm),
        ],
        out_specs=[pl.BlockSpec((tm, o.shape[1]), row) for o in outs],
        out_shape=outs,
        compiler_params=_cparams(("arbitrary",)),
        name="mla_proj",
    )(x, g_pre, mods, mods, win, qn, wqb, kvn, wkvb, cc, ss)


def _cache_expand_body(ckv_ref, kpe_ref, wkvb_ref, k_ref, v_ref):
    _expand_kv(ckv_ref[...].astype(BF16), kpe_ref[...].astype(BF16), wkvb_ref, k_ref, v_ref)


def _cache_expand_call(ckv, kpe_pad, wkvb, idx):
    n = ckv.shape[0]
    tm = 512
    while n % tm:
        tm //= 2
    row = lambda i: (i, 0)
    outs = [jax.ShapeDtypeStruct((n, H_C * HEAD_W), BF16), jax.ShapeDtypeStruct((n, H_C * V_HEAD_C), BF16)]
    return pl.pallas_call(
        _cache_expand_body,
        grid=(n // tm,),
        in_specs=[pl.BlockSpec((tm, KV_LORA), row), pl.BlockSpec((tm, LANE), row), _layer_block(wkvb, idx)],
        out_specs=[pl.BlockSpec((tm, o.shape[1]), row) for o in outs],
        out_shape=outs,
        compiler_params=_cparams(("arbitrary",)),
        name="cache_expand",
    )(ckv, kpe_pad, wkvb)


ATTN_TQ = 512
ATTN_TK = 512
ATTN_HEADS_PER_STEP = 2
ATTN_Q_SCALE = (QK_NOPE + QK_ROPE) ** -0.5 * float(np.log2(np.e))


def _softmax_pv(s, values):
    p = jnp.exp2(s - jnp.max(s, axis=-1, keepdims=True))
    den = jnp.sum(p, axis=-1, keepdims=True)
    p = p.astype(BF16)
    acc = functools.reduce(jnp.add, [_dot(p[:, k0:k0 + vb.shape[0]], vb) for vb, k0 in values])
    return (acc / den).astype(BF16)


def _attn_ctx_body(q_ref, k_ref, v_ref, o_ref):
    for hh in range(H_C):
        s = _dot_nt(q_ref[:, hh * HEAD_W:(hh + 1) * HEAD_W], k_ref[:, hh * HEAD_W:(hh + 1) * HEAD_W])
        o_ref[:, hh * V_HEAD_C:(hh + 1) * V_HEAD_C] = _softmax_pv(s, [(v_ref[:, hh * V_HEAD_C:(hh + 1) * V_HEAD_C], 0)])


def _lane_groups(x):
    return [x[:, g:g + LANE] for g in range(0, x.shape[1], LANE)]


def _attn_lat_body(q_ref, kc_ref, vc_ref, k_ref, v_ref, o_ref, sa_ref, sb_ref, ma_ref, mb_ref, vx_ref):
    past, n_self = kc_ref.shape[0], k_ref.shape[0]
    heads = vx_ref.shape[0]
    tq = sa_ref.shape[0]
    n = q_ref.shape[0] // tq
    tk = min(ATTN_TK, n_self)
    blocks = [(kc_ref, 0, past, 0)] + [(k_ref, k0, tk, past + k0) for k0 in range(0, n_self, tk)]

    for h in range(heads):
        vx_ref[h, 0:past, 0:V_HEAD_C] = vc_ref[:, h * V_HEAD_C:(h + 1) * V_HEAD_C]
        vx_ref[h, past:past + n_self, 0:V_HEAD_C] = v_ref[:, h * V_HEAD_C:(h + 1) * V_HEAD_C]
        vx_ref[h, :, V_HEAD_C:2 * V_HEAD_C] = jnp.ones((past + n_self, V_HEAD_C), BF16)

    def scores(h, i, s_ref, m_ref):
        q = q_ref[pl.ds(pl.multiple_of(i * tq, tq), tq), h * HEAD_W:(h + 1) * HEAD_W]
        mx = None
        for kk_ref, k0, size, col in blocks:
            s = _dot_nt(q, kk_ref[k0:k0 + size, h * HEAD_W:(h + 1) * HEAD_W])
            s_ref[:, col:col + size] = s
            mx = functools.reduce(jnp.maximum, _lane_groups(s) + ([] if mx is None else [mx]))
        m_ref[...] = mx

    def finish(h, i, s_ref, m_ref):
        m = jnp.max(m_ref[...], axis=-1, keepdims=True)
        acc = None
        for _, _, size, col in blocks:
            p = jnp.exp2(s_ref[:, col:col + size] - m).astype(BF16)
            pv = _dot(p, vx_ref[h, col:col + size, :])
            acc = pv if acc is None else acc + pv
        out = acc[:, 0:V_HEAD_C] / acc[:, V_HEAD_C:2 * V_HEAD_C]
        o_ref[pl.ds(pl.multiple_of(i * tq, tq), tq), h * V_HEAD_C:(h + 1) * V_HEAD_C] = out.astype(BF16)

    scores(0, 0, sa_ref, ma_ref)
    for h in range(heads):

        def two_tiles(j, carry, h=h):
            i = 2 * j
            scores(h, i + 1, sb_ref, mb_ref)
            finish(h, i, sa_ref, ma_ref)
            scores(h, i + 2, sa_ref, ma_ref)
            finish(h, i + 1, sb_ref, mb_ref)
            return carry

        lax.fori_loop(0, n // 2 - 1, two_tiles, 0)
        scores(h, n - 1, sb_ref, mb_ref)
        finish(h, n - 2, sa_ref, ma_ref)
        if h + 1 < heads:
            scores(h + 1, 0, sa_ref, ma_ref)
        finish(h, n - 1, sb_ref, mb_ref)


def _attn_ctx_call(tok, q, k, v):
    seq = lambda b: (b, 0)
    return pl.pallas_call(
        _attn_ctx_body,
        grid=(tok.nb_ctx,),
        in_specs=[pl.BlockSpec((tok.l_ctx, H_C * HEAD_W), seq),
                  pl.BlockSpec((tok.l_ctx, H_C * HEAD_W), seq),
                  pl.BlockSpec((tok.l_ctx, H_C * V_HEAD_C), seq)],
        out_specs=pl.BlockSpec((tok.l_ctx, H_C * V_HEAD_C), seq),
        out_shape=jax.ShapeDtypeStruct(v.shape, v.dtype),
        input_output_aliases={2: 0},
        compiler_params=_cparams(("arbitrary",)),
        name="attn_ctx",
    )(q, k, v)


def _attn_lat_call(tok, q, k, v, k_cache, v_cache):
    blk0 = tok.n_ctx // tok.l_lat
    past = k_cache.shape[0] // tok.nb_lat
    tq = min(ATTN_TQ, tok.l_lat // 2)
    hg = ATTN_HEADS_PER_STEP
    seq = lambda b, hh: (blk0 + b, hh)
    cache = lambda b, hh: (b, hh)
    return pl.pallas_call(
        _attn_lat_body,
        grid=(tok.nb_lat, H_C // hg),
        in_specs=[pl.BlockSpec((tok.l_lat, hg * HEAD_W), seq),
                  pl.BlockSpec((past, hg * HEAD_W), cache),
                  pl.BlockSpec((past, hg * V_HEAD_C), cache),
                  pl.BlockSpec((tok.l_lat, hg * HEAD_W), seq),
                  pl.BlockSpec((tok.l_lat, hg * V_HEAD_C), seq)],
        out_specs=pl.BlockSpec((tok.l_lat, hg * V_HEAD_C), seq),
        out_shape=jax.ShapeDtypeStruct(v.shape, v.dtype),
        input_output_aliases={4: 0},
        scratch_shapes=([pltpu.VMEM((tq, past + tok.l_lat), F32)] * 2 + [pltpu.VMEM((tq, LANE), F32)] * 2
                        + [pltpu.VMEM((hg, past + tok.l_lat, 2 * V_HEAD_C), BF16)]),
        compiler_params=_cparams(("arbitrary", "arbitrary")),
        name="attn_lat",
    )(q, k_cache, v_cache, k, v)


def _swap_halves(w, head_dim):
    lead = w.shape[:-1]
    halves = w.reshape(lead + (w.shape[-1] // head_dim, 2, head_dim // 2))
    return halves[..., ::-1, :].reshape(w.shape)


def _pad_cols(w, width):
    return jnp.pad(w, ((0, 0),) * (w.ndim - 1) + ((0, width - w.shape[-1]),))


def _even_weights(w_in, w_gk2, b_gk2):
    n = w_in.shape[0]
    sizes = (A_QK, A_QK, A_V, A_V, 2 * GATE_RANK, A_QK, A_QK, A_V, A_V)
    qa, ka, va, ga, gk, qb, kb, vb, gb = jnp.split(w_in, np.cumsum(sizes)[:-1].tolist(), axis=-1)
    w = jnp.concatenate([qa, ka, va, ga, qb, kb, vb, gb, _pad_cols(gk, LANE),
                         _swap_halves(qb, DK), _swap_halves(kb, DK)], axis=-1).astype(BF16)
    wg = jnp.zeros((n, LANE, 2 * A_QK), F32)
    wg = wg.at[:, 0:GATE_RANK, 0:A_QK].set(w_gk2[:, 0]).at[:, GATE_RANK:2 * GATE_RANK, A_QK:2 * A_QK].set(w_gk2[:, 1])
    bg = b_gk2.reshape(n, 1, 2 * A_QK)
    return w, wg.astype(BF16), bg


def _odd_weights(w_in, w_q_b, w_kv_b):
    n = w_in.shape[0]
    q_lat, ckv, kpe = w_in[..., :Q_LORA], w_in[..., Q_LORA:Q_LORA + KV_LORA], w_in[..., Q_LORA + KV_LORA:]
    win = jnp.concatenate([q_lat, ckv, _pad_cols(kpe, LANE), _pad_cols(_swap_halves(kpe, QK_ROPE), LANE)],
                          axis=-1).astype(BF16)
    wq = w_q_b.reshape(n, Q_LORA, H_C, QK_NOPE + QK_ROPE)
    nope = wq[..., :QK_NOPE].reshape(n, Q_LORA, H_C * QK_NOPE)
    rope = wq[..., QK_NOPE:]
    pad = lambda r: _pad_cols(r, LANE).reshape(n, Q_LORA, H_C * LANE)
    wqb = jnp.concatenate([nope, pad(rope), pad(_swap_halves(rope, QK_ROPE))], axis=-1).astype(BF16)
    wkv = w_kv_b.reshape(n, KV_LORA, H_C, QK_NOPE + V_HEAD_C)
    wkvb = jnp.concatenate([wkv[..., :QK_NOPE].reshape(n, KV_LORA, H_C * QK_NOPE),
                            wkv[..., QK_NOPE:].reshape(n, KV_LORA, H_C * V_HEAD_C)], axis=-1).astype(BF16)
    return win, wqb, wkvb


def _rope_tables(tok, tm):
    rows = tok.l_lat // GRID_W
    row = jnp.repeat(jnp.arange(rows), GRID_W).astype(F32)
    col = jnp.tile(jnp.arange(GRID_W), rows).astype(F32)
    n_freq = QK_ROPE // 4
    inv = ROPE_BASE ** (-jnp.arange(n_freq, dtype=F32) / n_freq)
    ang = jnp.concatenate([row[:, None] * inv, col[:, None] * inv], axis=-1)
    cos, sin = jnp.cos(ang), jnp.sin(ang)
    cc = jnp.tile(jnp.concatenate([cos, cos], axis=-1), (1, LANE // QK_ROPE))
    ss = jnp.tile(jnp.concatenate([-sin, sin], axis=-1), (1, LANE // QK_ROPE))
    cc = jnp.concatenate([jnp.ones((tm, LANE), F32), cc], axis=0)
    ss = jnp.concatenate([jnp.zeros((tm, LANE), F32), ss], axis=0)
    return cc, ss


def kernel(x_prompt, x_sample, cache_ckv, cache_kpe, state_gla, state_ret, c, c_ctx, w_ada, b_ada, norm_mix_pre, norm_mix_post, norm_mlp_pre, norm_mlp_post, w_in_even, w_gk2, b_gk2, gla_norm, ret_decay, w_out_even, w_in_odd, q_a_norm, w_q_b, kv_a_norm, w_kv_b, w_out_odd, w_mlp1, w_mlp2):
    nb_ctx, l_ctx, d = x_prompt.shape
    nb_lat, l_lat, _ = x_sample.shape
    depth = w_ada.shape[0]
    tok = _Tokens(nb_ctx, l_ctx, nb_lat, l_lat, d)
    assert nb_lat < MOD_ROWS and tok.n_ctx % l_lat == 0 and l_ctx % CHUNK == 0 and l_lat % (2 * CHUNK) == 0

    cond = jnp.concatenate([c, c_ctx[None, :], jnp.zeros((MOD_ROWS - nb_lat - 1, d), F32)], axis=0)
    mods = _ada_call(cond, w_ada, b_ada).reshape(depth, MOD_ROWS, 6, 1, d)
    rope_even = _rope_tables(tok, tok.tile(EVEN_TILE))
    rope_odd = _rope_tables(tok, tok.tile(MLA_TILE))
    x = (x_prompt.reshape(tok.n_ctx, d), x_sample.reshape(tok.n_lat, d))
    vec = lambda a: a.reshape(1, -1)
    mlp_w = (w_mlp1, w_mlp2)
    even_w = _even_weights(w_in_even, w_gk2, b_gk2) + (jnp.repeat(ret_decay, DK, axis=-1),)
    odd_w = _odd_weights(w_in_odd, w_q_b, w_kv_b)

    new_ckv, new_kpe, new_gla, new_ret = [], [], [], []
    for l in range(depth):
        i = l // 2
        if l % 2 == 0:
            cast_items = [(w, ll) for ll in range(l, min(l + 2, depth)) for w in mlp_w]
            cast_items += [(w_out_even, i)] + ([(w_out_odd, i)] if l + 1 < depth else [])
            q, k, lf, lb, gt, v, *w16 = _even_proj_call(tok, x, mods, l, i, vec(norm_mix_pre[l]), *even_w,
                                                          *rope_even, cast_items)
            gain = jnp.concatenate([jnp.tile(gla_norm[i], N_HEAD_SCAN), jnp.ones((A_V,), F32)]).reshape(1, 2 * A_V)
            s0_lat = jnp.concatenate([state_gla[:, i], state_ret[:, i]], axis=2)
            m_ctx, s_fin = _scan_call(tok, q, k, lf, lb, v, gt, gain)
            (m,) = _scan_call(tok, q, k, lf, lb, m_ctx, gt, gain, s0=s0_lat)
            new_gla.append(s_fin[:, :, :N_HEAD_SCAN])
            new_ret.append(s_fin[:, :, N_HEAD_SCAN:])
            w_out = w16[-2] if l + 1 < depth else w16[-1]
        else:
            win, wqb, wkvb = odd_w
            q, k, v, ckv, kpe = _mla_proj_call(tok, x, mods, l, i, vec(norm_mix_pre[l]), win, vec(q_a_norm[i]), wqb,
                                               vec(kv_a_norm[i]), wkvb, *rope_odd)
            past = cache_ckv.shape[2]
            kpe_pad = jnp.pad(cache_kpe[:, i].reshape(nb_lat * past, QK_ROPE), ((0, 0), (0, LANE - QK_ROPE)))
            k_c, v_c = _cache_expand_call(cache_ckv[:, i].reshape(nb_lat * past, KV_LORA), kpe_pad, wkvb, i)
            m = _attn_lat_call(tok, q, k, _attn_ctx_call(tok, q, k, v), k_c, v_c)
            new_ckv.append(ckv[:tok.n_ctx].reshape(nb_ctx, l_ctx, KV_LORA))
            new_kpe.append(kpe[:tok.n_ctx, :QK_ROPE].reshape(nb_ctx, l_ctx, QK_ROPE))
            w_out = w16[-1]
        w1, w2 = w16[2 * (l % 2):2 * (l % 2) + 2]
        x = _mix_mlp_call(tok, m, x, mods, l, w_out, vec(norm_mix_post[l]), vec(norm_mlp_pre[l]), w1, w2,
                          vec(norm_mlp_post[l]), split_out=(l == depth - 1))
        x = x[0] if len(x) == 1 else tuple(x)

    return (x[0].reshape(nb_ctx, l_ctx, d), x[1].reshape(nb_lat, l_lat, d),
            jnp.stack(new_ckv, axis=1), jnp.stack(new_kpe, axis=1),
            jnp.stack(new_gla, axis=1), jnp.stack(new_ret, axis=1))
```

```python
import functools

import numpy as np
import jax
import jax.numpy as jnp
from jax import lax
from jax.experimental import pallas as pl
from jax.experimental.pallas import tpu as pltpu

F32 = jnp.float32
BF16 = jnp.bfloat16

EPS = 1e-6
ROPE_BASE = 10000.0
GRID_W = 64
CHUNK = 64
GATE_RANK = 16
GATE_NORM = 16.0
N_HEAD_SCAN = 4
DK = 64
DV = 128
H_C = 8
Q_LORA = 256
KV_LORA = 256
QK_NOPE = 128
QK_ROPE = 64
V_HEAD_C = 128
LANE = 128
MOD_ROWS = 16
EVEN_TILE = 512
MLA_TILE = 1024

VMEM_LIMIT = 56 * 1024 * 1024


def _cparams(sem):
    return pltpu.CompilerParams(dimension_semantics=sem, vmem_limit_bytes=VMEM_LIMIT)


def _dot(a, b):
    return jnp.dot(a, b, preferred_element_type=F32)


def _dot_nt(a, b):
    return lax.dot_general(a, b, (((1,), (1,)), ((), ())), preferred_element_type=F32)


def _dot_tn(a, b):
    return lax.dot_general(a, b, (((0,), (0,)), ((), ())), preferred_element_type=F32)


def _rms(x):
    return x * lax.rsqrt(jnp.mean(x * x, axis=-1, keepdims=True) + EPS)


def _silu(x):
    return x * jax.nn.sigmoid(x)


def _full(shape):
    n = len(shape)
    return pl.BlockSpec(shape, lambda *_: (0,) * n, pipeline_mode=pl.Buffered(1))


def _layer_block(w, idx):
    tail = (0,) * (w.ndim - 1)
    return pl.BlockSpec((None,) + w.shape[1:], lambda *_: (idx,) + tail, pipeline_mode=pl.Buffered(1))


def _side_cast_specs(items, steps):
    nb = 1 << (steps.bit_length() - 1)
    block = lambda i: jnp.minimum(i, nb - 1)
    ins = [pl.BlockSpec((None, w.shape[1] // nb, w.shape[2]), lambda i, layer=layer: (layer, block(i), 0))
           for w, layer in items]
    outs = [pl.BlockSpec((w.shape[1] // nb, w.shape[2]), lambda i: (block(i), 0)) for w, _ in items]
    shapes = [jax.ShapeDtypeStruct(w.shape[1:], BF16) for w, _ in items]
    return ins, outs, shapes


def _side_cast(in_refs, out_refs):
    for src, dst in zip(in_refs, out_refs):
        dst[...] = src[...].astype(BF16)


def _ada_body(cond_ref, w_ref, b_ref, o_ref):
    s = _silu(cond_ref[...]).astype(BF16)
    o_ref[...] = _dot(s, w_ref[...].astype(BF16)) + b_ref[...]


def _ada_call(cond, w_ada, b_ada):
    depth, d, n = w_ada.shape
    tn = 1536
    return pl.pallas_call(
        _ada_body,
        grid=(depth, n // tn),
        in_specs=[
            pl.BlockSpec((MOD_ROWS, d), lambda l, j: (0, 0)),
            pl.BlockSpec((None, d, tn), lambda l, j: (l, 0, j)),
            pl.BlockSpec((None, 1, tn), lambda l, j: (l, 0, j)),
        ],
        out_specs=pl.BlockSpec((None, MOD_ROWS, tn), lambda l, j: (l, 0, j)),
        out_shape=jax.ShapeDtypeStruct((depth, MOD_ROWS, n), F32),
        compiler_params=_cparams(("arbitrary", "arbitrary")),
        name="ada_mod",
    )(cond, w_ada, b_ada.reshape(depth, 1, n))


class _Tokens:
    def __init__(self, nb_ctx, l_ctx, nb_lat, l_lat, d):
        self.nb_ctx, self.l_ctx, self.nb_lat, self.l_lat, self.d = nb_ctx, l_ctx, nb_lat, l_lat, d
        self.n_ctx = nb_ctx * l_ctx
        self.n_lat = nb_lat * l_lat
        self.n = self.n_ctx + self.n_lat
        self.ctx_row = nb_lat

    def tile(self, want):
        t = want
        while self.n_ctx % t or self.l_lat % t:
            t //= 2
        return t

    def mod_spec(self, layer, chunk, tm):
        n_ctx, l_lat, ctx_row = self.n_ctx, self.l_lat, self.ctx_row

        def idx(i, *_):
            start = i * tm
            row = jnp.where(start < n_ctx, ctx_row, (start - n_ctx) // l_lat)
            return (layer, row, chunk, 0, 0)

        return pl.BlockSpec((None, None, None, 1, self.d), idx)

    def x_specs(self, x, tm):
        if not isinstance(x, tuple):
            return [pl.BlockSpec((tm, self.d), lambda i: (i, 0))], [x]
        nct = self.n_ctx // tm
        return [pl.BlockSpec((tm, self.d), lambda i: (jnp.minimum(i, nct - 1), 0)),
                pl.BlockSpec((tm, self.d), lambda i: (jnp.maximum(i - nct, 0), 0))], list(x)

    def rope_spec(self, tm):
        n_ctx, l_lat = self.n_ctx, self.l_lat

        def idx(i):
            start = i * tm
            return (jnp.where(start < n_ctx, 0, 1 + ((start - n_ctx) % l_lat) // tm), 0)

        return pl.BlockSpec((tm, LANE), idx)


def _read_x(x_refs, n_ctx_tiles):
    if len(x_refs) == 1:
        return x_refs[0][...]
    return jnp.where(pl.program_id(0) < n_ctx_tiles, x_refs[0][...], x_refs[1][...])


A_QK = N_HEAD_SCAN * DK
A_V = N_HEAD_SCAN * DV
E_QA, E_KA, E_VA, E_GA = 0, A_QK, 2 * A_QK, 2 * A_QK + A_V
E_QB = E_GA + A_V
E_KB = E_QB + A_QK
E_VB = E_KB + A_QK
E_GB = E_VB + A_V
E_GK = E_GB + A_V
E_QBS = E_GK + LANE
E_KBS = E_QBS + A_QK
E_COLS = E_KBS + A_QK


def _log_sigmoid(x):
    return jnp.minimum(x, 0.0) - jnp.log(1.0 + jnp.exp(-jnp.abs(x)))


def _even_proj_body(*refs, n_x, n_ctx_tiles, n_cast):
    x = _read_x(refs[:n_x], n_ctx_tiles)
    g_ref, sh_ref, sc_ref, w_ref, wg_ref, bg_ref, dec_ref, cc_ref, ss_ref = refs[n_x:n_x + 9]
    q_ref, k_ref, lf_ref, lb_ref, gt_ref, v_ref = refs[n_x + 9 + n_cast:n_x + 15 + n_cast]
    _side_cast(refs[n_x + 9:n_x + 9 + n_cast], refs[n_x + 15 + n_cast:])
    tm = x.shape[0]
    h = (_rms(x) * g_ref[...] * (1.0 + sc_ref[...]) + sh_ref[...]).astype(BF16)

    def proj(start, width):
        return _dot(h, w_ref[:, start:start + width])

    gk = proj(E_GK, LANE).astype(BF16)
    la = _log_sigmoid(_dot(gk, wg_ref[...]) + bg_ref[...]) * (1.0 / GATE_NORM)
    lf_ref[:, 0:A_QK] = la[:, 0:A_QK]
    lb_ref[:, 0:A_QK] = la[:, A_QK:2 * A_QK]
    log_g = -jnp.exp(dec_ref[...])
    lf_ref[:, A_QK:2 * A_QK] = jnp.broadcast_to(log_g[0:1, :], (tm, A_QK))
    lb_ref[:, A_QK:2 * A_QK] = jnp.broadcast_to(log_g[1:2, :], (tm, A_QK))
    cc = cc_ref[...]
    ss = ss_ref[...]
    scale = DK ** -0.5
    qb, qbs, kb, kbs = proj(E_QB, A_QK), proj(E_QBS, A_QK), proj(E_KB, A_QK), proj(E_KBS, A_QK)
    for j in range(A_QK // LANE):
        sl = slice(j * LANE, (j + 1) * LANE)
        o = A_QK + j * LANE
        q_ref[:, o:o + LANE] = qb[:, sl] * cc + qbs[:, sl] * ss
        k_ref[:, o:o + LANE] = (kb[:, sl] * cc + kbs[:, sl] * ss) * scale
    q_ref[:, 0:A_QK] = proj(E_QA, A_QK) * scale
    k_ref[:, 0:A_QK] = proj(E_KA, A_QK)
    v_ref[:, 0:A_V] = proj(E_VA, A_V).astype(BF16)
    v_ref[:, A_V:2 * A_V] = proj(E_VB, A_V).astype(BF16)
    gt_ref[:, 0:A_V] = proj(E_GA, A_V)
    gt_ref[:, A_V:2 * A_V] = proj(E_GB, A_V)


def _even_proj_call(tok, x, mods, layer, idx, g_pre, w, wg, bg, dec, cc, ss, cast_items):
    tm = tok.tile(EVEN_TILE)
    d = tok.d
    row = lambda i: (i, 0)
    outs = [
        jax.ShapeDtypeStruct((tok.n, 2 * A_QK), F32),
        jax.ShapeDtypeStruct((tok.n, 2 * A_QK), F32),
        jax.ShapeDtypeStruct((tok.n, 2 * A_QK), F32),
        jax.ShapeDtypeStruct((tok.n, 2 * A_QK), F32),
        jax.ShapeDtypeStruct((tok.n, 2 * A_V), F32),
        jax.ShapeDtypeStruct((tok.n, 2 * A_V), BF16),
    ]
    x_specs, x_args = tok.x_specs(x, tm)
    cast_in, cast_out, cast_shape = _side_cast_specs(cast_items, tok.n // tm)
    return pl.pallas_call(
        functools.partial(_even_proj_body, n_x=len(x_args), n_ctx_tiles=tok.n_ctx // tm, n_cast=len(cast_items)),
        grid=(tok.n // tm,),
        in_specs=x_specs + [
            _full((1, d)),
            tok.mod_spec(layer, 0, tm),
            tok.mod_spec(layer, 1, tm),
            _layer_block(w, idx), _layer_block(wg, idx), _layer_block(bg, idx), _layer_block(dec, idx),
            tok.rope_spec(tm),
            tok.rope_spec(tm),
        ] + cast_in,
        out_specs=[pl.BlockSpec((tm, o.shape[1]), row) for o in outs] + cast_out,
        out_shape=outs + cast_shape,
        compiler_params=_cparams(("arbitrary",)),
        name="even_proj",
    )(*x_args, g_pre, mods, mods, w, wg, bg, dec, cc, ss, *[stack for stack, _ in cast_items])


SCAN_GROUP = 8


def _chunk_cumsum(x):
    row = lax.broadcasted_iota(jnp.int32, x.shape, 0) % CHUNK
    s = 1
    while s < CHUNK:
        x = x + jnp.where(row >= s, pltpu.roll(x, s, axis=0), 0.0)
        s *= 2
    return x


def _scan_body(q_ref, k_ref, lf_ref, lb_ref, v_ref, gt_ref, gain_ref, *rest, seq_len, group, context):
    if context:
        m_ref, sfin_ref, st_ref, o_acc = rest
    else:
        s0_ref, m_ref, st_ref, o_acc = rest
    C = CHUNK
    blk = group * C
    nblk = seq_len // blk
    pair_w = 2 * DK
    head0 = lax.broadcasted_iota(jnp.int32, (blk, pair_w), 1) < DK
    t_in = lax.broadcasted_iota(jnp.int32, (C, pair_w), 0)
    j_in = lax.broadcasted_iota(jnp.int32, (C, pair_w), 1) % DK
    keep_fwd = t_in >= j_in
    keep_bwd = t_in <= j_in
    on_diag = ((lax.broadcasted_iota(jnp.int32, (2 * DV, pair_w), 0) < DV)
               == (lax.broadcasted_iota(jnp.int32, (2 * DV, pair_w), 1) < DK))
    zeros_v = jnp.zeros((C, DV), BF16)

    if context:
        st_ref[...] = jnp.zeros(st_ref.shape, F32)
    else:
        zeros_s = jnp.zeros((DK, DV), F32)
        for d in range(2):
            both = jnp.concatenate([jnp.concatenate([s0_ref[d, 0], zeros_s], axis=1),
                                    jnp.concatenate([zeros_s, s0_ref[d, 1]], axis=1)], axis=0)
            st_ref[d] = both.T

    def one_direction(r0, d, log_ref, keep, reverse):
        g = log_ref[pl.ds(r0, blk), :]
        b = _chunk_cumsum(g)
        tots = [b[C * j + C - 1:C * j + C, :] for j in range(group)]
        totb = jnp.concatenate([jnp.broadcast_to(t, (C, pair_w)) for t in tots], axis=0)
        if reverse:
            b = totb - b + g
        q = q_ref[pl.ds(r0, blk), :]
        k = k_ref[pl.ds(r0, blk), :]
        vblk = v_ref[pl.ds(r0, blk), :]
        q_dec = (q * jnp.exp(b)).astype(BF16)
        k_inv = k * jnp.exp(-b)
        k_up = (k * jnp.exp(totb - b)).astype(BF16)
        k_inv0 = jnp.where(head0, k_inv, 0.0).astype(BF16)
        k_inv1 = jnp.where(head0, 0.0, k_inv).astype(BF16)
        st = st_ref[d]
        outs = [None] * group
        for j in (reversed(range(group)) if reverse else range(group)):
            sl = slice(C * j, C * (j + 1))
            k_bd = jnp.concatenate([k_inv0[sl], k_inv1[sl]], axis=0)
            a = jnp.where(keep, _dot_nt(q_dec[sl], k_bd), 0.0).astype(BF16)
            vc = vblk[sl]
            v_bd = jnp.concatenate([jnp.concatenate([vc[:, :DV], zeros_v], axis=1),
                                    jnp.concatenate([zeros_v, vc[:, DV:]], axis=1)], axis=0)
            outs[j] = _dot_nt(q_dec[sl], st.astype(BF16)) + _dot(a, v_bd)
            st = st * jnp.exp(tots[j]) + jnp.where(on_diag, _dot_tn(vc, k_up[sl]), 0.0)
        st_ref[d] = st
        return jnp.concatenate(outs, axis=0)

    def emit(r0, o):
        gate = _silu(gt_ref[pl.ds(r0, blk), :])
        gain = gain_ref[...]
        for hh in range(2):
            sl = slice(hh * DV, (hh + 1) * DV)
            m_ref[pl.ds(r0, blk), sl] = (_rms(o[:, sl]) * gain[:, sl] * gate[:, sl]).astype(BF16)

    def body(c, carry, second_half):
        rf = pl.multiple_of(c * blk, blk)
        rb = pl.multiple_of((nblk - 1 - c) * blk, blk)
        o_f = one_direction(rf, 0, lf_ref, keep_fwd, False)
        o_b = one_direction(rb, 1, lb_ref, keep_bwd, True)
        if second_half:
            emit(rf, o_acc[pl.ds(rf, blk), :] + o_f)
            emit(rb, o_acc[pl.ds(rb, blk), :] + o_b)
        else:
            o_acc[pl.ds(rf, blk), :] = o_f
            o_acc[pl.ds(rb, blk), :] = o_b
        return carry

    if nblk == 1:
        emit(0, one_direction(0, 0, lf_ref, keep_fwd, False) + one_direction(0, 1, lb_ref, keep_bwd, True))
    else:
        lax.fori_loop(0, nblk // 2, functools.partial(body, second_half=False), 0)
        lax.fori_loop(nblk // 2, nblk, functools.partial(body, second_half=True), 0)

    if context:
        for d in range(2):
            both = st_ref[d].T
            sfin_ref[d, 0] = both[0:DK, 0:DV]
            sfin_ref[d, 1] = both[DK:2 * DK, DV:2 * DV]


def _scan_call(tok, q, k, lf, lb, v, gt, gain, s0=None):
    context = s0 is None
    if context:
        nb, seq_len, blk0 = tok.nb_ctx, tok.l_ctx, 0
    else:
        nb, seq_len, blk0 = tok.nb_lat, tok.l_lat, tok.n_ctx // tok.l_lat
    pairs = q.shape[1] // (2 * DK)
    group = min(SCAN_GROUP, seq_len // CHUNK)
    seq = lambda b, p: (blk0 + b, p)
    st_spec = pl.BlockSpec((None, 2, 2, DK, DV), lambda b, p: (b, 0, p, 0, 0))
    in_specs = [
        pl.BlockSpec((seq_len, 2 * DK), seq),
        pl.BlockSpec((seq_len, 2 * DK), seq),
        pl.BlockSpec((seq_len, 2 * DK), seq),
        pl.BlockSpec((seq_len, 2 * DK), seq),
        pl.BlockSpec((seq_len, 2 * DV), seq),
        pl.BlockSpec((seq_len, 2 * DV), seq),
        pl.BlockSpec((1, 2 * DV), lambda b, p: (0, p)),
    ]
    args = [q, k, lf, lb, v, gt, gain]
    out_shape = [jax.ShapeDtypeStruct(v.shape, v.dtype)]
    out_specs = [pl.BlockSpec((seq_len, 2 * DV), seq)]
    if context:
        out_shape.append(jax.ShapeDtypeStruct((nb, 2, 2 * pairs, DK, DV), F32))
        out_specs.append(st_spec)
    else:
        in_specs.append(st_spec)
        args.append(s0)
    return pl.pallas_call(
        functools.partial(_scan_body, seq_len=seq_len, group=group, context=context),
        grid=(nb, pairs),
        in_specs=in_specs,
        out_specs=out_specs,
        out_shape=out_shape,
        input_output_aliases={4: 0},
        scratch_shapes=[pltpu.VMEM((2, 2 * DV, 2 * DK), F32), pltpu.VMEM((seq_len, 2 * DV), F32)],
        compiler_params=_cparams(("arbitrary", "arbitrary")),
        name="scan_ctx" if context else "scan_lat",
    )(*args)


MIX_ROW_GROUPS = 4


def _mix_mlp_body(*refs, n_x, n_out, n_ctx_tiles):
    x = _read_x(refs[:n_x], n_ctx_tiles)
    (m_ref, wo_ref, gmix_ref, gate1_ref, gpre_ref, sh_ref, sc_ref, w1_ref, w2_ref,
     gpost_ref, gate2_ref) = refs[n_x:len(refs) - n_out]
    out_refs = refs[len(refs) - n_out:]
    rows = x.shape[0] // MIX_ROW_GROUPS
    x1_parts, h_parts = [], []
    for r0 in range(0, x.shape[0], rows):
        y = _dot(m_ref[r0:r0 + rows, :], wo_ref[...])
        x1_g = x[r0:r0 + rows] + gate1_ref[...] * (_rms(y) * gmix_ref[...])
        x1_parts.append(x1_g)
        h_parts.append((_rms(x1_g) * gpre_ref[...] * (1.0 + sc_ref[...]) + sh_ref[...]).astype(BF16))
    x1 = jnp.concatenate(x1_parts, axis=0)
    h = jnp.concatenate(h_parts, axis=0)
    u = jnp.maximum(_dot(h, w1_ref[...]), 0.0)
    z = _dot((u * u).astype(BF16), w2_ref[...])
    res = x1 + gate2_ref[...] * (_rms(z) * gpost_ref[...])
    if n_out == 1:
        out_refs[0][...] = res
    else:
        is_ctx = pl.program_id(0) < n_ctx_tiles

        @pl.when(is_ctx)
        def _():
            out_refs[0][...] = res

        @pl.when(jnp.logical_not(is_ctx))
        def _():
            out_refs[1][...] = res


def _mix_mlp_call(tok, m, x, mods, layer, w_out, g_mix, g_pre, w1, w2, g_post, split_out=False):
    tm = tok.tile(512)
    d = tok.d
    row = lambda i: (i, 0)
    x_specs, x_args = tok.x_specs(x, tm)
    nct = tok.n_ctx // tm
    if split_out:
        out_specs = [pl.BlockSpec((tm, d), lambda i: (jnp.minimum(i, nct - 1), 0)),
                     pl.BlockSpec((tm, d), lambda i: (jnp.maximum(i - nct, 0), 0))]
        out_shape = [jax.ShapeDtypeStruct((tok.n_ctx, d), F32), jax.ShapeDtypeStruct((tok.n_lat, d), F32)]
    else:
        out_specs = [pl.BlockSpec((tm, d), row)]
        out_shape = [jax.ShapeDtypeStruct((tok.n, d), F32)]
    return pl.pallas_call(
        functools.partial(_mix_mlp_body, n_x=len(x_args), n_out=len(out_shape), n_ctx_tiles=nct),
        grid=(tok.n // tm,),
        in_specs=x_specs + [
            pl.BlockSpec((tm, m.shape[1]), row),
            _full(w_out.shape),
            _full((1, d)),
            tok.mod_spec(layer, 2, tm),
            _full((1, d)),
            tok.mod_spec(layer, 3, tm),
            tok.mod_spec(layer, 4, tm),
            _full(w1.shape),
            _full(w2.shape),
            _full((1, d)),
            tok.mod_spec(layer, 5, tm),
        ],
        out_specs=out_specs,
        out_shape=out_shape,
        compiler_params=_cparams(("arbitrary",)),
        name="mix_mlp",
    )(*x_args, m, w_out, g_mix, mods, g_pre, mods, mods, w1, w2, g_post, mods)


HEAD_W = 2 * LANE
O_QLAT, O_CKV, O_KPE, O_KPES, O_COLS = 0, Q_LORA, Q_LORA + KV_LORA, Q_LORA + KV_LORA + LANE, Q_LORA + KV_LORA + 2 * LANE
QB_NOPE, QB_ROPE, QB_SWAP = 0, H_C * LANE, 2 * H_C * LANE


def _expand_kv(cb, kper, wkvb_ref, k_ref, v_ref):
    for hp in range(H_C // 2):
        nope2 = _dot(cb, wkvb_ref[:, hp * 2 * LANE:(hp + 1) * 2 * LANE])
        for j in range(2):
            hh = 2 * hp + j
            k_ref[:, hh * HEAD_W:hh * HEAD_W + LANE] = nope2[:, j * LANE:(j + 1) * LANE].astype(BF16)
            k_ref[:, hh * HEAD_W + LANE:(hh + 1) * HEAD_W] = kper
    v_ref[...] = _dot(cb, wkvb_ref[:, H_C * LANE:2 * H_C * LANE]).astype(BF16)


def _mla_proj_body(x_ref, g_ref, sh_ref, sc_ref, win_ref, qn_ref, wqb_ref, kvn_ref, wkvb_ref, cc_ref, ss_ref,
                   q_ref, k_ref, v_ref, ckv_ref, kpe_ref):
    h =(_rms(x_ref[...]) * g_ref[...] * (1.0 + sc_ref[...]) + sh_ref[...]).astype(BF16)
    cc = cc_ref[...]
    ss = ss_ref[...]
    qn = (_rms(_dot(h, win_ref[:, O_QLAT:O_QLAT + Q_LORA])) * qn_ref[...]).astype(BF16)
    ckvn = _rms(_dot(h, win_ref[:, O_CKV:O_CKV + KV_LORA])) * kvn_ref[...]
    kpe2 = _dot(h, win_ref[:, O_KPE:O_KPE + 2 * LANE])
    kpe = kpe2[:, 0:LANE]
    kper = (kpe * cc + kpe2[:, LANE:2 * LANE] * ss).astype(BF16)
    ckv_ref[...] = ckvn
    kpe_ref[...] = kpe
    for hp in range(H_C // 2):
        o = hp * 2 * LANE
        nope2 = _dot(qn, wqb_ref[:, QB_NOPE + o:QB_NOPE + o + 2 * LANE])
        rope2 = _dot(qn, wqb_ref[:, QB_ROPE + o:QB_ROPE + o + 2 * LANE])
        swap2 = _dot(qn, wqb_ref[:, QB_SWAP + o:QB_SWAP + o + 2 * LANE])
        for j in range(2):
            hh = 2 * hp + j
            sl = slice(j * LANE, (j + 1) * LANE)
            rot = rope2[:, sl] * cc + swap2[:, sl] * ss
            q_ref[:, hh * HEAD_W:hh * HEAD_W + LANE] = (nope2[:, sl] * ATTN_Q_SCALE).astype(BF16)
            q_ref[:, hh * HEAD_W + LANE:(hh + 1) * HEAD_W] = (rot * ATTN_Q_SCALE).astype(BF16)
    _expand_kv(ckvn.astype(BF16), kper, wkvb_ref, k_ref, v_ref)


def _mla_proj_call(tok, x, mods, layer, idx, g_pre, win, qn, wqb, kvn, wkvb, cc, ss):
    tm = tok.tile(MLA_TILE)
    d = tok.d
    row = lambda i: (i, 0)
    outs = [
        jax.ShapeDtypeStruct((tok.n, H_C * HEAD_W), BF16),
        jax.ShapeDtypeStruct((tok.n, H_C * HEAD_W), BF16),
        jax.ShapeDtypeStruct((tok.n, H_C * V_HEAD_C), BF16),
        jax.ShapeDtypeStruct((tok.n, KV_LORA), F32),
        jax.ShapeDtypeStruct((tok.n, LANE), F32),
    ]
    return pl.pallas_call(
        _mla_proj_body,
        grid=(tok.n // tm,),
        in_specs=[
            pl.BlockSpec((tm, d), row),
            _full((1, d)),
            tok.mod_spec(layer, 0, tm),
            tok.mod_spec(layer, 1, tm),
            _layer_block(win, idx), _full(qn.shape), _layer_block(wqb, idx), _full(kvn.shape),
            _layer_block(wkvb, idx),
            tok.rope_spec(tm),
            tok.rope_spec(tm),
        ],
        out_specs=[pl.BlockSpec((tm, o.shape[1]), row) for o in outs],
        out_shape=outs,
        compiler_params=_cparams(("arbitrary",)),
        name="mla_proj",
    )(x, g_pre, mods, mods, win, qn, wqb, kvn, wkvb, cc, ss)


def _cache_expand_body(ckv_ref, kpe_ref, wkvb_ref, k_ref, v_ref):
    _expand_kv(ckv_ref[...].astype(BF16), kpe_ref[...].astype(BF16), wkvb_ref, k_ref, v_ref)


def _cache_expand_call(ckv, kpe_pad, wkvb, idx):
    n = ckv.shape[0]
    tm = 512
    while n % tm:
        tm //= 2
    row = lambda i: (i, 0)
    outs = [jax.ShapeDtypeStruct((n, H_C * HEAD_W), BF16), jax.ShapeDtypeStruct((n, H_C * V_HEAD_C), BF16)]
    return pl.pallas_call(
        _cache_expand_body,
        grid=(n // tm,),
        in_specs=[pl.BlockSpec((tm, KV_LORA), row), pl.BlockSpec((tm, LANE), row), _layer_block(wkvb, idx)],
        out_specs=[pl.BlockSpec((tm, o.shape[1]), row) for o in outs],
        out_shape=outs,
        compiler_params=_cparams(("arbitrary",)),
        name="cache_expand",
    )(ckv, kpe_pad, wkvb)


ATTN_TQ = 512
ATTN_TK = 512
ATTN_HEADS_PER_STEP = 2
ATTN_Q_SCALE = (QK_NOPE + QK_ROPE) ** -0.5 * float(np.log2(np.e))


def _softmax_pv(s, values):
    p = jnp.exp2(s - jnp.max(s, axis=-1, keepdims=True))
    den = jnp.sum(p, axis=-1, keepdims=True)
    p = p.astype(BF16)
    acc = functools.reduce(jnp.add, [_dot(p[:, k0:k0 + vb.shape[0]], vb) for vb, k0 in values])
    return (acc / den).astype(BF16)


def _attn_ctx_body(q_ref, k_ref, v_ref, o_ref):
    for hh in range(H_C):
        s = _dot_nt(q_ref[:, hh * HEAD_W:(hh + 1) * HEAD_W], k_ref[:, hh * HEAD_W:(hh + 1) * HEAD_W])
        o_ref[:, hh * V_HEAD_C:(hh + 1) * V_HEAD_C] = _softmax_pv(s, [(v_ref[:, hh * V_HEAD_C:(hh + 1) * V_HEAD_C], 0)])


def _lane_groups(x):
    return [x[:, g:g + LANE] for g in range(0, x.shape[1], LANE)]


def _attn_lat_body(q_ref, kc_ref, vc_ref, k_ref, v_ref, o_ref, sa_ref, sb_ref, ma_ref, mb_ref, vx_ref):
    past, n_self = kc_ref.shape[0], k_ref.shape[0]
    heads = vx_ref.shape[0]
    tq = sa_ref.shape[0]
    n = q_ref.shape[0] // tq
    tk = min(ATTN_TK, n_self)
    blocks = [(kc_ref, 0, past, 0)] + [(k_ref, k0, tk, past + k0) for k0 in range(0, n_self, tk)]

    for h in range(heads):
        vx_ref[h, 0:past, 0:V_HEAD_C] = vc_ref[:, h * V_HEAD_C:(h + 1) * V_HEAD_C]
        vx_ref[h, past:past + n_self, 0:V_HEAD_C] = v_ref[:, h * V_HEAD_C:(h + 1) * V_HEAD_C]
        vx_ref[h, :, V_HEAD_C:2 * V_HEAD_C] = jnp.ones((past + n_self, V_HEAD_C), BF16)

    def scores(h, i, s_ref, m_ref):
        q = q_ref[pl.ds(pl.multiple_of(i * tq, tq), tq), h * HEAD_W:(h + 1) * HEAD_W]
        mx = None
        for kk_ref, k0, size, col in blocks:
            s = _dot_nt(q, kk_ref[k0:k0 + size, h * HEAD_W:(h + 1) * HEAD_W])
            s_ref[:, col:col + size] = s
            mx = functools.reduce(jnp.maximum, _lane_groups(s) + ([] if mx is None else [mx]))
        m_ref[...] = mx

    def finish(h, i, s_ref, m_ref):
        m = jnp.max(m_ref[...], axis=-1, keepdims=True)
        acc = None
        for _, _, size, col in blocks:
            p = jnp.exp2(s_ref[:, col:col + size] - m).astype(BF16)
            pv = _dot(p, vx_ref[h, col:col + size, :])
            acc = pv if acc is None else acc + pv
        out = acc[:, 0:V_HEAD_C] / acc[:, V_HEAD_C:2 * V_HEAD_C]
        o_ref[pl.ds(pl.multiple_of(i * tq, tq), tq), h * V_HEAD_C:(h + 1) * V_HEAD_C] = out.astype(BF16)

    scores(0, 0, sa_ref, ma_ref)
    for h in range(heads):

        def two_tiles(j, carry, h=h):
            i = 2 * j
            scores(h, i + 1, sb_ref, mb_ref)
            finish(h, i, sa_ref, ma_ref)
            scores(h, i + 2, sa_ref, ma_ref)
            finish(h, i + 1, sb_ref, mb_ref)
            return carry

        lax.fori_loop(0, n // 2 - 1, two_tiles, 0)
        scores(h, n - 1, sb_ref, mb_ref)
        finish(h, n - 2, sa_ref, ma_ref)
        if h + 1 < heads:
            scores(h + 1, 0, sa_ref, ma_ref)
        finish(h, n - 1, sb_ref, mb_ref)


def _attn_ctx_call(tok, q, k, v):
    seq = lambda b: (b, 0)
    return pl.pallas_call(
        _attn_ctx_body,
        grid=(tok.nb_ctx,),
        in_specs=[pl.BlockSpec((tok.l_ctx, H_C * HEAD_W), seq),
                  pl.BlockSpec((tok.l_ctx, H_C * HEAD_W), seq),
                  pl.BlockSpec((tok.l_ctx, H_C * V_HEAD_C), seq)],
        out_specs=pl.BlockSpec((tok.l_ctx, H_C * V_HEAD_C), seq),
        out_shape=jax.ShapeDtypeStruct(v.shape, v.dtype),
        input_output_aliases={2: 0},
        compiler_params=_cparams(("arbitrary",)),
        name="attn_ctx",
    )(q, k, v)


def _attn_lat_call(tok, q, k, v, k_cache, v_cache):
    blk0 = tok.n_ctx // tok.l_lat
    past = k_cache.shape[0] // tok.nb_lat
    tq = min(ATTN_TQ, tok.l_lat // 2)
    hg = ATTN_HEADS_PER_STEP
    seq = lambda b, hh: (blk0 + b, hh)
    cache = lambda b, hh: (b, hh)
    return pl.pallas_call(
        _attn_lat_body,
        grid=(tok.nb_lat, H_C // hg),
        in_specs=[pl.BlockSpec((tok.l_lat, hg * HEAD_W), seq),
                  pl.BlockSpec((past, hg * HEAD_W), cache),
                  pl.BlockSpec((past, hg * V_HEAD_C), cache),
                  pl.BlockSpec((tok.l_lat, hg * HEAD_W), seq),
                  pl.BlockSpec((tok.l_lat, hg * V_HEAD_C), seq)],
        out_specs=pl.BlockSpec((tok.l_lat, hg * V_HEAD_C), seq),
        out_shape=jax.ShapeDtypeStruct(v.shape, v.dtype),
        input_output_aliases={4: 0},
        scratch_shapes=([pltpu.VMEM((tq, past + tok.l_lat), F32)] * 2 + [pltpu.VMEM((tq, LANE), F32)] * 2
                        + [pltpu.VMEM((hg, past + tok.l_lat, 2 * V_HEAD_C), BF16)]),
        compiler_params=_cparams(("arbitrary", "arbitrary")),
        name="attn_lat",
    )(q, k_cache, v_cache, k, v)


def _swap_halves(w, head_dim):
    lead = w.shape[:-1]
    halves = w.reshape(lead + (w.shape[-1] // head_dim, 2, head_dim // 2))
    return halves[..., ::-1, :].reshape(w.shape)


def _pad_cols(w, width):
    return jnp.pad(w, ((0, 0),) * (w.ndim - 1) + ((0, width - w.shape[-1]),))


def _even_weights(w_in, w_gk2, b_gk2):
    n = w_in.shape[0]
    sizes = (A_QK, A_QK, A_V, A_V, 2 * GATE_RANK, A_QK, A_QK, A_V, A_V)
    qa, ka, va, ga, gk, qb, kb, vb, gb = jnp.split(w_in, np.cumsum(sizes)[:-1].tolist(), axis=-1)
    w = jnp.concatenate([qa, ka, va, ga, qb, kb, vb, gb, _pad_cols(gk, LANE),
                         _swap_halves(qb, DK), _swap_halves(kb, DK)], axis=-1).astype(BF16)
    wg = jnp.zeros((n, LANE, 2 * A_QK), F32)
    wg = wg.at[:, 0:GATE_RANK, 0:A_QK].set(w_gk2[:, 0]).at[:, GATE_RANK:2 * GATE_RANK, A_QK:2 * A_QK].set(w_gk2[:, 1])
    bg = b_gk2.reshape(n, 1, 2 * A_QK)
    return w, wg.astype(BF16), bg


def _odd_weights(w_in, w_q_b, w_kv_b):
    n = w_in.shape[0]
    q_lat, ckv, kpe = w_in[..., :Q_LORA], w_in[..., Q_LORA:Q_LORA + KV_LORA], w_in[..., Q_LORA + KV_LORA:]
    win = jnp.concatenate([q_lat, ckv, _pad_cols(kpe, LANE), _pad_cols(_swap_halves(kpe, QK_ROPE), LANE)],
                          axis=-1).astype(BF16)
    wq = w_q_b.reshape(n, Q_LORA, H_C, QK_NOPE + QK_ROPE)
    nope = wq[..., :QK_NOPE].reshape(n, Q_LORA, H_C * QK_NOPE)
    rope = wq[..., QK_NOPE:]
    pad = lambda r: _pad_cols(r, LANE).reshape(n, Q_LORA, H_C * LANE)
    wqb = jnp.concatenate([nope, pad(rope), pad(_swap_halves(rope, QK_ROPE))], axis=-1).astype(BF16)
    wkv = w_kv_b.reshape(n, KV_LORA, H_C, QK_NOPE + V_HEAD_C)
    wkvb = jnp.concatenate([wkv[..., :QK_NOPE].reshape(n, KV_LORA, H_C * QK_NOPE),
                            wkv[..., QK_NOPE:].reshape(n, KV_LORA, H_C * V_HEAD_C)], axis=-1).astype(BF16)
    return win, wqb, wkvb


def _rope_tables(tok, tm):
    rows = tok.l_lat // GRID_W
    row = jnp.repeat(jnp.arange(rows), GRID_W).astype(F32)
    col = jnp.tile(jnp.arange(GRID_W), rows).astype(F32)
    n_freq = QK_ROPE // 4
    inv = ROPE_BASE ** (-jnp.arange(n_freq, dtype=F32) / n_freq)
    ang = jnp.concatenate([row[:, None] * inv, col[:, None] * inv], axis=-1)
    cos, sin = jnp.cos(ang), jnp.sin(ang)
    cc = jnp.tile(jnp.concatenate([cos, cos], axis=-1), (1, LANE // QK_ROPE))
    ss = jnp.tile(jnp.concatenate([-sin, sin], axis=-1), (1, LANE // QK_ROPE))
    cc = jnp.concatenate([jnp.ones((tm, LANE), F32), cc], axis=0)
    ss = jnp.concatenate([jnp.zeros((tm, LANE), F32), ss], axis=0)
    return cc, ss


def kernel(x_prompt, x_sample, cache_ckv, cache_kpe, state_gla, state_ret, c, c_ctx, w_ada, b_ada, norm_mix_pre, norm_mix_post, norm_mlp_pre, norm_mlp_post, w_in_even, w_gk2, b_gk2, gla_norm, ret_decay, w_out_even, w_in_odd, q_a_norm, w_q_b, kv_a_norm, w_kv_b, w_out_odd, w_mlp1, w_mlp2):
    nb_ctx, l_ctx, d = x_prompt.shape
    nb_lat, l_lat, _ = x_sample.shape
    depth = w_ada.shape[0]
    tok = _Tokens(nb_ctx, l_ctx, nb_lat, l_lat, d)
    assert nb_lat < MOD_ROWS and tok.n_ctx % l_lat == 0 and l_ctx % CHUNK == 0 and l_lat % (2 * CHUNK) == 0

    cond = jnp.concatenate([c, c_ctx[None, :], jnp.zeros((MOD_ROWS - nb_lat - 1, d), F32)], axis=0)
    mods = _ada_call(cond, w_ada, b_ada).reshape(depth, MOD_ROWS, 6, 1, d)
    rope_even = _rope_tables(tok, tok.tile(EVEN_TILE))
    rope_odd = _rope_tables(tok, tok.tile(MLA_TILE))
    x = (x_prompt.reshape(tok.n_ctx, d), x_sample.reshape(tok.n_lat, d))
    vec = lambda a: a.reshape(1, -1)
    mlp_w = (w_mlp1, w_mlp2)
    even_w = _even_weights(w_in_even, w_gk2, b_gk2) + (jnp.repeat(ret_decay, DK, axis=-1),)
    odd_w = _odd_weights(w_in_odd, w_q_b, w_kv_b)

    new_ckv, new_kpe, new_gla, new_ret = [], [], [], []
    for l in range(depth):
        i = l // 2
        if l % 2 == 0:
            cast_items = [(w, ll) for ll in range(l, min(l + 2, depth)) for w in mlp_w]
            cast_items += [(w_out_even, i)] + ([(w_out_odd, i)] if l + 1 < depth else [])
            q, k, lf, lb, gt, v, *w16 = _even_proj_call(tok, x, mods, l, i, vec(norm_mix_pre[l]), *even_w,
                                                          *rope_even, cast_items)
            gain = jnp.concatenate([jnp.tile(gla_norm[i], N_HEAD_SCAN), jnp.ones((A_V,), F32)]).reshape(1, 2 * A_V)
            s0_lat = jnp.concatenate([state_gla[:, i], state_ret[:, i]], axis=2)
            m_ctx, s_fin = _scan_call(tok, q, k, lf, lb, v, gt, gain)
            (m,) = _scan_call(tok, q, k, lf, lb, m_ctx, gt, gain, s0=s0_lat)
            new_gla.append(s_fin[:, :, :N_HEAD_SCAN])
            new_ret.append(s_fin[:, :, N_HEAD_SCAN:])
            w_out = w16[-2] if l + 1 < depth else w16[-1]
        else:
            win, wqb, wkvb = odd_w
            q, k, v, ckv, kpe = _mla_proj_call(tok, x, mods, l, i, vec(norm_mix_pre[l]), win, vec(q_a_norm[i]), wqb,
                                               vec(kv_a_norm[i]), wkvb, *rope_odd)
            past = cache_ckv.shape[2]
            kpe_pad = jnp.pad(cache_kpe[:, i].reshape(nb_lat * past, QK_ROPE), ((0, 0), (0, LANE - QK_ROPE)))
            k_c, v_c = _cache_expand_call(cache_ckv[:, i].reshape(nb_lat * past, KV_LORA), kpe_pad, wkvb, i)
            m = _attn_lat_call(tok, q, k, _attn_ctx_call(tok, q, k, v), k_c, v_c)
            new_ckv.append(ckv[:tok.n_ctx].reshape(nb_ctx, l_ctx, KV_LORA))
            new_kpe.append(kpe[:tok.n_ctx, :QK_ROPE].reshape(nb_ctx, l_ctx, QK_ROPE))
            w_out = w16[-1]
        w1, w2 = w16[2 * (l % 2):2 * (l % 2) + 2]
        x = _mix_mlp_call(tok, m, x, mods, l, w_out, vec(norm_mix_post[l]), vec(norm_mlp_pre[l]), w1, w2,
                          vec(norm_mlp_post[l]), split_out=(l == depth - 1))
        x = x[0] if len(x) == 1 else tuple(x)

    return (x[0].reshape(nb_ctx, l_ctx, d), x[1].reshape(nb_lat, l_lat, d),
            jnp.stack(new_ckv, axis=1), jnp.stack(new_kpe, axis=1),
            jnp.stack(new_gla, axis=1), jnp.stack(new_ret, axis=1))
```

```python
import functools

import numpy as np
import jax
import jax.numpy as jnp
from jax import lax
from jax.experimental import pallas as pl
from jax.experimental.pallas import tpu as pltpu

F32 = jnp.float32
BF16 = jnp.bfloat16

EPS = 1e-6
ROPE_BASE = 10000.0
GRID_W = 64
CHUNK = 64
GATE_RANK = 16
GATE_NORM = 16.0
N_HEAD_SCAN = 4
DK = 64
DV = 128
H_C = 8
Q_LORA = 256
KV_LORA = 256
QK_NOPE = 128
QK_ROPE = 64
V_HEAD_C = 128
LANE = 128
MOD_ROWS = 16
EVEN_TILE = 512
MLA_TILE = 1024

VMEM_LIMIT = 56 * 1024 * 1024


def _cparams(sem):
    return pltpu.CompilerParams(dimension_semantics=sem, vmem_limit_bytes=VMEM_LIMIT)


def _dot(a, b):
    return jnp.dot(a, b, preferred_element_type=F32)


def _dot_nt(a, b):
    return lax.dot_general(a, b, (((1,), (1,)), ((), ())), preferred_element_type=F32)


def _dot_tn(a, b):
    return lax.dot_general(a, b, (((0,), (0,)), ((), ())), preferred_element_type=F32)


def _rms(x):
    return x * lax.rsqrt(jnp.mean(x * x, axis=-1, keepdims=True) + EPS)


def _silu(x):
    return x * jax.nn.sigmoid(x)


def _full(shape):
    n = len(shape)
    return pl.BlockSpec(shape, lambda *_: (0,) * n, pipeline_mode=pl.Buffered(1))


def _layer_block(w, idx):
    tail = (0,) * (w.ndim - 1)
    return pl.BlockSpec((None,) + w.shape[1:], lambda *_: (idx,) + tail, pipeline_mode=pl.Buffered(1))


def _side_cast_specs(items, steps):
    nb = 1 << (steps.bit_length() - 1)
    block = lambda i: jnp.minimum(i, nb - 1)
    ins = [pl.BlockSpec((None, w.shape[1] // nb, w.shape[2]), lambda i, layer=layer: (layer, block(i), 0))
           for w, layer in items]
    outs = [pl.BlockSpec((w.shape[1] // nb, w.shape[2]), lambda i: (block(i), 0)) for w, _ in items]
    shapes = [jax.ShapeDtypeStruct(w.shape[1:], BF16) for w, _ in items]
    return ins, outs, shapes


def _side_cast(in_refs, out_refs):
    for src, dst in zip(in_refs, out_refs):
        dst[...] = src[...].astype(BF16)


def _ada_body(cond_ref, w_ref, b_ref, o_ref):
    s = _silu(cond_ref[...]).astype(BF16)
    o_ref[...] = _dot(s, w_ref[...].astype(BF16)) + b_ref[...]


def _ada_call(cond, w_ada, b_ada):
    depth, d, n = w_ada.shape
    tn = 1536
    return pl.pallas_call(
        _ada_body,
        grid=(depth, n // tn),
        in_specs=[
            pl.BlockSpec((MOD_ROWS, d), lambda l, j: (0, 0)),
            pl.BlockSpec((None, d, tn), lambda l, j: (l, 0, j)),
            pl.BlockSpec((None, 1, tn), lambda l, j: (l, 0, j)),
        ],
        out_specs=pl.BlockSpec((None, MOD_ROWS, tn), lambda l, j: (l, 0, j)),
        out_shape=jax.ShapeDtypeStruct((depth, MOD_ROWS, n), F32),
        compiler_params=_cparams(("arbitrary", "arbitrary")),
        name="ada_mod",
    )(cond, w_ada, b_ada.reshape(depth, 1, n))


class _Tokens:
    def __init__(self, nb_ctx, l_ctx, nb_lat, l_lat, d):
        self.nb_ctx, self.l_ctx, self.nb_lat, self.l_lat, self.d = nb_ctx, l_ctx, nb_lat, l_lat, d
        self.n_ctx = nb_ctx * l_ctx
        self.n_lat = nb_lat * l_lat
        self.n = self.n_ctx + self.n_lat
        self.ctx_row = nb_lat

    def tile(self, want):
        t = want
        while self.n_ctx % t or self.l_lat % t:
            t //= 2
        return t

    def mod_spec(self, layer, chunk, tm):
        n_ctx, l_lat, ctx_row = self.n_ctx, self.l_lat, self.ctx_row

        def idx(i, *_):
            start = i * tm
            row = jnp.where(start < n_ctx, ctx_row, (start - n_ctx) // l_lat)
            return (layer, row, chunk, 0, 0)

        return pl.BlockSpec((None, None, None, 1, self.d), idx)

    def x_specs(self, x, tm):
        if not isinstance(x, tuple):
            return [pl.BlockSpec((tm, self.d), lambda i: (i, 0))], [x]
        nct = self.n_ctx // tm
        return [pl.BlockSpec((tm, self.d), lambda i: (jnp.minimum(i, nct - 1), 0)),
                pl.BlockSpec((tm, self.d), lambda i: (jnp.maximum(i - nct, 0), 0))], list(x)

    def rope_spec(self, tm):
        n_ctx, l_lat = self.n_ctx, self.l_lat

        def idx(i):
            start = i * tm
            return (jnp.where(start < n_ctx, 0, 1 + ((start - n_ctx) % l_lat) // tm), 0)

        return pl.BlockSpec((tm, LANE), idx)


def _read_x(x_refs, n_ctx_tiles):
    if len(x_refs) == 1:
        return x_refs[0][...]
    return jnp.where(pl.program_id(0) < n_ctx_tiles, x_refs[0][...], x_refs[1][...])


A_QK = N_HEAD_SCAN * DK
A_V = N_HEAD_SCAN * DV
E_QA, E_KA, E_VA, E_GA = 0, A_QK, 2 * A_QK, 2 * A_QK + A_V
E_QB = E_GA + A_V
E_KB = E_QB + A_QK
E_VB = E_KB + A_QK
E_GB = E_VB + A_V
E_GK = E_GB + A_V
E_QBS = E_GK + LANE
E_KBS = E_QBS + A_QK
E_COLS = E_KBS + A_QK


def _log_sigmoid(x):
    return jnp.minimum(x, 0.0) - jnp.log(1.0 + jnp.exp(-jnp.abs(x)))


def _even_proj_body(*refs, n_x, n_ctx_tiles, n_cast):
    x = _read_x(refs[:n_x], n_ctx_tiles)
    g_ref, sh_ref, sc_ref, w_ref, wg_ref, bg_ref, dec_ref, cc_ref, ss_ref = refs[n_x:n_x + 9]
    q_ref, k_ref, lf_ref, lb_ref, gt_ref, v_ref = refs[n_x + 9 + n_cast:n_x + 15 + n_cast]
    _side_cast(refs[n_x + 9:n_x + 9 + n_cast], refs[n_x + 15 + n_cast:])
    tm = x.shape[0]
    h = (_rms(x) * g_ref[...] * (1.0 + sc_ref[...]) + sh_ref[...]).astype(BF16)

    def proj(start, width):
        return _dot(h, w_ref[:, start:start + width])

    gk = proj(E_GK, LANE).astype(BF16)
    la = _log_sigmoid(_dot(gk, wg_ref[...]) + bg_ref[...]) * (1.0 / GATE_NORM)
    lf_ref[:, 0:A_QK] = la[:, 0:A_QK]
    lb_ref[:, 0:A_QK] = la[:, A_QK:2 * A_QK]
    log_g = -jnp.exp(dec_ref[...])
    lf_ref[:, A_QK:2 * A_QK] = jnp.broadcast_to(log_g[0:1, :], (tm, A_QK))
    lb_ref[:, A_QK:2 * A_QK] = jnp.broadcast_to(log_g[1:2, :], (tm, A_QK))
    cc = cc_ref[...]
    ss = ss_ref[...]
    scale = DK ** -0.5
    qb, qbs, kb, kbs = proj(E_QB, A_QK), proj(E_QBS, A_QK), proj(E_KB, A_QK), proj(E_KBS, A_QK)
    for j in range(A_QK // LANE):
        sl = slice(j * LANE, (j + 1) * LANE)
        o = A_QK + j * LANE
        q_ref[:, o:o + LANE] = qb[:, sl] * cc + qbs[:, sl] * ss
        k_ref[:, o:o + LANE] = (kb[:, sl] * cc + kbs[:, sl] * ss) * scale
    q_ref[:, 0:A_QK] = proj(E_QA, A_QK) * scale
    k_ref[:, 0:A_QK] = proj(E_KA, A_QK)
    v_ref[:, 0:A_V] = proj(E_VA, A_V).astype(BF16)
    v_ref[:, A_V:2 * A_V] = proj(E_VB, A_V).astype(BF16)
    gt_ref[:, 0:A_V] = proj(E_GA, A_V)
    gt_ref[:, A_V:2 * A_V] = proj(E_GB, A_V)


def _even_proj_call(tok, x, mods, layer, idx, g_pre, w, wg, bg, dec, cc, ss, cast_items):
    tm = tok.tile(EVEN_TILE)
    d = tok.d
    row = lambda i: (i, 0)
    outs = [
        jax.ShapeDtypeStruct((tok.n, 2 * A_QK), F32),
        jax.ShapeDtypeStruct((tok.n, 2 * A_QK), F32),
        jax.ShapeDtypeStruct((tok.n, 2 * A_QK), F32),
        jax.ShapeDtypeStruct((tok.n, 2 * A_QK), F32),
        jax.ShapeDtypeStruct((tok.n, 2 * A_V), F32),
        jax.ShapeDtypeStruct((tok.n, 2 * A_V), BF16),
    ]
    x_specs, x_args = tok.x_specs(x, tm)
    cast_in, cast_out, cast_shape = _side_cast_specs(cast_items, tok.n // tm)
    return pl.pallas_call(
        functools.partial(_even_proj_body, n_x=len(x_args), n_ctx_tiles=tok.n_ctx // tm, n_cast=len(cast_items)),
        grid=(tok.n // tm,),
        in_specs=x_specs + [
            _full((1, d)),
            tok.mod_spec(layer, 0, tm),
            tok.mod_spec(layer, 1, tm),
            _layer_block(w, idx), _layer_block(wg, idx), _layer_block(bg, idx), _layer_block(dec, idx),
            tok.rope_spec(tm),
            tok.rope_spec(tm),
        ] + cast_in,
        out_specs=[pl.BlockSpec((tm, o.shape[1]), row) for o in outs] + cast_out,
        out_shape=outs + cast_shape,
        compiler_params=_cparams(("arbitrary",)),
        name="even_proj",
    )(*x_args, g_pre, mods, mods, w, wg, bg, dec, cc, ss, *[stack for stack, _ in cast_items])


SCAN_GROUP = 16


def _chunk_cumsum(x):
    row = lax.broadcasted_iota(jnp.int32, x.shape, 0) % CHUNK
    s = 1
    while s < CHUNK:
        x = x + jnp.where(row >= s, pltpu.roll(x, s, axis=0), 0.0)
        s *= 2
    return x


def _scan_body(q_ref, k_ref, lf_ref, lb_ref, v_ref, gt_ref, gain_ref, *rest, seq_len, group, context):
    if context:
        m_ref, sfin_ref, st_ref, o_acc = rest
    else:
        s0_ref, m_ref, st_ref, o_acc = rest
    C = CHUNK
    blk = group * C
    nblk = seq_len // blk
    pair_w = 2 * DK
    head0 = lax.broadcasted_iota(jnp.int32, (blk, pair_w), 1) < DK
    t_in = lax.broadcasted_iota(jnp.int32, (C, pair_w), 0)
    j_in = lax.broadcasted_iota(jnp.int32, (C, pair_w), 1) % DK
    keep_fwd = t_in >= j_in
    keep_bwd = t_in <= j_in
    on_diag = ((lax.broadcasted_iota(jnp.int32, (2 * DV, pair_w), 0) < DV)
               == (lax.broadcasted_iota(jnp.int32, (2 * DV, pair_w), 1) < DK))
    zeros_v = jnp.zeros((C, DV), BF16)

    if context:
        st_ref[...] = jnp.zeros(st_ref.shape, F32)
    else:
        zeros_s = jnp.zeros((DK, DV), F32)
        for d in range(2):
            both = jnp.concatenate([jnp.concatenate([s0_ref[d, 0], zeros_s], axis=1),
                                    jnp.concatenate([zeros_s, s0_ref[d, 1]], axis=1)], axis=0)
            st_ref[d] = both.T

    def one_direction(r0, d, log_ref, keep, reverse):
        g = log_ref[pl.ds(r0, blk), :]
        b = _chunk_cumsum(g)
        tots = [b[C * j + C - 1:C * j + C, :] for j in range(group)]
        totb = jnp.concatenate([jnp.broadcast_to(t, (C, pair_w)) for t in tots], axis=0)
        if reverse:
            b = totb - b + g
        q = q_ref[pl.ds(r0, blk), :]
        k = k_ref[pl.ds(r0, blk), :]
        vblk = v_ref[pl.ds(r0, blk), :]
        q_dec = (q * jnp.exp(b)).astype(BF16)
        k_inv = k * jnp.exp(-b)
        k_up = (k * jnp.exp(totb - b)).astype(BF16)
        k_inv0 = jnp.where(head0, k_inv, 0.0).astype(BF16)
        k_inv1 = jnp.where(head0, 0.0, k_inv).astype(BF16)
        st = st_ref[d]
        outs = [None] * group
        for j in (reversed(range(group)) if reverse else range(group)):
            sl = slice(C * j, C * (j + 1))
            k_bd = jnp.concatenate([k_inv0[sl], k_inv1[sl]], axis=0)
            a = jnp.where(keep, _dot_nt(q_dec[sl], k_bd), 0.0).astype(BF16)
            vc = vblk[sl]
            v_bd = jnp.concatenate([jnp.concatenate([vc[:, :DV], zeros_v], axis=1),
                                    jnp.concatenate([zeros_v, vc[:, DV:]], axis=1)], axis=0)
            outs[j] = _dot_nt(q_dec[sl], st.astype(BF16)) + _dot(a, v_bd)
            st = st * jnp.exp(tots[j]) + jnp.where(on_diag, _dot_tn(vc, k_up[sl]), 0.0)
        st_ref[d] = st
        return jnp.concatenate(outs, axis=0)

    def emit(r0, o):
        gate = _silu(gt_ref[pl.ds(r0, blk), :])
        gain = gain_ref[...]
        for hh in range(2):
            sl = slice(hh * DV, (hh + 1) * DV)
            m_ref[pl.ds(r0, blk), sl] = (_rms(o[:, sl]) * gain[:, sl] * gate[:, sl]).astype(BF16)

    def body(c, carry, second_half):
        rf = pl.multiple_of(c * blk, blk)
        rb = pl.multiple_of((nblk - 1 - c) * blk, blk)
        o_f = one_direction(rf, 0, lf_ref, keep_fwd, False)
        o_b = one_direction(rb, 1, lb_ref, keep_bwd, True)
        if second_half:
            emit(rf, o_acc[pl.ds(rf, blk), :] + o_f)
            emit(rb, o_acc[pl.ds(rb, blk), :] + o_b)
        else:
            o_acc[pl.ds(rf, blk), :] = o_f
            o_acc[pl.ds(rb, blk), :] = o_b
        return carry

    if nblk == 1:
        emit(0, one_direction(0, 0, lf_ref, keep_fwd, False) + one_direction(0, 1, lb_ref, keep_bwd, True))
    else:
        lax.fori_loop(0, nblk // 2, functools.partial(body, second_half=False), 0)
        lax.fori_loop(nblk // 2, nblk, functools.partial(body, second_half=True), 0)

    if context:
        for d in range(2):
            both = st_ref[d].T
            sfin_ref[d, 0] = both[0:DK, 0:DV]
            sfin_ref[d, 1] = both[DK:2 * DK, DV:2 * DV]


def _scan_call(tok, q, k, lf, lb, v, gt, gain, s0=None):
    context = s0 is None
    if context:
        nb, seq_len, blk0 = tok.nb_ctx, tok.l_ctx, 0
    else:
        nb, seq_len, blk0 = tok.nb_lat, tok.l_lat, tok.n_ctx // tok.l_lat
    pairs = q.shape[1] // (2 * DK)
    group = min(SCAN_GROUP, seq_len // CHUNK)
    seq = lambda b, p: (blk0 + b, p)
    st_spec = pl.BlockSpec((None, 2, 2, DK, DV), lambda b, p: (b, 0, p, 0, 0))
    in_specs = [
        pl.BlockSpec((seq_len, 2 * DK), seq),
        pl.BlockSpec((seq_len, 2 * DK), seq),
        pl.BlockSpec((seq_len, 2 * DK), seq),
        pl.BlockSpec((seq_len, 2 * DK), seq),
        pl.BlockSpec((seq_len, 2 * DV), seq),
        pl.BlockSpec((seq_len, 2 * DV), seq),
        pl.BlockSpec((1, 2 * DV), lambda b, p: (0, p)),
    ]
    args = [q, k, lf, lb, v, gt, gain]
    out_shape = [jax.ShapeDtypeStruct(v.shape, v.dtype)]
    out_specs = [pl.BlockSpec((seq_len, 2 * DV), seq)]
    if context:
        out_shape.append(jax.ShapeDtypeStruct((nb, 2, 2 * pairs, DK, DV), F32))
        out_specs.append(st_spec)
    else:
        in_specs.append(st_spec)
        args.append(s0)
    return pl.pallas_call(
        functools.partial(_scan_body, seq_len=seq_len, group=group, context=context),
        grid=(nb, pairs),
        in_specs=in_specs,
        out_specs=out_specs,
        out_shape=out_shape,
        input_output_aliases={4: 0},
        scratch_shapes=[pltpu.VMEM((2, 2 * DV, 2 * DK), F32), pltpu.VMEM((seq_len, 2 * DV), F32)],
        compiler_params=_cparams(("arbitrary", "arbitrary")),
        name="scan_ctx" if context else "scan_lat",
    )(*args)


MIX_ROW_GROUPS = 4


def _mix_mlp_body(*refs, n_x, n_out, n_ctx_tiles):
    x = _read_x(refs[:n_x], n_ctx_tiles)
    (m_ref, wo_ref, gmix_ref, gate1_ref, gpre_ref, sh_ref, sc_ref, w1_ref, w2_ref,
     gpost_ref, gate2_ref) = refs[n_x:len(refs) - n_out]
    out_refs = refs[len(refs) - n_out:]
    rows = x.shape[0] // MIX_ROW_GROUPS
    x1_parts, h_parts = [], []
    for r0 in range(0, x.shape[0], rows):
        y = _dot(m_ref[r0:r0 + rows, :], wo_ref[...])
        x1_g = x[r0:r0 + rows] + gate1_ref[...] * (_rms(y) * gmix_ref[...])
        x1_parts.append(x1_g)
        h_parts.append((_rms(x1_g) * gpre_ref[...] * (1.0 + sc_ref[...]) + sh_ref[...]).astype(BF16))
    x1 = jnp.concatenate(x1_parts, axis=0)
    h = jnp.concatenate(h_parts, axis=0)
    u = jnp.maximum(_dot(h, w1_ref[...]), 0.0)
    u = (u * u).astype(BF16)
    half = x.shape[0] // 2
    res = jnp.concatenate(
        [x1[r0:r0 + half] + gate2_ref[...] * (_rms(_dot(u[r0:r0 + half], w2_ref[...])) * gpost_ref[...])
         for r0 in (0, half)], axis=0)
    if n_out == 1:
        out_refs[0][...] = res
    else:
        is_ctx = pl.program_id(0) < n_ctx_tiles

        @pl.when(is_ctx)
        def _():
            out_refs[0][...] = res

        @pl.when(jnp.logical_not(is_ctx))
        def _():
            out_refs[1][...] = res


def _mix_mlp_call(tok, m, x, mods, layer, w_out, g_mix, g_pre, w1, w2, g_post, split_out=False):
    tm = tok.tile(512)
    d = tok.d
    row = lambda i: (i, 0)
    x_specs, x_args = tok.x_specs(x, tm)
    nct = tok.n_ctx // tm
    if split_out:
        out_specs = [pl.BlockSpec((tm, d), lambda i: (jnp.minimum(i, nct - 1), 0)),
                     pl.BlockSpec((tm, d), lambda i: (jnp.maximum(i - nct, 0), 0))]
        out_shape = [jax.ShapeDtypeStruct((tok.n_ctx, d), F32), jax.ShapeDtypeStruct((tok.n_lat, d), F32)]
    else:
        out_specs = [pl.BlockSpec((tm, d), row)]
        out_shape = [jax.ShapeDtypeStruct((tok.n, d), F32)]
    return pl.pallas_call(
        functools.partial(_mix_mlp_body, n_x=len(x_args), n_out=len(out_shape), n_ctx_tiles=nct),
        grid=(tok.n // tm,),
        in_specs=x_specs + [
            pl.BlockSpec((tm, m.shape[1]), row),
            _full(w_out.shape),
            _full((1, d)),
            tok.mod_spec(layer, 2, tm),
            _full((1, d)),
            tok.mod_spec(layer, 3, tm),
            tok.mod_spec(layer, 4, tm),
            _full(w1.shape),
            _full(w2.shape),
            _full((1, d)),
            tok.mod_spec(layer, 5, tm),
        ],
        out_specs=out_specs,
        out_shape=out_shape,
        compiler_params=_cparams(("arbitrary",)),
        name="mix_mlp",
    )(*x_args, m, w_out, g_mix, mods, g_pre, mods, mods, w1, w2, g_post, mods)


HEAD_W = 2 * LANE
O_QLAT, O_CKV, O_KPE, O_KPES, O_COLS = 0, Q_LORA, Q_LORA + KV_LORA, Q_LORA + KV_LORA + LANE, Q_LORA + KV_LORA + 2 * LANE
QB_NOPE, QB_ROPE, QB_SWAP = 0, H_C * LANE, 2 * H_C * LANE


def _expand_kv(cb, kper, wkvb_ref, k_ref, v_ref):
    for hp in range(H_C // 2):
        nope2 = _dot(cb, wkvb_ref[:, hp * 2 * LANE:(hp + 1) * 2 * LANE])
        for j in range(2):
            hh = 2 * hp + j
            k_ref[:, hh * HEAD_W:hh * HEAD_W + LANE] = nope2[:, j * LANE:(j + 1) * LANE].astype(BF16)
            k_ref[:, hh * HEAD_W + LANE:(hh + 1) * HEAD_W] = kper
    v_ref[...] = _dot(cb, wkvb_ref[:, H_C * LANE:2 * H_C * LANE]).astype(BF16)


def _mla_proj_body(x_ref, g_ref, sh_ref, sc_ref, win_ref, qn_ref, wqb_ref, kvn_ref, wkvb_ref, cc_ref, ss_ref,
                   q_ref, k_ref, v_ref, ckv_ref, kpe_ref):
    h =(_rms(x_ref[...]) * g_ref[...] * (1.0 + sc_ref[...]) + sh_ref[...]).astype(BF16)
    cc = cc_ref[...]
    ss = ss_ref[...]
    qn = (_rms(_dot(h, win_ref[:, O_QLAT:O_QLAT + Q_LORA])) * qn_ref[...]).astype(BF16)
    ckvn = _rms(_dot(h, win_ref[:, O_CKV:O_CKV + KV_LORA])) * kvn_ref[...]
    kpe2 = _dot(h, win_ref[:, O_KPE:O_KPE + 2 * LANE])
    kpe = kpe2[:, 0:LANE]
    kper = (kpe * cc + kpe2[:, LANE:2 * LANE] * ss).astype(BF16)
    ckv_ref[...] = ckvn
    kpe_ref[...] = kpe
    for hp in range(H_C // 2):
        o = hp * 2 * LANE
        nope2 = _dot(qn, wqb_ref[:, QB_NOPE + o:QB_NOPE + o + 2 * LANE])
        rope2 = _dot(qn, wqb_ref[:, QB_ROPE + o:QB_ROPE + o + 2 * LANE])
        swap2 = _dot(qn, wqb_ref[:, QB_SWAP + o:QB_SWAP + o + 2 * LANE])
        for j in range(2):
            hh = 2 * hp + j
            sl = slice(j * LANE, (j + 1) * LANE)
            rot = rope2[:, sl] * cc + swap2[:, sl] * ss
            q_ref[:, hh * HEAD_W:hh * HEAD_W + LANE] = (nope2[:, sl] * ATTN_Q_SCALE).astype(BF16)
            q_ref[:, hh * HEAD_W + LANE:(hh + 1) * HEAD_W] = (rot * ATTN_Q_SCALE).astype(BF16)
    _expand_kv(ckvn.astype(BF16), kper, wkvb_ref, k_ref, v_ref)


def _mla_proj_call(tok, x, mods, layer, idx, g_pre, win, qn, wqb, kvn, wkvb, cc, ss):
    tm = tok.tile(MLA_TILE)
    d = tok.d
    row = lambda i: (i, 0)
    outs = [
        jax.ShapeDtypeStruct((tok.n, H_C * HEAD_W), BF16),
        jax.ShapeDtypeStruct((tok.n, H_C * HEAD_W), BF16),
        jax.ShapeDtypeStruct((tok.n, H_C * V_HEAD_C), BF16),
        jax.ShapeDtypeStruct((tok.n, KV_LORA), F32),
        jax.ShapeDtypeStruct((tok.n, LANE), F32),
    ]
    return pl.pallas_call(
        _mla_proj_body,
        grid=(tok.n // tm,),
        in_specs=[
            pl.BlockSpec((tm, d), row),
            _full((1, d)),
            tok.mod_spec(layer, 0, tm),
            tok.mod_spec(layer, 1, tm),
            _layer_block(win, idx), _full(qn.shape), _layer_block(wqb, idx), _full(kvn.shape),
            _layer_block(wkvb, idx),
            tok.rope_spec(tm),
            tok.rope_spec(tm),
        ],
        out_specs=[pl.BlockSpec((tm, o.shape[1]), row) for o in outs],
        out_shape=outs,
        compiler_params=_cparams(("arbitrary",)),
        name="mla_proj",
    )(x, g_pre, mods, mods, win, qn, wqb, kvn, wkvb, cc, ss)


def _cache_expand_body(ckv_ref, kpe_ref, wkvb_ref, k_ref, v_ref):
    _expand_kv(ckv_ref[...].astype(BF16), kpe_ref[...].astype(BF16), wkvb_ref, k_ref, v_ref)


def _cache_expand_call(ckv, kpe_pad, wkvb, idx):
    n = ckv.shape[0]
    tm = 512
    while n % tm:
        tm //= 2
    row = lambda i: (i, 0)
    outs = [jax.ShapeDtypeStruct((n, H_C * HEAD_W), BF16), jax.ShapeDtypeStruct((n, H_C * V_HEAD_C), BF16)]
    return pl.pallas_call(
        _cache_expand_body,
        grid=(n // tm,),
        in_specs=[pl.BlockSpec((tm, KV_LORA), row), pl.BlockSpec((tm, LANE), row), _layer_block(wkvb, idx)],
        out_specs=[pl.BlockSpec((tm, o.shape[1]), row) for o in outs],
        out_shape=outs,
        compiler_params=_cparams(("arbitrary",)),
        name="cache_expand",
    )(ckv, kpe_pad, wkvb)


ATTN_TQ = 512
ATTN_TK = 512
ATTN_HEADS_PER_STEP = 2
ATTN_Q_SCALE = (QK_NOPE + QK_ROPE) ** -0.5 * float(np.log2(np.e))


def _softmax_pv(s, values):
    p = jnp.exp2(s - jnp.max(s, axis=-1, keepdims=True))
    den = jnp.sum(p, axis=-1, keepdims=True)
    p = p.astype(BF16)
    acc = functools.reduce(jnp.add, [_dot(p[:, k0:k0 + vb.shape[0]], vb) for vb, k0 in values])
    return (acc / den).astype(BF16)


def _attn_ctx_body(q_ref, k_ref, v_ref, o_ref):
    for hh in range(H_C):
        s = _dot_nt(q_ref[:, hh * HEAD_W:(hh + 1) * HEAD_W], k_ref[:, hh * HEAD_W:(hh + 1) * HEAD_W])
        o_ref[:, hh * V_HEAD_C:(hh + 1) * V_HEAD_C] = _softmax_pv(s, [(v_ref[:, hh * V_HEAD_C:(hh + 1) * V_HEAD_C], 0)])


def _lane_groups(x):
    return [x[:, g:g + LANE] for g in range(0, x.shape[1], LANE)]


def _attn_lat_body(q_ref, kc_ref, vc_ref, k_ref, v_ref, o_ref, sa_ref, sb_ref, ma_ref, mb_ref, vx_ref):
    past, n_self = kc_ref.shape[0], k_ref.shape[0]
    heads = vx_ref.shape[0]
    tq = sa_ref.shape[0]
    n = q_ref.shape[0] // tq
    tk = min(ATTN_TK, n_self)
    blocks = [(kc_ref, 0, past, 0)] + [(k_ref, k0, tk, past + k0) for k0 in range(0, n_self, tk)]

    for h in range(heads):
        vx_ref[h, 0:past, 0:V_HEAD_C] = vc_ref[:, h * V_HEAD_C:(h + 1) * V_HEAD_C]
        vx_ref[h, past:past + n_self, 0:V_HEAD_C] = v_ref[:, h * V_HEAD_C:(h + 1) * V_HEAD_C]
        vx_ref[h, :, V_HEAD_C:2 * V_HEAD_C] = jnp.ones((past + n_self, V_HEAD_C), BF16)

    def scores(h, i, s_ref, m_ref):
        q = q_ref[pl.ds(pl.multiple_of(i * tq, tq), tq), h * HEAD_W:(h + 1) * HEAD_W]
        mx = None
        for kk_ref, k0, size, col in blocks:
            s = _dot_nt(q, kk_ref[k0:k0 + size, h * HEAD_W:(h + 1) * HEAD_W])
            s_ref[:, col:col + size] = s
            mx = functools.reduce(jnp.maximum, _lane_groups(s) + ([] if mx is None else [mx]))
        m_ref[...] = mx

    def finish(h, i, s_ref, m_ref):
        m = jnp.max(m_ref[...], axis=-1, keepdims=True)
        acc = None
        for _, _, size, col in blocks:
            p = jnp.exp2(s_ref[:, col:col + size] - m).astype(BF16)
            pv = _dot(p, vx_ref[h, col:col + size, :])
            acc = pv if acc is None else acc + pv
        out = acc[:, 0:V_HEAD_C] / acc[:, V_HEAD_C:2 * V_HEAD_C]
        o_ref[pl.ds(pl.multiple_of(i * tq, tq), tq), h * V_HEAD_C:(h + 1) * V_HEAD_C] = out.astype(BF16)

    scores(0, 0, sa_ref, ma_ref)
    for h in range(heads):

        def two_tiles(j, carry, h=h):
            i = 2 * j
            scores(h, i + 1, sb_ref, mb_ref)
            finish(h, i, sa_ref, ma_ref)
            scores(h, i + 2, sa_ref, ma_ref)
            finish(h, i + 1, sb_ref, mb_ref)
            return carry

        lax.fori_loop(0, n // 2 - 1, two_tiles, 0)
        scores(h, n - 1, sb_ref, mb_ref)
        finish(h, n - 2, sa_ref, ma_ref)
        if h + 1 < heads:
            scores(h + 1, 0, sa_ref, ma_ref)
        finish(h, n - 1, sb_ref, mb_ref)


def _attn_ctx_call(tok, q, k, v):
    seq = lambda b: (b, 0)
    return pl.pallas_call(
        _attn_ctx_body,
        grid=(tok.nb_ctx,),
        in_specs=[pl.BlockSpec((tok.l_ctx, H_C * HEAD_W), seq),
                  pl.BlockSpec((tok.l_ctx, H_C * HEAD_W), seq),
                  pl.BlockSpec((tok.l_ctx, H_C * V_HEAD_C), seq)],
        out_specs=pl.BlockSpec((tok.l_ctx, H_C * V_HEAD_C), seq),
        out_shape=jax.ShapeDtypeStruct(v.shape, v.dtype),
        input_output_aliases={2: 0},
        compiler_params=_cparams(("arbitrary",)),
        name="attn_ctx",
    )(q, k, v)


def _attn_lat_call(tok, q, k, v, k_cache, v_cache):
    blk0 = tok.n_ctx // tok.l_lat
    past = k_cache.shape[0] // tok.nb_lat
    tq = min(ATTN_TQ, tok.l_lat // 2)
    hg = ATTN_HEADS_PER_STEP
    seq = lambda b, hh: (blk0 + b, hh)
    cache = lambda b, hh: (b, hh)
    return pl.pallas_call(
        _attn_lat_body,
        grid=(tok.nb_lat, H_C // hg),
        in_specs=[pl.BlockSpec((tok.l_lat, hg * HEAD_W), seq),
                  pl.BlockSpec((past, hg * HEAD_W), cache),
                  pl.BlockSpec((past, hg * V_HEAD_C), cache),
                  pl.BlockSpec((tok.l_lat, hg * HEAD_W), seq),
                  pl.BlockSpec((tok.l_lat, hg * V_HEAD_C), seq)],
        out_specs=pl.BlockSpec((tok.l_lat, hg * V_HEAD_C), seq),
        out_shape=jax.ShapeDtypeStruct(v.shape, v.dtype),
        input_output_aliases={4: 0},
        scratch_shapes=([pltpu.VMEM((tq, past + tok.l_lat), F32)] * 2 + [pltpu.VMEM((tq, LANE), F32)] * 2
                        + [pltpu.VMEM((hg, past + tok.l_lat, 2 * V_HEAD_C), BF16)]),
        compiler_params=_cparams(("arbitrary", "arbitrary")),
        name="attn_lat",
    )(q, k_cache, v_cache, k, v)


def _swap_halves(w, head_dim):
    lead = w.shape[:-1]
    halves = w.reshape(lead + (w.shape[-1] // head_dim, 2, head_dim // 2))
    return halves[..., ::-1, :].reshape(w.shape)


def _pad_cols(w, width):
    return jnp.pad(w, ((0, 0),) * (w.ndim - 1) + ((0, width - w.shape[-1]),))


def _even_weights(w_in, w_gk2, b_gk2):
    n = w_in.shape[0]
    sizes = (A_QK, A_QK, A_V, A_V, 2 * GATE_RANK, A_QK, A_QK, A_V, A_V)
    qa, ka, va, ga, gk, qb, kb, vb, gb = jnp.split(w_in, np.cumsum(sizes)[:-1].tolist(), axis=-1)
    w = jnp.concatenate([qa, ka, va, ga, qb, kb, vb, gb, _pad_cols(gk, LANE),
                         _swap_halves(qb, DK), _swap_halves(kb, DK)], axis=-1).astype(BF16)
    wg = jnp.zeros((n, LANE, 2 * A_QK), F32)
    wg = wg.at[:, 0:GATE_RANK, 0:A_QK].set(w_gk2[:, 0]).at[:, GATE_RANK:2 * GATE_RANK, A_QK:2 * A_QK].set(w_gk2[:, 1])
    bg = b_gk2.reshape(n, 1, 2 * A_QK)
    return w, wg.astype(BF16), bg


def _odd_weights(w_in, w_q_b, w_kv_b):
    n = w_in.shape[0]
    q_lat, ckv, kpe = w_in[..., :Q_LORA], w_in[..., Q_LORA:Q_LORA + KV_LORA], w_in[..., Q_LORA + KV_LORA:]
    win = jnp.concatenate([q_lat, ckv, _pad_cols(kpe, LANE), _pad_cols(_swap_halves(kpe, QK_ROPE), LANE)],
                          axis=-1).astype(BF16)
    wq = w_q_b.reshape(n, Q_LORA, H_C, QK_NOPE + QK_ROPE)
    nope = wq[..., :QK_NOPE].reshape(n, Q_LORA, H_C * QK_NOPE)
    rope = wq[..., QK_NOPE:]
    pad = lambda r: _pad_cols(r, LANE).reshape(n, Q_LORA, H_C * LANE)
    wqb = jnp.concatenate([nope, pad(rope), pad(_swap_halves(rope, QK_ROPE))], axis=-1).astype(BF16)
    wkv = w_kv_b.reshape(n, KV_LORA, H_C, QK_NOPE + V_HEAD_C)
    wkvb = jnp.concatenate([wkv[..., :QK_NOPE].reshape(n, KV_LORA, H_C * QK_NOPE),
                            wkv[..., QK_NOPE:].reshape(n, KV_LORA, H_C * V_HEAD_C)], axis=-1).astype(BF16)
    return win, wqb, wkvb


def _rope_tables(tok, tm):
    rows = tok.l_lat // GRID_W
    row = jnp.repeat(jnp.arange(rows), GRID_W).astype(F32)
    col = jnp.tile(jnp.arange(GRID_W), rows).astype(F32)
    n_freq = QK_ROPE // 4
    inv = ROPE_BASE ** (-jnp.arange(n_freq, dtype=F32) / n_freq)
    ang = jnp.concatenate([row[:, None] * inv, col[:, None] * inv], axis=-1)
    cos, sin = jnp.cos(ang), jnp.sin(ang)
    cc = jnp.tile(jnp.concatenate([cos, cos], axis=-1), (1, LANE // QK_ROPE))
    ss = jnp.tile(jnp.concatenate([-sin, sin], axis=-1), (1, LANE // QK_ROPE))
    cc = jnp.concatenate([jnp.ones((tm, LANE), F32), cc], axis=0)
    ss = jnp.concatenate([jnp.zeros((tm, LANE), F32), ss], axis=0)
    return cc, ss


def kernel(x_prompt, x_sample, cache_ckv, cache_kpe, state_gla, state_ret, c, c_ctx, w_ada, b_ada, norm_mix_pre, norm_mix_post, norm_mlp_pre, norm_mlp_post, w_in_even, w_gk2, b_gk2, gla_norm, ret_decay, w_out_even, w_in_odd, q_a_norm, w_q_b, kv_a_norm, w_kv_b, w_out_odd, w_mlp1, w_mlp2):
    nb_ctx, l_ctx, d = x_prompt.shape
    nb_lat, l_lat, _ = x_sample.shape
    depth = w_ada.shape[0]
    tok = _Tokens(nb_ctx, l_ctx, nb_lat, l_lat, d)
    assert nb_lat < MOD_ROWS and tok.n_ctx % l_lat == 0 and l_ctx % CHUNK == 0 and l_lat % (2 * CHUNK) == 0

    cond = jnp.concatenate([c, c_ctx[None, :], jnp.zeros((MOD_ROWS - nb_lat - 1, d), F32)], axis=0)
    mods = _ada_call(cond, w_ada, b_ada).reshape(depth, MOD_ROWS, 6, 1, d)
    rope_even = _rope_tables(tok, tok.tile(EVEN_TILE))
    rope_odd = _rope_tables(tok, tok.tile(MLA_TILE))
    x = (x_prompt.reshape(tok.n_ctx, d), x_sample.reshape(tok.n_lat, d))
    vec = lambda a: a.reshape(1, -1)
    mlp_w = (w_mlp1, w_mlp2)
    even_w = _even_weights(w_in_even, w_gk2, b_gk2) + (jnp.repeat(ret_decay, DK, axis=-1),)
    odd_w = _odd_weights(w_in_odd, w_q_b, w_kv_b)

    new_ckv, new_kpe, new_gla, new_ret = [], [], [], []
    for l in range(depth):
        i = l // 2
        if l % 2 == 0:
            cast_items = [(w, ll) for ll in range(l, min(l + 2, depth)) for w in mlp_w]
            cast_items += [(w_out_even, i)] + ([(w_out_odd, i)] if l + 1 < depth else [])
            q, k, lf, lb, gt, v, *w16 = _even_proj_call(tok, x, mods, l, i, vec(norm_mix_pre[l]), *even_w,
                                                          *rope_even, cast_items)
            gain = jnp.concatenate([jnp.tile(gla_norm[i], N_HEAD_SCAN), jnp.ones((A_V,), F32)]).reshape(1, 2 * A_V)
            s0_lat = jnp.concatenate([state_gla[:, i], state_ret[:, i]], axis=2)
            m_ctx, s_fin = _scan_call(tok, q, k, lf, lb, v, gt, gain)
            (m,) = _scan_call(tok, q, k, lf, lb, m_ctx, gt, gain, s0=s0_lat)
            new_gla.append(s_fin[:, :, :N_HEAD_SCAN])
            new_ret.append(s_fin[:, :, N_HEAD_SCAN:])
            w_out = w16[-2] if l + 1 < depth else w16[-1]
        else:
            win, wqb, wkvb = odd_w
            q, k, v, ckv, kpe = _mla_proj_call(tok, x, mods, l, i, vec(norm_mix_pre[l]), win, vec(q_a_norm[i]), wqb,
                                               vec(kv_a_norm[i]), wkvb, *rope_odd)
            past = cache_ckv.shape[2]
            kpe_pad = jnp.pad(cache_kpe[:, i].reshape(nb_lat * past, QK_ROPE), ((0, 0), (0, LANE - QK_ROPE)))
            k_c, v_c = _cache_expand_call(cache_ckv[:, i].reshape(nb_lat * past, KV_LORA), kpe_pad, wkvb, i)
            m = _attn_lat_call(tok, q, k, _attn_ctx_call(tok, q, k, v), k_c, v_c)
            new_ckv.append(ckv[:tok.n_ctx].reshape(nb_ctx, l_ctx, KV_LORA))
            new_kpe.append(kpe[:tok.n_ctx, :QK_ROPE].reshape(nb_ctx, l_ctx, QK_ROPE))
            w_out = w16[-1]
        w1, w2 = w16[2 * (l % 2):2 * (l % 2) + 2]
        x = _mix_mlp_call(tok, m, x, mods, l, w_out, vec(norm_mix_post[l]), vec(norm_mlp_pre[l]), w1, w2,
                          vec(norm_mlp_post[l]), split_out=(l == depth - 1))
        x = x[0] if len(x) == 1 else tuple(x)

    return (x[0].reshape(nb_ctx, l_ctx, d), x[1].reshape(nb_lat, l_lat, d),
            jnp.stack(new_ckv, axis=1), jnp.stack(new_kpe, axis=1),
            jnp.stack(new_gla, axis=1), jnp.stack(new_ret, axis=1))
```

```python
import functools

import numpy as np
import jax
import jax.numpy as jnp
from jax import lax
from jax.experimental import pallas as pl
from jax.experimental.pallas import tpu as pltpu

F32 = jnp.float32
BF16 = jnp.bfloat16

EPS = 1e-6
ROPE_BASE = 10000.0
GRID_W = 64
CHUNK = 64
GATE_RANK = 16
GATE_NORM = 16.0
N_HEAD_SCAN = 4
DK = 64
DV = 128
H_C = 8
Q_LORA = 256
KV_LORA = 256
QK_NOPE = 128
QK_ROPE = 64
V_HEAD_C = 128
LANE = 128
MOD_ROWS = 16
EVEN_TILE = 512
MLA_TILE = 1024
MIX_TILE = 512
CACHE_TILE = 512
ADA_TILE = 1536

V7X_VMEM_BYTES = 64 * 1024 * 1024
VMEM_LIMIT = V7X_VMEM_BYTES - 8 * 1024 * 1024


def _cparams(sem):
    return pltpu.CompilerParams(dimension_semantics=sem, vmem_limit_bytes=VMEM_LIMIT)


def _dot(a, b):
    return jnp.dot(a, b, preferred_element_type=F32)


def _dot_nt(a, b):
    return lax.dot_general(a, b, (((1,), (1,)), ((), ())), preferred_element_type=F32)


def _dot_tn(a, b):
    return lax.dot_general(a, b, (((0,), (0,)), ((), ())), preferred_element_type=F32)


def _rms(x):
    return x * lax.rsqrt(jnp.mean(x * x, axis=-1, keepdims=True) + EPS)


def _silu(x):
    return x * jax.nn.sigmoid(x)


def _full(shape):
    n = len(shape)
    return pl.BlockSpec(shape, lambda *_: (0,) * n, pipeline_mode=pl.Buffered(1))


def _layer_block(w, idx):
    tail = (0,) * (w.ndim - 1)
    return pl.BlockSpec((None,) + w.shape[1:], lambda *_: (idx,) + tail, pipeline_mode=pl.Buffered(1))


def _side_cast_specs(items, steps):
    nb = 1 << (steps.bit_length() - 1)
    block = lambda i: jnp.minimum(i, nb - 1)
    ins = [pl.BlockSpec((None, w.shape[1] // nb, w.shape[2]), lambda i, layer=layer: (layer, block(i), 0))
           for w, layer in items]
    outs = [pl.BlockSpec((w.shape[1] // nb, w.shape[2]), lambda i: (block(i), 0)) for w, _ in items]
    shapes = [jax.ShapeDtypeStruct(w.shape[1:], BF16) for w, _ in items]
    return ins, outs, shapes


def _side_cast(in_refs, out_refs):
    for src, dst in zip(in_refs, out_refs):
        dst[...] = src[...].astype(BF16)


def _ada_body(cond_ref, w_ref, b_ref, o_ref):
    s = _silu(cond_ref[...]).astype(BF16)
    o_ref[...] = _dot(s, w_ref[...].astype(BF16)) + b_ref[...]


def _ada_call(cond, w_ada, b_ada):
    depth, d, n = w_ada.shape
    tn = ADA_TILE
    return pl.pallas_call(
        _ada_body,
        grid=(depth, n // tn),
        in_specs=[
            pl.BlockSpec((MOD_ROWS, d), lambda l, j: (0, 0)),
            pl.BlockSpec((None, d, tn), lambda l, j: (l, 0, j)),
            pl.BlockSpec((None, 1, tn), lambda l, j: (l, 0, j)),
        ],
        out_specs=pl.BlockSpec((None, MOD_ROWS, tn), lambda l, j: (l, 0, j)),
        out_shape=jax.ShapeDtypeStruct((depth, MOD_ROWS, n), F32),
        compiler_params=_cparams(("arbitrary", "arbitrary")),
        name="ada_mod",
    )(cond, w_ada, b_ada.reshape(depth, 1, n))


class _Tokens:
    def __init__(self, nb_ctx, l_ctx, nb_lat, l_lat, d):
        self.nb_ctx, self.l_ctx, self.nb_lat, self.l_lat, self.d = nb_ctx, l_ctx, nb_lat, l_lat, d
        self.n_ctx = nb_ctx * l_ctx
        self.n_lat = nb_lat * l_lat
        self.n = self.n_ctx + self.n_lat
        self.ctx_row = nb_lat

    def tile(self, want):
        t = want
        while self.n_ctx % t or self.l_lat % t:
            t //= 2
        return t

    def mod_spec(self, layer, chunk, tm):
        n_ctx, l_lat, ctx_row = self.n_ctx, self.l_lat, self.ctx_row

        def idx(i, *_):
            start = i * tm
            row = jnp.where(start < n_ctx, ctx_row, (start - n_ctx) // l_lat)
            return (layer, row, chunk, 0, 0)

        return pl.BlockSpec((None, None, None, 1, self.d), idx)

    def x_specs(self, x, tm):
        if not isinstance(x, tuple):
            return [pl.BlockSpec((tm, self.d), lambda i: (i, 0))], [x]
        nct = self.n_ctx // tm
        return [pl.BlockSpec((tm, self.d), lambda i: (jnp.minimum(i, nct - 1), 0)),
                pl.BlockSpec((tm, self.d), lambda i: (jnp.maximum(i - nct, 0), 0))], list(x)

    def rope_spec(self, tm):
        n_ctx, l_lat = self.n_ctx, self.l_lat

        def idx(i):
            start = i * tm
            return (jnp.where(start < n_ctx, 0, 1 + ((start - n_ctx) % l_lat) // tm), 0)

        return pl.BlockSpec((tm, LANE), idx)


def _read_x(x_refs, n_ctx_tiles):
    if len(x_refs) == 1:
        return x_refs[0][...]
    return jnp.where(pl.program_id(0) < n_ctx_tiles, x_refs[0][...], x_refs[1][...])


A_QK = N_HEAD_SCAN * DK
A_V = N_HEAD_SCAN * DV
E_QA, E_KA, E_VA, E_GA = 0, A_QK, 2 * A_QK, 2 * A_QK + A_V
E_QB = E_GA + A_V
E_KB = E_QB + A_QK
E_VB = E_KB + A_QK
E_GB = E_VB + A_V
E_GK = E_GB + A_V
E_QBS = E_GK + LANE
E_KBS = E_QBS + A_QK


def _log_sigmoid(x):
    return jnp.minimum(x, 0.0) - jnp.log(1.0 + jnp.exp(-jnp.abs(x)))


def _even_proj_body(*refs, n_x, n_ctx_tiles, n_cast):
    x = _read_x(refs[:n_x], n_ctx_tiles)
    g_ref, sh_ref, sc_ref, w_ref, wg_ref, bg_ref, dec_ref, cc_ref, ss_ref = refs[n_x:n_x + 9]
    q_ref, k_ref, lf_ref, lb_ref, gt_ref, v_ref = refs[n_x + 9 + n_cast:n_x + 15 + n_cast]
    _side_cast(refs[n_x + 9:n_x + 9 + n_cast], refs[n_x + 15 + n_cast:])
    tm = x.shape[0]
    h = (_rms(x) * g_ref[...] * (1.0 + sc_ref[...]) + sh_ref[...]).astype(BF16)

    def proj(start, width):
        return _dot(h, w_ref[:, start:start + width])

    gk = proj(E_GK, LANE).astype(BF16)
    la = _log_sigmoid(_dot(gk, wg_ref[...]) + bg_ref[...]) * (1.0 / GATE_NORM)
    lf_ref[:, 0:A_QK] = la[:, 0:A_QK]
    lb_ref[:, 0:A_QK] = la[:, A_QK:2 * A_QK]
    log_g = -jnp.exp(dec_ref[...])
    lf_ref[:, A_QK:2 * A_QK] = jnp.broadcast_to(log_g[0:1, :], (tm, A_QK))
    lb_ref[:, A_QK:2 * A_QK] = jnp.broadcast_to(log_g[1:2, :], (tm, A_QK))
    cc = cc_ref[...]
    ss = ss_ref[...]
    scale = DK ** -0.5
    qb, qbs, kb, kbs = proj(E_QB, A_QK), proj(E_QBS, A_QK), proj(E_KB, A_QK), proj(E_KBS, A_QK)
    for j in range(A_QK // LANE):
        sl = slice(j * LANE, (j + 1) * LANE)
        o = A_QK + j * LANE
        q_ref[:, o:o + LANE] = qb[:, sl] * cc + qbs[:, sl] * ss
        k_ref[:, o:o + LANE] = (kb[:, sl] * cc + kbs[:, sl] * ss) * scale
    q_ref[:, 0:A_QK] = proj(E_QA, A_QK) * scale
    k_ref[:, 0:A_QK] = proj(E_KA, A_QK)
    v_ref[:, 0:A_V] = proj(E_VA, A_V).astype(BF16)
    v_ref[:, A_V:2 * A_V] = proj(E_VB, A_V).astype(BF16)
    gt_ref[:, 0:A_V] = proj(E_GA, A_V)
    gt_ref[:, A_V:2 * A_V] = proj(E_GB, A_V)


def _even_proj_call(tok, x, mods, layer, idx, g_pre, w, wg, bg, dec, cc, ss, cast_items):
    tm = tok.tile(EVEN_TILE)
    d = tok.d
    row = lambda i: (i, 0)
    outs = [
        jax.ShapeDtypeStruct((tok.n, 2 * A_QK), F32),
        jax.ShapeDtypeStruct((tok.n, 2 * A_QK), F32),
        jax.ShapeDtypeStruct((tok.n, 2 * A_QK), F32),
        jax.ShapeDtypeStruct((tok.n, 2 * A_QK), F32),
        jax.ShapeDtypeStruct((tok.n, 2 * A_V), F32),
        jax.ShapeDtypeStruct((tok.n, 2 * A_V), BF16),
    ]
    x_specs, x_args = tok.x_specs(x, tm)
    cast_in, cast_out, cast_shape = _side_cast_specs(cast_items, tok.n // tm)
    return pl.pallas_call(
        functools.partial(_even_proj_body, n_x=len(x_args), n_ctx_tiles=tok.n_ctx // tm, n_cast=len(cast_items)),
        grid=(tok.n // tm,),
        in_specs=x_specs + [
            _full((1, d)),
            tok.mod_spec(layer, 0, tm),
            tok.mod_spec(layer, 1, tm),
            _layer_block(w, idx), _layer_block(wg, idx), _layer_block(bg, idx), _layer_block(dec, idx),
            tok.rope_spec(tm),
            tok.rope_spec(tm),
        ] + cast_in,
        out_specs=[pl.BlockSpec((tm, o.shape[1]), row) for o in outs] + cast_out,
        out_shape=outs + cast_shape,
        compiler_params=_cparams(("arbitrary",)),
        name="even_proj",
    )(*x_args, g_pre, mods, mods, w, wg, bg, dec, cc, ss, *[stack for stack, _ in cast_items])


SCAN_GROUP = 16


def _chunk_cumsum(x):
    row = lax.broadcasted_iota(jnp.int32, x.shape, 0) % CHUNK
    s = 1
    while s < CHUNK:
        x = x + jnp.where(row >= s, pltpu.roll(x, s, axis=0), 0.0)
        s *= 2
    return x


def _scan_body(q_ref, k_ref, lf_ref, lb_ref, v_ref, gt_ref, gain_ref, *rest, seq_len, group, context):
    if context:
        m_ref, sfin_ref, st_ref, o_acc = rest
    else:
        s0_ref, m_ref, st_ref, o_acc = rest
    C = CHUNK
    blk = group * C
    nblk = seq_len // blk
    pair_w = 2 * DK
    head0 = lax.broadcasted_iota(jnp.int32, (blk, pair_w), 1) < DK
    t_in = lax.broadcasted_iota(jnp.int32, (C, pair_w), 0)
    j_in = lax.broadcasted_iota(jnp.int32, (C, pair_w), 1) % DK
    keep_fwd = t_in >= j_in
    keep_bwd = t_in <= j_in
    on_diag = ((lax.broadcasted_iota(jnp.int32, (2 * DV, pair_w), 0) < DV)
               == (lax.broadcasted_iota(jnp.int32, (2 * DV, pair_w), 1) < DK))
    zeros_v = jnp.zeros((C, DV), BF16)

    if context:
        st_ref[...] = jnp.zeros(st_ref.shape, F32)
    else:
        zeros_s = jnp.zeros((DK, DV), F32)
        for d in range(2):
            both = jnp.concatenate([jnp.concatenate([s0_ref[d, 0], zeros_s], axis=1),
                                    jnp.concatenate([zeros_s, s0_ref[d, 1]], axis=1)], axis=0)
            st_ref[d] = both.T

    def one_direction(r0, d, log_ref, keep, reverse):
        g = log_ref[pl.ds(r0, blk), :]
        b = _chunk_cumsum(g)
        tots = [b[C * j + C - 1:C * j + C, :] for j in range(group)]
        totb = jnp.concatenate([jnp.broadcast_to(t, (C, pair_w)) for t in tots], axis=0)
        if reverse:
            b = totb - b + g
        q = q_ref[pl.ds(r0, blk), :]
        k = k_ref[pl.ds(r0, blk), :]
        vblk = v_ref[pl.ds(r0, blk), :]
        q_dec = (q * jnp.exp(b)).astype(BF16)
        k_inv = k * jnp.exp(-b)
        k_up = (k * jnp.exp(totb - b)).astype(BF16)
        k_inv0 = jnp.where(head0, k_inv, 0.0).astype(BF16)
        k_inv1 = jnp.where(head0, 0.0, k_inv).astype(BF16)
        st = st_ref[d]
        outs = [None] * group
        for j in (reversed(range(group)) if reverse else range(group)):
            sl = slice(C * j, C * (j + 1))
            k_bd = jnp.concatenate([k_inv0[sl], k_inv1[sl]], axis=0)
            a = jnp.where(keep, _dot_nt(q_dec[sl], k_bd), 0.0).astype(BF16)
            vc = vblk[sl]
            v_bd = jnp.concatenate([jnp.concatenate([vc[:, :DV], zeros_v], axis=1),
                                    jnp.concatenate([zeros_v, vc[:, DV:]], axis=1)], axis=0)
            outs[j] = _dot_nt(q_dec[sl], st.astype(BF16)) + _dot(a, v_bd)
            st = st * jnp.exp(tots[j]) + jnp.where(on_diag, _dot_tn(vc, k_up[sl]), 0.0)
        st_ref[d] = st
        return jnp.concatenate(outs, axis=0)

    def emit(r0, o):
        gate = _silu(gt_ref[pl.ds(r0, blk), :])
        gain = gain_ref[...]
        for hh in range(2):
            sl = slice(hh * DV, (hh + 1) * DV)
            m_ref[pl.ds(r0, blk), sl] = (_rms(o[:, sl]) * gain[:, sl] * gate[:, sl]).astype(BF16)

    def body(c, carry, second_half):
        rf = pl.multiple_of(c * blk, blk)
        rb = pl.multiple_of((nblk - 1 - c) * blk, blk)
        o_f = one_direction(rf, 0, lf_ref, keep_fwd, False)
        o_b = one_direction(rb, 1, lb_ref, keep_bwd, True)
        if second_half:
            emit(rf, o_acc[pl.ds(rf, blk), :] + o_f)
            emit(rb, o_acc[pl.ds(rb, blk), :] + o_b)
        else:
            o_acc[pl.ds(rf, blk), :] = o_f
            o_acc[pl.ds(rb, blk), :] = o_b
        return carry

    if nblk == 1:
        emit(0, one_direction(0, 0, lf_ref, keep_fwd, False) + one_direction(0, 1, lb_ref, keep_bwd, True))
    else:
        lax.fori_loop(0, nblk // 2, functools.partial(body, second_half=False), 0)
        lax.fori_loop(nblk // 2, nblk, functools.partial(body, second_half=True), 0)

    if context:
        for d in range(2):
            both = st_ref[d].T
            sfin_ref[d, 0] = both[0:DK, 0:DV]
            sfin_ref[d, 1] = both[DK:2 * DK, DV:2 * DV]


def _scan_call(tok, q, k, lf, lb, v, gt, gain, s0=None):
    context = s0 is None
    if context:
        nb, seq_len, blk0 = tok.nb_ctx, tok.l_ctx, 0
    else:
        nb, seq_len, blk0 = tok.nb_lat, tok.l_lat, tok.n_ctx // tok.l_lat
    pairs = q.shape[1] // (2 * DK)
    group = min(SCAN_GROUP, seq_len // CHUNK)
    seq = lambda b, p: (blk0 + b, p)
    st_spec = pl.BlockSpec((None, 2, 2, DK, DV), lambda b, p: (b, 0, p, 0, 0))
    in_specs = [
        pl.BlockSpec((seq_len, 2 * DK), seq),
        pl.BlockSpec((seq_len, 2 * DK), seq),
        pl.BlockSpec((seq_len, 2 * DK), seq),
        pl.BlockSpec((seq_len, 2 * DK), seq),
        pl.BlockSpec((seq_len, 2 * DV), seq),
        pl.BlockSpec((seq_len, 2 * DV), seq),
        pl.BlockSpec((1, 2 * DV), lambda b, p: (0, p)),
    ]
    args = [q, k, lf, lb, v, gt, gain]
    out_shape = [jax.ShapeDtypeStruct(v.shape, v.dtype)]
    out_specs = [pl.BlockSpec((seq_len, 2 * DV), seq)]
    if context:
        out_shape.append(jax.ShapeDtypeStruct((nb, 2, 2 * pairs, DK, DV), F32))
        out_specs.append(st_spec)
    else:
        in_specs.append(st_spec)
        args.append(s0)
    return pl.pallas_call(
        functools.partial(_scan_body, seq_len=seq_len, group=group, context=context),
        grid=(nb, pairs),
        in_specs=in_specs,
        out_specs=out_specs,
        out_shape=out_shape,
        input_output_aliases={4: 0},
        scratch_shapes=[pltpu.VMEM((2, 2 * DV, 2 * DK), F32), pltpu.VMEM((seq_len, 2 * DV), F32)],
        compiler_params=_cparams(("arbitrary", "arbitrary")),
        name="scan_ctx" if context else "scan_lat",
    )(*args)


MIX_ROW_GROUPS = 4


def _mix_mlp_body(*refs, n_x, n_out, n_ctx_tiles):
    x = _read_x(refs[:n_x], n_ctx_tiles)
    (m_ref, wo_ref, gmix_ref, gate1_ref, gpre_ref, sh_ref, sc_ref, w1_ref, w2_ref,
     gpost_ref, gate2_ref) = refs[n_x:len(refs) - n_out]
    out_refs = refs[len(refs) - n_out:]
    rows = x.shape[0] // MIX_ROW_GROUPS
    x1_parts, h_parts = [], []
    for r0 in range(0, x.shape[0], rows):
        y = _dot(m_ref[r0:r0 + rows, :], wo_ref[...])
        x1_g = x[r0:r0 + rows] + gate1_ref[...] * (_rms(y) * gmix_ref[...])
        x1_parts.append(x1_g)
        h_parts.append((_rms(x1_g) * gpre_ref[...] * (1.0 + sc_ref[...]) + sh_ref[...]).astype(BF16))
    x1 = jnp.concatenate(x1_parts, axis=0)
    h = jnp.concatenate(h_parts, axis=0)
    u = jnp.maximum(_dot(h, w1_ref[...]), 0.0)
    u = (u * u).astype(BF16)
    half = x.shape[0] // 2
    res = jnp.concatenate(
        [x1[r0:r0 + half] + gate2_ref[...] * (_rms(_dot(u[r0:r0 + half], w2_ref[...])) * gpost_ref[...])
         for r0 in (0, half)], axis=0)
    if n_out == 1:
        out_refs[0][...] = res
    else:
        is_ctx = pl.program_id(0) < n_ctx_tiles

        @pl.when(is_ctx)
        def _():
            out_refs[0][...] = res

        @pl.when(jnp.logical_not(is_ctx))
        def _():
            out_refs[1][...] = res


def _mix_mlp_call(tok, m, x, mods, layer, w_out, g_mix, g_pre, w1, w2, g_post, split_out=False):
    tm = tok.tile(MIX_TILE)
    d = tok.d
    row = lambda i: (i, 0)
    x_specs, x_args = tok.x_specs(x, tm)
    nct = tok.n_ctx // tm
    if split_out:
        out_specs = [pl.BlockSpec((tm, d), lambda i: (jnp.minimum(i, nct - 1), 0)),
                     pl.BlockSpec((tm, d), lambda i: (jnp.maximum(i - nct, 0), 0))]
        out_shape = [jax.ShapeDtypeStruct((tok.n_ctx, d), F32), jax.ShapeDtypeStruct((tok.n_lat, d), F32)]
    else:
        out_specs = [pl.BlockSpec((tm, d), row)]
        out_shape = [jax.ShapeDtypeStruct((tok.n, d), F32)]
    return pl.pallas_call(
        functools.partial(_mix_mlp_body, n_x=len(x_args), n_out=len(out_shape), n_ctx_tiles=nct),
        grid=(tok.n // tm,),
        in_specs=x_specs + [
            pl.BlockSpec((tm, m.shape[1]), row),
            _full(w_out.shape),
            _full((1, d)),
            tok.mod_spec(layer, 2, tm),
            _full((1, d)),
            tok.mod_spec(layer, 3, tm),
            tok.mod_spec(layer, 4, tm),
            _full(w1.shape),
            _full(w2.shape),
            _full((1, d)),
            tok.mod_spec(layer, 5, tm),
        ],
        out_specs=out_specs,
        out_shape=out_shape,
        compiler_params=_cparams(("arbitrary",)),
        name="mix_mlp",
    )(*x_args, m, w_out, g_mix, mods, g_pre, mods, mods, w1, w2, g_post, mods)


HEAD_W = 2 * LANE
O_QLAT, O_CKV, O_KPE = 0, Q_LORA, Q_LORA + KV_LORA
QB_NOPE, QB_ROPE, QB_SWAP = 0, H_C * LANE, 2 * H_C * LANE


def _expand_kv(cb, kper, wkvb_ref, k_ref, v_ref):
    for hp in range(H_C // 2):
        nope2 = _dot(cb, wkvb_ref[:, hp * 2 * LANE:(hp + 1) * 2 * LANE])
        for j in range(2):
            hh = 2 * hp + j
            k_ref[:, hh * HEAD_W:hh * HEAD_W + LANE] = nope2[:, j * LANE:(j + 1) * LANE].astype(BF16)
            k_ref[:, hh * HEAD_W + LANE:(hh + 1) * HEAD_W] = kper
    v_ref[...] = _dot(cb, wkvb_ref[:, H_C * LANE:2 * H_C * LANE]).astype(BF16)


def _mla_proj_body(x_ref, g_ref, sh_ref, sc_ref, win_ref, qn_ref, wqb_ref, kvn_ref, wkvb_ref, cc_ref, ss_ref,
                   q_ref, k_ref, v_ref, ckv_ref, kpe_ref):
    h =(_rms(x_ref[...]) * g_ref[...] * (1.0 + sc_ref[...]) + sh_ref[...]).astype(BF16)
    cc = cc_ref[...]
    ss = ss_ref[...]
    qn = (_rms(_dot(h, win_ref[:, O_QLAT:O_QLAT + Q_LORA])) * qn_ref[...]).astype(BF16)
    ckvn = _rms(_dot(h, win_ref[:, O_CKV:O_CKV + KV_LORA])) * kvn_ref[...]
    kpe2 = _dot(h, win_ref[:, O_KPE:O_KPE + 2 * LANE])
    kpe = kpe2[:, 0:LANE]
    kper = (kpe * cc + kpe2[:, LANE:2 * LANE] * ss).astype(BF16)
    ckv_ref[...] = ckvn
    kpe_ref[...] = kpe
    for hp in range(H_C // 2):
        o = hp * 2 * LANE
        nope2 = _dot(qn, wqb_ref[:, QB_NOPE + o:QB_NOPE + o + 2 * LANE])
        rope2 = _dot(qn, wqb_ref[:, QB_ROPE + o:QB_ROPE + o + 2 * LANE])
        swap2 = _dot(qn, wqb_ref[:, QB_SWAP + o:QB_SWAP + o + 2 * LANE])
        for j in range(2):
            hh = 2 * hp + j
            sl = slice(j * LANE, (j + 1) * LANE)
            rot = rope2[:, sl] * cc + swap2[:, sl] * ss
            q_ref[:, hh * HEAD_W:hh * HEAD_W + LANE] = (nope2[:, sl] * ATTN_Q_SCALE).astype(BF16)
            q_ref[:, hh * HEAD_W + LANE:(hh + 1) * HEAD_W] = (rot * ATTN_Q_SCALE).astype(BF16)
    _expand_kv(ckvn.astype(BF16), kper, wkvb_ref, k_ref, v_ref)


def _mla_proj_call(tok, x, mods, layer, idx, g_pre, win, qn, wqb, kvn, wkvb, cc, ss):
    tm = tok.tile(MLA_TILE)
    d = tok.d
    row = lambda i: (i, 0)
    outs = [
        jax.ShapeDtypeStruct((tok.n, H_C * HEAD_W), BF16),
        jax.ShapeDtypeStruct((tok.n, H_C * HEAD_W), BF16),
        jax.ShapeDtypeStruct((tok.n, H_C * V_HEAD_C), BF16),
        jax.ShapeDtypeStruct((tok.n, KV_LORA), F32),
        jax.ShapeDtypeStruct((tok.n, LANE), F32),
    ]
    return pl.pallas_call(
        _mla_proj_body,
        grid=(tok.n // tm,),
        in_specs=[
            pl.BlockSpec((tm, d), row),
            _full((1, d)),
            tok.mod_spec(layer, 0, tm),
            tok.mod_spec(layer, 1, tm),
            _layer_block(win, idx), _full(qn.shape), _layer_block(wqb, idx), _full(kvn.shape),
            _layer_block(wkvb, idx),
            tok.rope_spec(tm),
            tok.rope_spec(tm),
        ],
        out_specs=[pl.BlockSpec((tm, o.shape[1]), row) for o in outs],
        out_shape=outs,
        compiler_params=_cparams(("arbitrary",)),
        name="mla_proj",
    )(x, g_pre, mods, mods, win, qn, wqb, kvn, wkvb, cc, ss)


def _cache_expand_body(ckv_ref, kpe_ref, wkvb_ref, k_ref, v_ref):
    _expand_kv(ckv_ref[...].astype(BF16), kpe_ref[...].astype(BF16), wkvb_ref, k_ref, v_ref)


def _cache_expand_call(ckv, kpe_pad, wkvb, idx):
    n = ckv.shape[0]
    tm = CACHE_TILE
    while n % tm:
        tm //= 2
    row = lambda i: (i, 0)
    outs = [jax.ShapeDtypeStruct((n, H_C * HEAD_W), BF16), jax.ShapeDtypeStruct((n, H_C * V_HEAD_C), BF16)]
    return pl.pallas_call(
        _cache_expand_body,
        grid=(n // tm,),
        in_specs=[pl.BlockSpec((tm, KV_LORA), row), pl.BlockSpec((tm, LANE), row), _layer_block(wkvb, idx)],
        out_specs=[pl.BlockSpec((tm, o.shape[1]), row) for o in outs],
        out_shape=outs,
        compiler_params=_cparams(("arbitrary",)),
        name="cache_expand",
    )(ckv, kpe_pad, wkvb)


ATTN_TQ = 512
ATTN_TK = 512
ATTN_HEADS_PER_STEP = 2
ATTN_Q_SCALE = (QK_NOPE + QK_ROPE) ** -0.5 * float(np.log2(np.e))


def _attn_ctx_body(q_ref, k_ref, v_ref, o_ref):
    for hh in range(H_C):
        s = _dot_nt(q_ref[:, hh * HEAD_W:(hh + 1) * HEAD_W], k_ref[:, hh * HEAD_W:(hh + 1) * HEAD_W])
        p = jnp.exp2(s - jnp.max(s, axis=-1, keepdims=True))
        den = jnp.sum(p, axis=-1, keepdims=True)
        acc = _dot(p.astype(BF16), v_ref[:, hh * V_HEAD_C:(hh + 1) * V_HEAD_C])
        o_ref[:, hh * V_HEAD_C:(hh + 1) * V_HEAD_C] = (acc / den).astype(BF16)


def _lane_groups(x):
    return [x[:, g:g + LANE] for g in range(0, x.shape[1], LANE)]


def _attn_lat_body(q_ref, kc_ref, vc_ref, k_ref, v_ref, o_ref, sa_ref, sb_ref, ma_ref, mb_ref, vx_ref):
    past, n_self = kc_ref.shape[0], k_ref.shape[0]
    heads = vx_ref.shape[0]
    tq = sa_ref.shape[0]
    n = q_ref.shape[0] // tq
    tk = min(ATTN_TK, n_self)
    blocks = [(kc_ref, 0, past, 0)] + [(k_ref, k0, tk, past + k0) for k0 in range(0, n_self, tk)]

    for h in range(heads):
        vx_ref[h, 0:past, 0:V_HEAD_C] = vc_ref[:, h * V_HEAD_C:(h + 1) * V_HEAD_C]
        vx_ref[h, past:past + n_self, 0:V_HEAD_C] = v_ref[:, h * V_HEAD_C:(h + 1) * V_HEAD_C]
        vx_ref[h, :, V_HEAD_C:2 * V_HEAD_C] = jnp.ones((past + n_self, V_HEAD_C), BF16)

    def scores(h, i, s_ref, m_ref):
        q = q_ref[pl.ds(pl.multiple_of(i * tq, tq), tq), h * HEAD_W:(h + 1) * HEAD_W]
        mx = None
        for kk_ref, k0, size, col in blocks:
            s = _dot_nt(q, kk_ref[k0:k0 + size, h * HEAD_W:(h + 1) * HEAD_W])
            s_ref[:, col:col + size] = s
            mx = functools.reduce(jnp.maximum, _lane_groups(s) + ([] if mx is None else [mx]))
        m_ref[...] = mx

    def finish(h, i, s_ref, m_ref):
        m = jnp.max(m_ref[...], axis=-1, keepdims=True)
        acc = None
        for _, _, size, col in blocks:
            p = jnp.exp2(s_ref[:, col:col + size] - m).astype(BF16)
            pv = _dot(p, vx_ref[h, col:col + size, :])
            acc = pv if acc is None else acc + pv
        out = acc[:, 0:V_HEAD_C] / acc[:, V_HEAD_C:2 * V_HEAD_C]
        o_ref[pl.ds(pl.multiple_of(i * tq, tq), tq), h * V_HEAD_C:(h + 1) * V_HEAD_C] = out.astype(BF16)

    scores(0, 0, sa_ref, ma_ref)
    for h in range(heads):

        def two_tiles(j, carry, h=h):
            i = 2 * j
            scores(h, i + 1, sb_ref, mb_ref)
            finish(h, i, sa_ref, ma_ref)
            scores(h, i + 2, sa_ref, ma_ref)
            finish(h, i + 1, sb_ref, mb_ref)
            return carry

        lax.fori_loop(0, n // 2 - 1, two_tiles, 0)
        scores(h, n - 1, sb_ref, mb_ref)
        finish(h, n - 2, sa_ref, ma_ref)
        if h + 1 < heads:
            scores(h + 1, 0, sa_ref, ma_ref)
        finish(h, n - 1, sb_ref, mb_ref)


def _attn_ctx_call(tok, q, k, v):
    seq = lambda b: (b, 0)
    return pl.pallas_call(
        _attn_ctx_body,
        grid=(tok.nb_ctx,),
        in_specs=[pl.BlockSpec((tok.l_ctx, H_C * HEAD_W), seq),
                  pl.BlockSpec((tok.l_ctx, H_C * HEAD_W), seq),
                  pl.BlockSpec((tok.l_ctx, H_C * V_HEAD_C), seq)],
        out_specs=pl.BlockSpec((tok.l_ctx, H_C * V_HEAD_C), seq),
        out_shape=jax.ShapeDtypeStruct(v.shape, v.dtype),
        input_output_aliases={2: 0},
        compiler_params=_cparams(("arbitrary",)),
        name="attn_ctx",
    )(q, k, v)


def _attn_lat_call(tok, q, k, v, k_cache, v_cache):
    blk0 = tok.n_ctx // tok.l_lat
    past = k_cache.shape[0] // tok.nb_lat
    tq = min(ATTN_TQ, tok.l_lat // 2)
    hg = ATTN_HEADS_PER_STEP
    seq = lambda b, hh: (blk0 + b, hh)
    cache = lambda b, hh: (b, hh)
    return pl.pallas_call(
        _attn_lat_body,
        grid=(tok.nb_lat, H_C // hg),
        in_specs=[pl.BlockSpec((tok.l_lat, hg * HEAD_W), seq),
                  pl.BlockSpec((past, hg * HEAD_W), cache),
                  pl.BlockSpec((past, hg * V_HEAD_C), cache),
                  pl.BlockSpec((tok.l_lat, hg * HEAD_W), seq),
                  pl.BlockSpec((tok.l_lat, hg * V_HEAD_C), seq)],
        out_specs=pl.BlockSpec((tok.l_lat, hg * V_HEAD_C), seq),
        out_shape=jax.ShapeDtypeStruct(v.shape, v.dtype),
        input_output_aliases={4: 0},
        scratch_shapes=([pltpu.VMEM((tq, past + tok.l_lat), F32)] * 2 + [pltpu.VMEM((tq, LANE), F32)] * 2
                        + [pltpu.VMEM((hg, past + tok.l_lat, 2 * V_HEAD_C), BF16)]),
        compiler_params=_cparams(("arbitrary", "arbitrary")),
        name="attn_lat",
    )(q, k_cache, v_cache, k, v)


def _swap_halves(w, head_dim):
    lead = w.shape[:-1]
    halves = w.reshape(lead + (w.shape[-1] // head_dim, 2, head_dim // 2))
    return halves[..., ::-1, :].reshape(w.shape)


def _pad_cols(w, width):
    return jnp.pad(w, ((0, 0),) * (w.ndim - 1) + ((0, width - w.shape[-1]),))


def _even_weights(w_in, w_gk2, b_gk2):
    n = w_in.shape[0]
    sizes = (A_QK, A_QK, A_V, A_V, 2 * GATE_RANK, A_QK, A_QK, A_V, A_V)
    qa, ka, va, ga, gk, qb, kb, vb, gb = jnp.split(w_in, np.cumsum(sizes)[:-1].tolist(), axis=-1)
    w = jnp.concatenate([qa, ka, va, ga, qb, kb, vb, gb, _pad_cols(gk, LANE),
                         _swap_halves(qb, DK), _swap_halves(kb, DK)], axis=-1).astype(BF16)
    wg = jnp.zeros((n, LANE, 2 * A_QK), F32)
    wg = wg.at[:, 0:GATE_RANK, 0:A_QK].set(w_gk2[:, 0]).at[:, GATE_RANK:2 * GATE_RANK, A_QK:2 * A_QK].set(w_gk2[:, 1])
    bg = b_gk2.reshape(n, 1, 2 * A_QK)
    return w, wg.astype(BF16), bg


def _odd_weights(w_in, w_q_b, w_kv_b):
    n = w_in.shape[0]
    q_lat, ckv, kpe = w_in[..., :Q_LORA], w_in[..., Q_LORA:Q_LORA + KV_LORA], w_in[..., Q_LORA + KV_LORA:]
    win = jnp.concatenate([q_lat, ckv, _pad_cols(kpe, LANE), _pad_cols(_swap_halves(kpe, QK_ROPE), LANE)],
                          axis=-1).astype(BF16)
    wq = w_q_b.reshape(n, Q_LORA, H_C, QK_NOPE + QK_ROPE)
    nope = wq[..., :QK_NOPE].reshape(n, Q_LORA, H_C * QK_NOPE)
    rope = wq[..., QK_NOPE:]
    pad = lambda r: _pad_cols(r, LANE).reshape(n, Q_LORA, H_C * LANE)
    wqb = jnp.concatenate([nope, pad(rope), pad(_swap_halves(rope, QK_ROPE))], axis=-1).astype(BF16)
    wkv = w_kv_b.reshape(n, KV_LORA, H_C, QK_NOPE + V_HEAD_C)
    wkvb = jnp.concatenate([wkv[..., :QK_NOPE].reshape(n, KV_LORA, H_C * QK_NOPE),
                            wkv[..., QK_NOPE:].reshape(n, KV_LORA, H_C * V_HEAD_C)], axis=-1).astype(BF16)
    return win, wqb, wkvb


def _rope_tables(tok, tm):
    rows = tok.l_lat // GRID_W
    row = jnp.repeat(jnp.arange(rows), GRID_W).astype(F32)
    col = jnp.tile(jnp.arange(GRID_W), rows).astype(F32)
    n_freq = QK_ROPE // 4
    inv = ROPE_BASE ** (-jnp.arange(n_freq, dtype=F32) / n_freq)
    ang = jnp.concatenate([row[:, None] * inv, col[:, None] * inv], axis=-1)
    cos, sin = jnp.cos(ang), jnp.sin(ang)
    cc = jnp.tile(jnp.concatenate([cos, cos], axis=-1), (1, LANE // QK_ROPE))
    ss = jnp.tile(jnp.concatenate([-sin, sin], axis=-1), (1, LANE // QK_ROPE))
    cc = jnp.concatenate([jnp.ones((tm, LANE), F32), cc], axis=0)
    ss = jnp.concatenate([jnp.zeros((tm, LANE), F32), ss], axis=0)
    return cc, ss


def kernel(x_prompt, x_sample, cache_ckv, cache_kpe, state_gla, state_ret, c, c_ctx, w_ada, b_ada, norm_mix_pre, norm_mix_post, norm_mlp_pre, norm_mlp_post, w_in_even, w_gk2, b_gk2, gla_norm, ret_decay, w_out_even, w_in_odd, q_a_norm, w_q_b, kv_a_norm, w_kv_b, w_out_odd, w_mlp1, w_mlp2):
    nb_ctx, l_ctx, d = x_prompt.shape
    nb_lat, l_lat, _ = x_sample.shape
    depth = w_ada.shape[0]
    tok = _Tokens(nb_ctx, l_ctx, nb_lat, l_lat, d)
    assert nb_lat < MOD_ROWS and tok.n_ctx % l_lat == 0 and l_ctx % CHUNK == 0 and l_lat % (2 * CHUNK) == 0

    cond = jnp.concatenate([c, c_ctx[None, :], jnp.zeros((MOD_ROWS - nb_lat - 1, d), F32)], axis=0)
    mods = _ada_call(cond, w_ada, b_ada).reshape(depth, MOD_ROWS, 6, 1, d)
    rope_even = _rope_tables(tok, tok.tile(EVEN_TILE))
    rope_odd = _rope_tables(tok, tok.tile(MLA_TILE))
    x = (x_prompt.reshape(tok.n_ctx, d), x_sample.reshape(tok.n_lat, d))
    vec = lambda a: a.reshape(1, -1)
    mlp_w = (w_mlp1, w_mlp2)
    even_w = _even_weights(w_in_even, w_gk2, b_gk2) + (jnp.repeat(ret_decay, DK, axis=-1),)
    odd_w = _odd_weights(w_in_odd, w_q_b, w_kv_b)

    new_ckv, new_kpe, new_gla, new_ret = [], [], [], []
    for l in range(depth):
        i = l // 2
        if l % 2 == 0:
            cast_items = [(w, ll) for ll in range(l, min(l + 2, depth)) for w in mlp_w]
            cast_items += [(w_out_even, i)] + ([(w_out_odd, i)] if l + 1 < depth else [])
            q, k, lf, lb, gt, v, *w16 = _even_proj_call(tok, x, mods, l, i, vec(norm_mix_pre[l]), *even_w,
                                                          *rope_even, cast_items)
            gain = jnp.concatenate([jnp.tile(gla_norm[i], N_HEAD_SCAN), jnp.ones((A_V,), F32)]).reshape(1, 2 * A_V)
            s0_lat = jnp.concatenate([state_gla[:, i], state_ret[:, i]], axis=2)
            m_ctx, s_fin = _scan_call(tok, q, k, lf, lb, v, gt, gain)
            (m,) = _scan_call(tok, q, k, lf, lb, m_ctx, gt, gain, s0=s0_lat)
            new_gla.append(s_fin[:, :, :N_HEAD_SCAN])
            new_ret.append(s_fin[:, :, N_HEAD_SCAN:])
            w_out = w16[-2] if l + 1 < depth else w16[-1]
        else:
            win, wqb, wkvb = odd_w
            q, k, v, ckv, kpe = _mla_proj_call(tok, x, mods, l, i, vec(norm_mix_pre[l]), win, vec(q_a_norm[i]), wqb,
                                               vec(kv_a_norm[i]), wkvb, *rope_odd)
            past = cache_ckv.shape[2]
            kpe_pad = jnp.pad(cache_kpe[:, i].reshape(nb_lat * past, QK_ROPE), ((0, 0), (0, LANE - QK_ROPE)))
            k_c, v_c = _cache_expand_call(cache_ckv[:, i].reshape(nb_lat * past, KV_LORA), kpe_pad, wkvb, i)
            m = _attn_lat_call(tok, q, k, _attn_ctx_call(tok, q, k, v), k_c, v_c)
            new_ckv.append(ckv[:tok.n_ctx].reshape(nb_ctx, l_ctx, KV_LORA))
            new_kpe.append(kpe[:tok.n_ctx, :QK_ROPE].reshape(nb_ctx, l_ctx, QK_ROPE))
            w_out = w16[-1]
        w1, w2 = w16[2 * (l % 2):2 * (l % 2) + 2]
        x = _mix_mlp_call(tok, m, x, mods, l, w_out, vec(norm_mix_post[l]), vec(norm_mlp_pre[l]), w1, w2,
                          vec(norm_mlp_post[l]), split_out=(l == depth - 1))
        x = x[0] if len(x) == 1 else tuple(x)

    return (x[0].reshape(nb_ctx, l_ctx, d), x[1].reshape(nb_lat, l_lat, d),
            jnp.stack(new_ckv, axis=1), jnp.stack(new_kpe, axis=1),
            jnp.stack(new_gla, axis=1), jnp.stack(new_ret, axis=1))
```

```python
import functools

import numpy as np
import jax
import jax.numpy as jnp
from jax import lax
from jax.experimental import pallas as pl
from jax.experimental.pallas import tpu as pltpu

F32 = jnp.float32
BF16 = jnp.bfloat16

EPS = 1e-6
ROPE_BASE = 10000.0
GRID_W = 64
CHUNK = 64
GATE_RANK = 16
GATE_NORM = 16.0
N_HEAD_SCAN = 4
DK = 64
DV = 128
H_C = 8
Q_LORA = 256
KV_LORA = 256
QK_NOPE = 128
QK_ROPE = 64
V_HEAD_C = 128
LANE = 128
MOD_ROWS = 16
EVEN_TILE = 512
MLA_TILE = 1024
MIX_TILE = 512
CACHE_TILE = 512
ADA_TILE = 1536

V7X_VMEM_BYTES = 64 * 1024 * 1024
VMEM_LIMIT = V7X_VMEM_BYTES - 8 * 1024 * 1024


def _cparams(sem):
    return pltpu.CompilerParams(dimension_semantics=sem, vmem_limit_bytes=VMEM_LIMIT)


def _dot(a, b):
    return jnp.dot(a, b, preferred_element_type=F32)


def _dot_nt(a, b):
    return lax.dot_general(a, b, (((1,), (1,)), ((), ())), preferred_element_type=F32)


def _dot_tn(a, b):
    return lax.dot_general(a, b, (((0,), (0,)), ((), ())), preferred_element_type=F32)


def _rms(x):
    return x * lax.rsqrt(jnp.mean(x * x, axis=-1, keepdims=True) + EPS)


def _silu(x):
    return x * jax.nn.sigmoid(x)


def _full(shape):
    n = len(shape)
    return pl.BlockSpec(shape, lambda *_: (0,) * n, pipeline_mode=pl.Buffered(1))


def _layer_block(w, idx):
    tail = (0,) * (w.ndim - 1)
    return pl.BlockSpec((None,) + w.shape[1:], lambda *_: (idx,) + tail, pipeline_mode=pl.Buffered(1))


def _side_cast_specs(items, steps):
    nb = 1 << (steps.bit_length() - 1)
    block = lambda i: jnp.minimum(i, nb - 1)
    ins = [pl.BlockSpec((None, w.shape[1] // nb, w.shape[2]), lambda i, layer=layer: (layer, block(i), 0))
           for w, layer in items]
    outs = [pl.BlockSpec((w.shape[1] // nb, w.shape[2]), lambda i: (block(i), 0)) for w, _ in items]
    shapes = [jax.ShapeDtypeStruct(w.shape[1:], BF16) for w, _ in items]
    return ins, outs, shapes


def _side_cast(in_refs, out_refs):
    for src, dst in zip(in_refs, out_refs):
        dst[...] = src[...].astype(BF16)


def _ada_body(cond_ref, w_ref, b_ref, o_ref):
    s = _silu(cond_ref[...]).astype(BF16)
    o_ref[...] = _dot(s, w_ref[...].astype(BF16)) + b_ref[...]


def _ada_call(cond, w_ada, b_ada):
    depth, d, n = w_ada.shape
    tn = ADA_TILE
    return pl.pallas_call(
        _ada_body,
        grid=(depth, n // tn),
        in_specs=[
            pl.BlockSpec((MOD_ROWS, d), lambda l, j: (0, 0)),
            pl.BlockSpec((None, d, tn), lambda l, j: (l, 0, j)),
            pl.BlockSpec((None, 1, tn), lambda l, j: (l, 0, j)),
        ],
        out_specs=pl.BlockSpec((None, MOD_ROWS, tn), lambda l, j: (l, 0, j)),
        out_shape=jax.ShapeDtypeStruct((depth, MOD_ROWS, n), F32),
        compiler_params=_cparams(("arbitrary", "arbitrary")),
        name="ada_mod",
    )(cond, w_ada, b_ada.reshape(depth, 1, n))


class _Tokens:
    def __init__(self, nb_ctx, l_ctx, nb_lat, l_lat, d):
        self.nb_ctx, self.l_ctx, self.nb_lat, self.l_lat, self.d = nb_ctx, l_ctx, nb_lat, l_lat, d
        self.n_ctx = nb_ctx * l_ctx
        self.n_lat = nb_lat * l_lat
        self.n = self.n_ctx + self.n_lat
        self.ctx_row = nb_lat

    def tile(self, want):
        t = want
        while self.n_ctx % t or self.l_lat % t:
            t //= 2
        return t

    def mod_spec(self, layer, chunk, tm):
        n_ctx, l_lat, ctx_row = self.n_ctx, self.l_lat, self.ctx_row

        def idx(i, *_):
            start = i * tm
            row = jnp.where(start < n_ctx, ctx_row, (start - n_ctx) // l_lat)
            return (layer, row, chunk, 0, 0)

        return pl.BlockSpec((None, None, None, 1, self.d), idx)

    def x_specs(self, x, tm):
        if not isinstance(x, tuple):
            return [pl.BlockSpec((tm, self.d), lambda i: (i, 0))], [x]
        nct = self.n_ctx // tm
        return [pl.BlockSpec((tm, self.d), lambda i: (jnp.minimum(i, nct - 1), 0)),
                pl.BlockSpec((tm, self.d), lambda i: (jnp.maximum(i - nct, 0), 0))], list(x)

    def rope_spec(self, tm):
        n_ctx, l_lat = self.n_ctx, self.l_lat

        def idx(i):
            start = i * tm
            return (jnp.where(start < n_ctx, 0, 1 + ((start - n_ctx) % l_lat) // tm), 0)

        return pl.BlockSpec((tm, LANE), idx)


def _read_x(x_refs, n_ctx_tiles):
    if len(x_refs) == 1:
        return x_refs[0][...]
    return jnp.where(pl.program_id(0) < n_ctx_tiles, x_refs[0][...], x_refs[1][...])


A_QK = N_HEAD_SCAN * DK
A_V = N_HEAD_SCAN * DV
E_QA, E_KA, E_VA, E_GA = 0, A_QK, 2 * A_QK, 2 * A_QK + A_V
E_QB = E_GA + A_V
E_KB = E_QB + A_QK
E_VB = E_KB + A_QK
E_GB = E_VB + A_V
E_GK = E_GB + A_V
E_QBS = E_GK + LANE
E_KBS = E_QBS + A_QK


def _log_sigmoid(x):
    return jnp.minimum(x, 0.0) - jnp.log(1.0 + jnp.exp(-jnp.abs(x)))


def _even_proj_body(*refs, n_x, n_ctx_tiles, n_cast):
    x = _read_x(refs[:n_x], n_ctx_tiles)
    g_ref, sh_ref, sc_ref, w_ref, wg_ref, bg_ref, dec_ref, cc_ref, ss_ref = refs[n_x:n_x + 9]
    q_ref, k_ref, lf_ref, lb_ref, gt_ref, v_ref = refs[n_x + 9 + n_cast:n_x + 15 + n_cast]
    _side_cast(refs[n_x + 9:n_x + 9 + n_cast], refs[n_x + 15 + n_cast:])
    tm = x.shape[0]
    h = (_rms(x) * g_ref[...] * (1.0 + sc_ref[...]) + sh_ref[...]).astype(BF16)

    def proj(start, width):
        return _dot(h, w_ref[:, start:start + width])

    gk = proj(E_GK, LANE).astype(BF16)
    la = _log_sigmoid(_dot(gk, wg_ref[...]) + bg_ref[...]) * (1.0 / GATE_NORM)
    lf_ref[:, 0:A_QK] = la[:, 0:A_QK]
    lb_ref[:, 0:A_QK] = la[:, A_QK:2 * A_QK]
    log_g = -jnp.exp(dec_ref[...])
    lf_ref[:, A_QK:2 * A_QK] = jnp.broadcast_to(log_g[0:1, :], (tm, A_QK))
    lb_ref[:, A_QK:2 * A_QK] = jnp.broadcast_to(log_g[1:2, :], (tm, A_QK))
    cc = cc_ref[...]
    ss = ss_ref[...]
    scale = DK ** -0.5
    qb, qbs, kb, kbs = proj(E_QB, A_QK), proj(E_QBS, A_QK), proj(E_KB, A_QK), proj(E_KBS, A_QK)
    for j in range(A_QK // LANE):
        sl = slice(j * LANE, (j + 1) * LANE)
        o = A_QK + j * LANE
        q_ref[:, o:o + LANE] = qb[:, sl] * cc + qbs[:, sl] * ss
        k_ref[:, o:o + LANE] = (kb[:, sl] * cc + kbs[:, sl] * ss) * scale
    q_ref[:, 0:A_QK] = proj(E_QA, A_QK) * scale
    k_ref[:, 0:A_QK] = proj(E_KA, A_QK)
    v_ref[:, 0:A_V] = proj(E_VA, A_V).astype(BF16)
    v_ref[:, A_V:2 * A_V] = proj(E_VB, A_V).astype(BF16)
    gt_ref[:, 0:A_V] = proj(E_GA, A_V)
    gt_ref[:, A_V:2 * A_V] = proj(E_GB, A_V)


def _even_proj_call(tok, x, mods, layer, idx, g_pre, w, wg, bg, dec, cc, ss, cast_items):
    tm = tok.tile(EVEN_TILE)
    d = tok.d
    row = lambda i: (i, 0)
    outs = [
        jax.ShapeDtypeStruct((tok.n, 2 * A_QK), F32),
        jax.ShapeDtypeStruct((tok.n, 2 * A_QK), F32),
        jax.ShapeDtypeStruct((tok.n, 2 * A_QK), F32),
        jax.ShapeDtypeStruct((tok.n, 2 * A_QK), F32),
        jax.ShapeDtypeStruct((tok.n, 2 * A_V), F32),
        jax.ShapeDtypeStruct((tok.n, 2 * A_V), BF16),
    ]
    x_specs, x_args = tok.x_specs(x, tm)
    cast_in, cast_out, cast_shape = _side_cast_specs(cast_items, tok.n // tm)
    return pl.pallas_call(
        functools.partial(_even_proj_body, n_x=len(x_args), n_ctx_tiles=tok.n_ctx // tm, n_cast=len(cast_items)),
        grid=(tok.n // tm,),
        in_specs=x_specs + [
            _full((1, d)),
            tok.mod_spec(layer, 0, tm),
            tok.mod_spec(layer, 1, tm),
            _layer_block(w, idx), _layer_block(wg, idx), _layer_block(bg, idx), _layer_block(dec, idx),
            tok.rope_spec(tm),
            tok.rope_spec(tm),
        ] + cast_in,
        out_specs=[pl.BlockSpec((tm, o.shape[1]), row) for o in outs] + cast_out,
        out_shape=outs + cast_shape,
        compiler_params=_cparams(("arbitrary",)),
        name="even_proj",
    )(*x_args, g_pre, mods, mods, w, wg, bg, dec, cc, ss, *[stack for stack, _ in cast_items])


SCAN_GROUP = 16


def _chunk_cumsum(x):
    row = lax.broadcasted_iota(jnp.int32, x.shape, 0) % CHUNK
    s = 1
    while s < CHUNK:
        x = x + jnp.where(row >= s, pltpu.roll(x, s, axis=0), 0.0)
        s *= 2
    return x


def _scan_body(q_ref, k_ref, lf_ref, lb_ref, v_ref, gt_ref, gain_ref, *rest, seq_len, group, context):
    if context:
        m_ref, sfin_ref, st_ref, o_acc = rest
    else:
        s0_ref, m_ref, st_ref, o_acc = rest
    C = CHUNK
    blk = group * C
    nblk = seq_len // blk
    pair_w = 2 * DK
    head0 = lax.broadcasted_iota(jnp.int32, (blk, pair_w), 1) < DK
    t_in = lax.broadcasted_iota(jnp.int32, (C, pair_w), 0)
    j_in = lax.broadcasted_iota(jnp.int32, (C, pair_w), 1) % DK
    keep_fwd = t_in >= j_in
    keep_bwd = t_in <= j_in
    on_diag = ((lax.broadcasted_iota(jnp.int32, (2 * DV, pair_w), 0) < DV)
               == (lax.broadcasted_iota(jnp.int32, (2 * DV, pair_w), 1) < DK))
    zeros_v = jnp.zeros((C, DV), BF16)

    if context:
        st_ref[...] = jnp.zeros(st_ref.shape, F32)
    else:
        zeros_s = jnp.zeros((DK, DV), F32)
        for d in range(2):
            both = jnp.concatenate([jnp.concatenate([s0_ref[d, 0], zeros_s], axis=1),
                                    jnp.concatenate([zeros_s, s0_ref[d, 1]], axis=1)], axis=0)
            st_ref[d] = both.T

    def one_direction(r0, d, log_ref, keep, reverse):
        g = log_ref[pl.ds(r0, blk), :]
        b = _chunk_cumsum(g)
        tots = [b[C * j + C - 1:C * j + C, :] for j in range(group)]
        totb = jnp.concatenate([jnp.broadcast_to(t, (C, pair_w)) for t in tots], axis=0)
        if reverse:
            b = totb - b + g
        q = q_ref[pl.ds(r0, blk), :]
        k = k_ref[pl.ds(r0, blk), :]
        vblk = v_ref[pl.ds(r0, blk), :]
        q_dec = (q * jnp.exp(b)).astype(BF16)
        k_inv = k * jnp.exp(-b)
        k_up = (k * jnp.exp(totb - b)).astype(BF16)
        k_inv0 = jnp.where(head0, k_inv, 0.0).astype(BF16)
        k_inv1 = jnp.where(head0, 0.0, k_inv).astype(BF16)
        st = st_ref[d]
        outs = [None] * group
        for j in (reversed(range(group)) if reverse else range(group)):
            sl = slice(C * j, C * (j + 1))
            k_bd = jnp.concatenate([k_inv0[sl], k_inv1[sl]], axis=0)
            a = jnp.where(keep, _dot_nt(q_dec[sl], k_bd), 0.0).astype(BF16)
            vc = vblk[sl]
            v_bd = jnp.concatenate([jnp.concatenate([vc[:, :DV], zeros_v], axis=1),
                                    jnp.concatenate([zeros_v, vc[:, DV:]], axis=1)], axis=0)
            outs[j] = _dot_nt(q_dec[sl], st.astype(BF16)) + _dot(a, v_bd)
            st = st * jnp.exp(tots[j]) + jnp.where(on_diag, _dot_tn(vc, k_up[sl]), 0.0)
        st_ref[d] = st
        return jnp.concatenate(outs, axis=0)

    def emit(r0, o):
        gate = _silu(gt_ref[pl.ds(r0, blk), :])
        gain = gain_ref[...]
        for hh in range(2):
            sl = slice(hh * DV, (hh + 1) * DV)
            m_ref[pl.ds(r0, blk), sl] = (_rms(o[:, sl]) * gain[:, sl] * gate[:, sl]).astype(BF16)

    def body(c, carry, second_half):
        rf = pl.multiple_of(c * blk, blk)
        rb = pl.multiple_of((nblk - 1 - c) * blk, blk)
        o_f = one_direction(rf, 0, lf_ref, keep_fwd, False)
        o_b = one_direction(rb, 1, lb_ref, keep_bwd, True)
        if second_half:
            emit(rf, o_acc[pl.ds(rf, blk), :] + o_f)
            emit(rb, o_acc[pl.ds(rb, blk), :] + o_b)
        else:
            o_acc[pl.ds(rf, blk), :] = o_f
            o_acc[pl.ds(rb, blk), :] = o_b
        return carry

    if nblk == 1:
        emit(0, one_direction(0, 0, lf_ref, keep_fwd, False) + one_direction(0, 1, lb_ref, keep_bwd, True))
    else:
        lax.fori_loop(0, nblk // 2, functools.partial(body, second_half=False), 0)
        lax.fori_loop(nblk // 2, nblk, functools.partial(body, second_half=True), 0)

    if context:
        for d in range(2):
            both = st_ref[d].T
            sfin_ref[d, 0] = both[0:DK, 0:DV]
            sfin_ref[d, 1] = both[DK:2 * DK, DV:2 * DV]


def _scan_call(tok, q, k, lf, lb, v, gt, gain, s0=None):
    context = s0 is None
    if context:
        nb, seq_len, blk0 = tok.nb_ctx, tok.l_ctx, 0
    else:
        nb, seq_len, blk0 = tok.nb_lat, tok.l_lat, tok.n_ctx // tok.l_lat
    pairs = q.shape[1] // (2 * DK)
    group = min(SCAN_GROUP, seq_len // CHUNK)
    seq = lambda b, p: (blk0 + b, p)
    st_spec = pl.BlockSpec((None, 2, 2, DK, DV), lambda b, p: (b, 0, p, 0, 0))
    in_specs = [
        pl.BlockSpec((seq_len, 2 * DK), seq),
        pl.BlockSpec((seq_len, 2 * DK), seq),
        pl.BlockSpec((seq_len, 2 * DK), seq),
        pl.BlockSpec((seq_len, 2 * DK), seq),
        pl.BlockSpec((seq_len, 2 * DV), seq),
        pl.BlockSpec((seq_len, 2 * DV), seq),
        pl.BlockSpec((1, 2 * DV), lambda b, p: (0, p)),
    ]
    args = [q, k, lf, lb, v, gt, gain]
    out_shape = [jax.ShapeDtypeStruct(v.shape, v.dtype)]
    out_specs = [pl.BlockSpec((seq_len, 2 * DV), seq)]
    if context:
        out_shape.append(jax.ShapeDtypeStruct((nb, 2, 2 * pairs, DK, DV), F32))
        out_specs.append(st_spec)
    else:
        in_specs.append(st_spec)
        args.append(s0)
    return pl.pallas_call(
        functools.partial(_scan_body, seq_len=seq_len, group=group, context=context),
        grid=(nb, pairs),
        in_specs=in_specs,
        out_specs=out_specs,
        out_shape=out_shape,
        input_output_aliases={4: 0},
        scratch_shapes=[pltpu.VMEM((2, 2 * DV, 2 * DK), F32), pltpu.VMEM((seq_len, 2 * DV), F32)],
        compiler_params=_cparams(("arbitrary", "arbitrary")),
        name="scan_ctx" if context else "scan_lat",
    )(*args)


MIX_ROW_GROUPS = 4


def _mix_mlp_body(*refs, n_x, n_out, n_ctx_tiles):
    x = _read_x(refs[:n_x], n_ctx_tiles)
    (m_ref, wo_ref, gmix_ref, gate1_ref, gpre_ref, sh_ref, sc_ref, w1_ref, w2_ref,
     gpost_ref, gate2_ref) = refs[n_x:len(refs) - n_out]
    out_refs = refs[len(refs) - n_out:]
    rows = x.shape[0] // MIX_ROW_GROUPS
    x1_parts, h_parts = [], []
    for r0 in range(0, x.shape[0], rows):
        y = _dot(m_ref[r0:r0 + rows, :], wo_ref[...])
        x1_g = x[r0:r0 + rows] + gate1_ref[...] * (_rms(y) * gmix_ref[...])
        x1_parts.append(x1_g)
        h_parts.append((_rms(x1_g) * gpre_ref[...] * (1.0 + sc_ref[...]) + sh_ref[...]).astype(BF16))
    x1 = jnp.concatenate(x1_parts, axis=0)
    h = jnp.concatenate(h_parts, axis=0)
    u = jnp.maximum(_dot(h, w1_ref[...]), 0.0)
    u = (u * u).astype(BF16)
    half = x.shape[0] // 2
    res = jnp.concatenate(
        [x1[r0:r0 + half] + gate2_ref[...] * (_rms(_dot(u[r0:r0 + half], w2_ref[...])) * gpost_ref[...])
         for r0 in (0, half)], axis=0)
    if n_out == 1:
        out_refs[0][...] = res
    else:
        is_ctx = pl.program_id(0) < n_ctx_tiles

        @pl.when(is_ctx)
        def _():
            out_refs[0][...] = res

        @pl.when(jnp.logical_not(is_ctx))
        def _():
            out_refs[1][...] = res


def _mix_mlp_call(tok, m, x, mods, layer, w_out, g_mix, g_pre, w1, w2, g_post, split_out=False):
    tm = tok.tile(MIX_TILE)
    d = tok.d
    row = lambda i: (i, 0)
    x_specs, x_args = tok.x_specs(x, tm)
    nct = tok.n_ctx // tm
    if split_out:
        out_specs = [pl.BlockSpec((tm, d), lambda i: (jnp.minimum(i, nct - 1), 0)),
                     pl.BlockSpec((tm, d), lambda i: (jnp.maximum(i - nct, 0), 0))]
        out_shape = [jax.ShapeDtypeStruct((tok.n_ctx, d), F32), jax.ShapeDtypeStruct((tok.n_lat, d), F32)]
    else:
        out_specs = [pl.BlockSpec((tm, d), row)]
        out_shape = [jax.ShapeDtypeStruct((tok.n, d), F32)]
    return pl.pallas_call(
        functools.partial(_mix_mlp_body, n_x=len(x_args), n_out=len(out_shape), n_ctx_tiles=nct),
        grid=(tok.n // tm,),
        in_specs=x_specs + [
            pl.BlockSpec((tm, m.shape[1]), row),
            _full(w_out.shape),
            _full((1, d)),
            tok.mod_spec(layer, 2, tm),
            _full((1, d)),
            tok.mod_spec(layer, 3, tm),
            tok.mod_spec(layer, 4, tm),
            _full(w1.shape),
            _full(w2.shape),
            _full((1, d)),
            tok.mod_spec(layer, 5, tm),
        ],
        out_specs=out_specs,
        out_shape=out_shape,
        compiler_params=_cparams(("arbitrary",)),
        name="mix_mlp",
    )(*x_args, m, w_out, g_mix, mods, g_pre, mods, mods, w1, w2, g_post, mods)


HEAD_W = 2 * LANE
O_QLAT, O_CKV, O_KPE = 0, Q_LORA, Q_LORA + KV_LORA
QB_NOPE, QB_ROPE, QB_SWAP = 0, H_C * LANE, 2 * H_C * LANE


def _expand_kv(cb, kper, wkvb_ref, k_ref, v_ref):
    for hp in range(H_C // 2):
        nope2 = _dot(cb, wkvb_ref[:, hp * 2 * LANE:(hp + 1) * 2 * LANE])
        for j in range(2):
            hh = 2 * hp + j
            k_ref[:, hh * HEAD_W:hh * HEAD_W + LANE] = nope2[:, j * LANE:(j + 1) * LANE].astype(BF16)
            k_ref[:, hh * HEAD_W + LANE:(hh + 1) * HEAD_W] = kper
    v_ref[...] = _dot(cb, wkvb_ref[:, H_C * LANE:2 * H_C * LANE]).astype(BF16)


def _mla_proj_body(x_ref, g_ref, sh_ref, sc_ref, win_ref, qn_ref, wqb_ref, kvn_ref, wkvb_ref, cc_ref, ss_ref,
                   q_ref, k_ref, v_ref, ckv_ref, kpe_ref):
    h =(_rms(x_ref[...]) * g_ref[...] * (1.0 + sc_ref[...]) + sh_ref[...]).astype(BF16)
    cc = cc_ref[...]
    ss = ss_ref[...]
    qn = (_rms(_dot(h, win_ref[:, O_QLAT:O_QLAT + Q_LORA])) * qn_ref[...]).astype(BF16)
    ckvn = _rms(_dot(h, win_ref[:, O_CKV:O_CKV + KV_LORA])) * kvn_ref[...]
    kpe2 = _dot(h, win_ref[:, O_KPE:O_KPE + 2 * LANE])
    kpe = kpe2[:, 0:LANE]
    kper = (kpe * cc + kpe2[:, LANE:2 * LANE] * ss).astype(BF16)
    ckv_ref[...] = ckvn
    kpe_ref[...] = kpe
    for hp in range(H_C // 2):
        o = hp * 2 * LANE
        nope2 = _dot(qn, wqb_ref[:, QB_NOPE + o:QB_NOPE + o + 2 * LANE])
        rope2 = _dot(qn, wqb_ref[:, QB_ROPE + o:QB_ROPE + o + 2 * LANE])
        swap2 = _dot(qn, wqb_ref[:, QB_SWAP + o:QB_SWAP + o + 2 * LANE])
        for j in range(2):
            hh = 2 * hp + j
            sl = slice(j * LANE, (j + 1) * LANE)
            rot = rope2[:, sl] * cc + swap2[:, sl] * ss
            q_ref[:, hh * HEAD_W:hh * HEAD_W + LANE] = (nope2[:, sl] * ATTN_Q_SCALE).astype(BF16)
            q_ref[:, hh * HEAD_W + LANE:(hh + 1) * HEAD_W] = (rot * ATTN_Q_SCALE).astype(BF16)
    _expand_kv(ckvn.astype(BF16), kper, wkvb_ref, k_ref, v_ref)


def _mla_proj_call(tok, x, mods, layer, idx, g_pre, win, qn, wqb, kvn, wkvb, cc, ss):
    tm = tok.tile(MLA_TILE)
    d = tok.d
    row = lambda i: (i, 0)
    outs = [
        jax.ShapeDtypeStruct((tok.n, H_C * HEAD_W), BF16),
        jax.ShapeDtypeStruct((tok.n, H_C * HEAD_W), BF16),
        jax.ShapeDtypeStruct((tok.n, H_C * V_HEAD_C), BF16),
        jax.ShapeDtypeStruct((tok.n, KV_LORA), F32),
        jax.ShapeDtypeStruct((tok.n, LANE), F32),
    ]
    return pl.pallas_call(
        _mla_proj_body,
        grid=(tok.n // tm,),
        in_specs=[
            pl.BlockSpec((tm, d), row),
            _full((1, d)),
            tok.mod_spec(layer, 0, tm),
            tok.mod_spec(layer, 1, tm),
            _layer_block(win, idx), _full(qn.shape), _layer_block(wqb, idx), _full(kvn.shape),
            _layer_block(wkvb, idx),
            tok.rope_spec(tm),
            tok.rope_spec(tm),
        ],
        out_specs=[pl.BlockSpec((tm, o.shape[1]), row) for o in outs],
        out_shape=outs,
        compiler_params=_cparams(("arbitrary",)),
        name="mla_proj",
    )(x, g_pre, mods, mods, win, qn, wqb, kvn, wkvb, cc, ss)


def _cache_expand_body(ckv_ref, kpe_ref, wkvb_ref, k_ref, v_ref):
    _expand_kv(ckv_ref[...].astype(BF16), kpe_ref[...].astype(BF16), wkvb_ref, k_ref, v_ref)


def _cache_expand_call(ckv, kpe_pad, wkvb, idx):
    n = ckv.shape[0]
    tm = CACHE_TILE
    while n % tm:
        tm //= 2
    row = lambda i: (i, 0)
    outs = [jax.ShapeDtypeStruct((n, H_C * HEAD_W), BF16), jax.ShapeDtypeStruct((n, H_C * V_HEAD_C), BF16)]
    return pl.pallas_call(
        _cache_expand_body,
        grid=(n // tm,),
        in_specs=[pl.BlockSpec((tm, KV_LORA), row), pl.BlockSpec((tm, LANE), row), _layer_block(wkvb, idx)],
        out_specs=[pl.BlockSpec((tm, o.shape[1]), row) for o in outs],
        out_shape=outs,
        compiler_params=_cparams(("arbitrary",)),
        name="cache_expand",
    )(ckv, kpe_pad, wkvb)


ATTN_TQ = 1024
ATTN_TK = 512
ATTN_HEADS_PER_STEP = 2
ATTN_Q_SCALE = (QK_NOPE + QK_ROPE) ** -0.5 * float(np.log2(np.e))


def _attn_ctx_body(q_ref, k_ref, v_ref, o_ref):
    for hh in range(H_C):
        s = _dot_nt(q_ref[:, hh * HEAD_W:(hh + 1) * HEAD_W], k_ref[:, hh * HEAD_W:(hh + 1) * HEAD_W])
        p = jnp.exp2(s - jnp.max(s, axis=-1, keepdims=True))
        den = jnp.sum(p, axis=-1, keepdims=True)
        acc = _dot(p.astype(BF16), v_ref[:, hh * V_HEAD_C:(hh + 1) * V_HEAD_C])
        o_ref[:, hh * V_HEAD_C:(hh + 1) * V_HEAD_C] = (acc / den).astype(BF16)


def _lane_groups(x):
    return [x[:, g:g + LANE] for g in range(0, x.shape[1], LANE)]


def _attn_lat_body(q_ref, kc_ref, vc_ref, k_ref, v_ref, o_ref, sa_ref, sb_ref, ma_ref, mb_ref, vx_ref):
    past, n_self = kc_ref.shape[0], k_ref.shape[0]
    heads = vx_ref.shape[0]
    tq = sa_ref.shape[0]
    n = q_ref.shape[0] // tq
    tk = min(ATTN_TK, n_self)
    blocks = [(kc_ref, 0, past, 0)] + [(k_ref, k0, tk, past + k0) for k0 in range(0, n_self, tk)]

    for h in range(heads):
        vx_ref[h, 0:past, 0:V_HEAD_C] = vc_ref[:, h * V_HEAD_C:(h + 1) * V_HEAD_C]
        vx_ref[h, past:past + n_self, 0:V_HEAD_C] = v_ref[:, h * V_HEAD_C:(h + 1) * V_HEAD_C]
        vx_ref[h, :, V_HEAD_C:2 * V_HEAD_C] = jnp.ones((past + n_self, V_HEAD_C), BF16)

    def scores(h, i, s_ref, m_ref):
        q = q_ref[pl.ds(pl.multiple_of(i * tq, tq), tq), h * HEAD_W:(h + 1) * HEAD_W]
        mx = None
        for kk_ref, k0, size, col in blocks:
            s = _dot_nt(q, kk_ref[k0:k0 + size, h * HEAD_W:(h + 1) * HEAD_W])
            s_ref[:, col:col + size] = s
            mx = functools.reduce(jnp.maximum, _lane_groups(s) + ([] if mx is None else [mx]))
        m_ref[...] = mx

    def finish(h, i, s_ref, m_ref):
        m = jnp.max(m_ref[...], axis=-1, keepdims=True)
        acc = None
        for _, _, size, col in blocks:
            p = jnp.exp2(s_ref[:, col:col + size] - m).astype(BF16)
            pv = _dot(p, vx_ref[h, col:col + size, :])
            acc = pv if acc is None else acc + pv
        out = acc[:, 0:V_HEAD_C] / acc[:, V_HEAD_C:2 * V_HEAD_C]
        o_ref[pl.ds(pl.multiple_of(i * tq, tq), tq), h * V_HEAD_C:(h + 1) * V_HEAD_C] = out.astype(BF16)

    scores(0, 0, sa_ref, ma_ref)
    for h in range(heads):

        def two_tiles(j, carry, h=h):
            i = 2 * j
            scores(h, i + 1, sb_ref, mb_ref)
            finish(h, i, sa_ref, ma_ref)
            scores(h, i + 2, sa_ref, ma_ref)
            finish(h, i + 1, sb_ref, mb_ref)
            return carry

        lax.fori_loop(0, n // 2 - 1, two_tiles, 0)
        scores(h, n - 1, sb_ref, mb_ref)
        finish(h, n - 2, sa_ref, ma_ref)
        if h + 1 < heads:
            scores(h + 1, 0, sa_ref, ma_ref)
        finish(h, n - 1, sb_ref, mb_ref)


def _attn_ctx_call(tok, q, k, v):
    seq = lambda b: (b, 0)
    return pl.pallas_call(
        _attn_ctx_body,
        grid=(tok.nb_ctx,),
        in_specs=[pl.BlockSpec((tok.l_ctx, H_C * HEAD_W), seq),
                  pl.BlockSpec((tok.l_ctx, H_C * HEAD_W), seq),
                  pl.BlockSpec((tok.l_ctx, H_C * V_HEAD_C), seq)],
        out_specs=pl.BlockSpec((tok.l_ctx, H_C * V_HEAD_C), seq),
        out_shape=jax.ShapeDtypeStruct(v.shape, v.dtype),
        input_output_aliases={2: 0},
        compiler_params=_cparams(("arbitrary",)),
        name="attn_ctx",
    )(q, k, v)


def _attn_lat_call(tok, q, k, v, k_cache, v_cache):
    blk0 = tok.n_ctx // tok.l_lat
    past = k_cache.shape[0] // tok.nb_lat
    tq = min(ATTN_TQ, tok.l_lat // 2)
    hg = ATTN_HEADS_PER_STEP
    seq = lambda b, hh: (blk0 + b, hh)
    cache = lambda b, hh: (b, hh)
    return pl.pallas_call(
        _attn_lat_body,
        grid=(tok.nb_lat, H_C // hg),
        in_specs=[pl.BlockSpec((tok.l_lat, hg * HEAD_W), seq),
                  pl.BlockSpec((past, hg * HEAD_W), cache),
                  pl.BlockSpec((past, hg * V_HEAD_C), cache),
                  pl.BlockSpec((tok.l_lat, hg * HEAD_W), seq),
                  pl.BlockSpec((tok.l_lat, hg * V_HEAD_C), seq)],
        out_specs=pl.BlockSpec((tok.l_lat, hg * V_HEAD_C), seq),
        out_shape=jax.ShapeDtypeStruct(v.shape, v.dtype),
        input_output_aliases={4: 0},
        scratch_shapes=([pltpu.VMEM((tq, past + tok.l_lat), F32)] * 2 + [pltpu.VMEM((tq, LANE), F32)] * 2
                        + [pltpu.VMEM((hg, past + tok.l_lat, 2 * V_HEAD_C), BF16)]),
        compiler_params=_cparams(("arbitrary", "arbitrary")),
        name="attn_lat",
    )(q, k_cache, v_cache, k, v)


def _swap_halves(w, head_dim):
    lead = w.shape[:-1]
    halves = w.reshape(lead + (w.shape[-1] // head_dim, 2, head_dim // 2))
    return halves[..., ::-1, :].reshape(w.shape)


def _pad_cols(w, width):
    return jnp.pad(w, ((0, 0),) * (w.ndim - 1) + ((0, width - w.shape[-1]),))


def _even_weights(w_in, w_gk2, b_gk2):
    n = w_in.shape[0]
    sizes = (A_QK, A_QK, A_V, A_V, 2 * GATE_RANK, A_QK, A_QK, A_V, A_V)
    qa, ka, va, ga, gk, qb, kb, vb, gb = jnp.split(w_in, np.cumsum(sizes)[:-1].tolist(), axis=-1)
    w = jnp.concatenate([qa, ka, va, ga, qb, kb, vb, gb, _pad_cols(gk, LANE),
                         _swap_halves(qb, DK), _swap_halves(kb, DK)], axis=-1).astype(BF16)
    wg = jnp.zeros((n, LANE, 2 * A_QK), F32)
    wg = wg.at[:, 0:GATE_RANK, 0:A_QK].set(w_gk2[:, 0]).at[:, GATE_RANK:2 * GATE_RANK, A_QK:2 * A_QK].set(w_gk2[:, 1])
    bg = b_gk2.reshape(n, 1, 2 * A_QK)
    return w, wg.astype(BF16), bg


def _odd_weights(w_in, w_q_b, w_kv_b):
    n = w_in.shape[0]
    q_lat, ckv, kpe = w_in[..., :Q_LORA], w_in[..., Q_LORA:Q_LORA + KV_LORA], w_in[..., Q_LORA + KV_LORA:]
    win = jnp.concatenate([q_lat, ckv, _pad_cols(kpe, LANE), _pad_cols(_swap_halves(kpe, QK_ROPE), LANE)],
                          axis=-1).astype(BF16)
    wq = w_q_b.reshape(n, Q_LORA, H_C, QK_NOPE + QK_ROPE)
    nope = wq[..., :QK_NOPE].reshape(n, Q_LORA, H_C * QK_NOPE)
    rope = wq[..., QK_NOPE:]
    pad = lambda r: _pad_cols(r, LANE).reshape(n, Q_LORA, H_C * LANE)
    wqb = jnp.concatenate([nope, pad(rope), pad(_swap_halves(rope, QK_ROPE))], axis=-1).astype(BF16)
    wkv = w_kv_b.reshape(n, KV_LORA, H_C, QK_NOPE + V_HEAD_C)
    wkvb = jnp.concatenate([wkv[..., :QK_NOPE].reshape(n, KV_LORA, H_C * QK_NOPE),
                            wkv[..., QK_NOPE:].reshape(n, KV_LORA, H_C * V_HEAD_C)], axis=-1).astype(BF16)
    return win, wqb, wkvb


def _rope_tables(tok, tm):
    rows = tok.l_lat // GRID_W
    row = jnp.repeat(jnp.arange(rows), GRID_W).astype(F32)
    col = jnp.tile(jnp.arange(GRID_W), rows).astype(F32)
    n_freq = QK_ROPE // 4
    inv = ROPE_BASE ** (-jnp.arange(n_freq, dtype=F32) / n_freq)
    ang = jnp.concatenate([row[:, None] * inv, col[:, None] * inv], axis=-1)
    cos, sin = jnp.cos(ang), jnp.sin(ang)
    cc = jnp.tile(jnp.concatenate([cos, cos], axis=-1), (1, LANE // QK_ROPE))
    ss = jnp.tile(jnp.concatenate([-sin, sin], axis=-1), (1, LANE // QK_ROPE))
    cc = jnp.concatenate([jnp.ones((tm, LANE), F32), cc], axis=0)
    ss = jnp.concatenate([jnp.zeros((tm, LANE), F32), ss], axis=0)
    return cc, ss


def kernel(x_prompt, x_sample, cache_ckv, cache_kpe, state_gla, state_ret, c, c_ctx, w_ada, b_ada, norm_mix_pre, norm_mix_post, norm_mlp_pre, norm_mlp_post, w_in_even, w_gk2, b_gk2, gla_norm, ret_decay, w_out_even, w_in_odd, q_a_norm, w_q_b, kv_a_norm, w_kv_b, w_out_odd, w_mlp1, w_mlp2):
    nb_ctx, l_ctx, d = x_prompt.shape
    nb_lat, l_lat, _ = x_sample.shape
    depth = w_ada.shape[0]
    tok = _Tokens(nb_ctx, l_ctx, nb_lat, l_lat, d)
    assert nb_lat < MOD_ROWS and tok.n_ctx % l_lat == 0 and l_ctx % CHUNK == 0 and l_lat % (2 * CHUNK) == 0

    cond = jnp.concatenate([c, c_ctx[None, :], jnp.zeros((MOD_ROWS - nb_lat - 1, d), F32)], axis=0)
    mods = _ada_call(cond, w_ada, b_ada).reshape(depth, MOD_ROWS, 6, 1, d)
    rope_even = _rope_tables(tok, tok.tile(EVEN_TILE))
    rope_odd = _rope_tables(tok, tok.tile(MLA_TILE))
    x = (x_prompt.reshape(tok.n_ctx, d), x_sample.reshape(tok.n_lat, d))
    vec = lambda a: a.reshape(1, -1)
    mlp_w = (w_mlp1, w_mlp2)
    even_w = _even_weights(w_in_even, w_gk2, b_gk2) + (jnp.repeat(ret_decay, DK, axis=-1),)
    odd_w = _odd_weights(w_in_odd, w_q_b, w_kv_b)

    new_ckv, new_kpe, new_gla, new_ret = [], [], [], []
    for l in range(depth):
        i = l // 2
        if l % 2 == 0:
            cast_items = [(w, ll) for ll in range(l, min(l + 2, depth)) for w in mlp_w]
            cast_items += [(w_out_even, i)] + ([(w_out_odd, i)] if l + 1 < depth else [])
            q, k, lf, lb, gt, v, *w16 = _even_proj_call(tok, x, mods, l, i, vec(norm_mix_pre[l]), *even_w,
                                                          *rope_even, cast_items)
            gain = jnp.concatenate([jnp.tile(gla_norm[i], N_HEAD_SCAN), jnp.ones((A_V,), F32)]).reshape(1, 2 * A_V)
            s0_lat = jnp.concatenate([state_gla[:, i], state_ret[:, i]], axis=2)
            m_ctx, s_fin = _scan_call(tok, q, k, lf, lb, v, gt, gain)
            (m,) = _scan_call(tok, q, k, lf, lb, m_ctx, gt, gain, s0=s0_lat)
            new_gla.append(s_fin[:, :, :N_HEAD_SCAN])
            new_ret.append(s_fin[:, :, N_HEAD_SCAN:])
            w_out = w16[-2] if l + 1 < depth else w16[-1]
        else:
            win, wqb, wkvb = odd_w
            q, k, v, ckv, kpe = _mla_proj_call(tok, x, mods, l, i, vec(norm_mix_pre[l]), win, vec(q_a_norm[i]), wqb,
                                               vec(kv_a_norm[i]), wkvb, *rope_odd)
            past = cache_ckv.shape[2]
            kpe_pad = jnp.pad(cache_kpe[:, i].reshape(nb_lat * past, QK_ROPE), ((0, 0), (0, LANE - QK_ROPE)))
            k_c, v_c = _cache_expand_call(cache_ckv[:, i].reshape(nb_lat * past, KV_LORA), kpe_pad, wkvb, i)
            m = _attn_lat_call(tok, q, k, _attn_ctx_call(tok, q, k, v), k_c, v_c)
            new_ckv.append(ckv[:tok.n_ctx].reshape(nb_ctx, l_ctx, KV_LORA))
            new_kpe.append(kpe[:tok.n_ctx, :QK_ROPE].reshape(nb_ctx, l_ctx, QK_ROPE))
            w_out = w16[-1]
        w1, w2 = w16[2 * (l % 2):2 * (l % 2) + 2]
        x = _mix_mlp_call(tok, m, x, mods, l, w_out, vec(norm_mix_post[l]), vec(norm_mlp_pre[l]), w1, w2,
                          vec(norm_mlp_post[l]), split_out=(l == depth - 1))
        x = x[0] if len(x) == 1 else tuple(x)

    return (x[0].reshape(nb_ctx, l_ctx, d), x[1].reshape(nb_lat, l_lat, d),
            jnp.stack(new_ckv, axis=1), jnp.stack(new_kpe, axis=1),
            jnp.stack(new_gla, axis=1), jnp.stack(new_ret, axis=1))
```
